```python
import math
import jax, jax.numpy as jnp
from jax import lax
import numpy as np

D_MODEL = 2048
BATCH = 2
SEQ = 4096
DEPTH = 1

GRID_W = 64
HEAD_DIM = 128
NA_HEADS = 8
NA_WIDTH = NA_HEADS * HEAD_DIM
NA_WIN_ROWS_MAX = 8
NA_WIN_COLS = 16
DA_HEADS = 4
DA_VDIM = 2 * HEAD_DIM
DA_WIDTH = DA_HEADS * DA_VDIM
DA_QK = DA_HEADS * HEAD_DIM
DA_Q_BLOCK = 128
DA_LAYER_LAMBDA_BASE = 0.8
MIX_WIDTH = NA_WIDTH + DA_WIDTH
IN_COLS = 3 * NA_WIDTH + 4 * DA_QK + DA_WIDTH
N_EXPERTS = 16
EC_CAPACITY_FACTOR = 2
EXPERT_FF = 2816
RMS_EPS = 1e-6

kernel_name = "hybrid_natten_diffattn_ec_moe"


def rms_norm(x, g):
    xf = x.astype(jnp.float32)
    y = xf * lax.rsqrt(jnp.mean(xf * xf, axis=-1, keepdims=True) + RMS_EPS)
    return (y * g.astype(jnp.float32)).astype(x.dtype)


def lambda_init(layer_idx):
    return DA_LAYER_LAMBDA_BASE - 0.6 * math.exp(-0.3 * layer_idx)


def neighborhood_attention(q, k, v, rpb):
    B, S, H, d = q.shape
    rows = S // GRID_W
    kh = min(NA_WIN_ROWS_MAX, rows)
    kw = NA_WIN_COLS
    qg = q.reshape(B, rows, GRID_W, H, d)
    kg = k.reshape(B, rows, GRID_W, H, d)
    vg = v.reshape(B, rows, GRID_W, H, d)
    r = jnp.arange(rows)
    rs = jnp.clip(r - kh // 2, 0, rows - kh)
    row_idx = rs[:, None] + jnp.arange(kh)[None, :]
    k_rows = kg[:, row_idx]
    v_rows = vg[:, row_idx]
    c = jnp.arange(GRID_W)
    cs = jnp.clip(c - kw // 2, 0, GRID_W - kw)
    col_valid = (c[None, :] >= cs[:, None]) & (c[None, :] < cs[:, None] + kw)
    row_off = row_idx - r[:, None] + (NA_WIN_ROWS_MAX - 1)
    col_off = jnp.clip(c[None, :] - c[:, None], -(kw - 1), kw - 1) + (kw - 1)
    bias = rpb[:, row_off[:, None, :, None], col_off[None, :, None, :]]
    s = jnp.einsum('brqhd,brkchd->bhrqkc', qg, k_rows).astype(jnp.float32) * (d ** -0.5)
    s = s + bias.astype(jnp.float32)[None]
    s = jnp.where(col_valid[:, None, :], s, -jnp.inf)
    p = jax.nn.softmax(s.reshape(B, H, rows, GRID_W, kh * GRID_W), axis=-1)
    p = p.reshape(s.shape).astype(v.dtype)
    o = jnp.einsum('bhrqkc,brkchd->brqhd', p, v_rows)
    return o.reshape(B, S, H * d)


def differential_attention(q1, q2, k1, k2, v, lam, slopes):
    B, S, H, d = q1.shape
    nb = S // DA_Q_BLOCK
    scale = d ** -0.5
    pos = jnp.arange(S, dtype=jnp.float32)

    def to_blocks(t):
        return t.reshape(B, nb, DA_Q_BLOCK, H, d).transpose(1, 0, 2, 3, 4)

    def block(args):
        q1b, q2b, tb = args
        dist = jnp.abs(tb[:, None] - pos[None, :])
        alibi = -slopes[:, None, None] * dist[None]
        s1 = jnp.einsum('bqhd,bkhd->bhqk', q1b, k1).astype(jnp.float32) * scale + alibi
        s2 = jnp.einsum('bqhd,bkhd->bhqk', q2b, k2).astype(jnp.float32) * scale + alibi
        w = jax.nn.softmax(s1, axis=-1) - lam * jax.nn.softmax(s2, axis=-1)
        return jnp.einsum('bhqk,bkhe->bqhe', w.astype(v.dtype), v)

    o = lax.map(block, (to_blocks(q1), to_blocks(q2), pos.reshape(nb, DA_Q_BLOCK)))
    return o.transpose(1, 0, 2, 3, 4).reshape(B, S, H, v.shape[-1])


def expert_choice_ffn(h, w_router, w_gate, w_up, w_down):
    B, S, D = h.shape
    n_exp = w_router.shape[-1]
    cap = EC_CAPACITY_FACTOR * S // n_exp
    aff = jax.nn.softmax((h @ w_router).astype(jnp.float32), axis=-1)
    gate, idx = lax.top_k(aff.transpose(0, 2, 1), cap)
    bidx = jnp.arange(B)[:, None, None]
    xs = h[bidx, idx]
    g = jnp.einsum('becd,edf->becf', xs, w_gate)
    u = jnp.einsum('becd,edf->becf', xs, w_up)
    y = jnp.einsum('becf,efd->becd', jax.nn.silu(g) * u, w_down)
    y = y * gate[..., None].astype(h.dtype)
    return jnp.zeros_like(h).at[bidx, idx].add(y)


def setup_inputs(seed: int = 0) -> dict:
    key = jax.random.key(seed)
    ks = jax.random.split(key, 20)
    f32 = jnp.float32
    L = DEPTH

    def nrm(k, shape, scale):
        return jax.random.normal(k, shape, f32) * scale

    def gain(k, shape):
        return 1.0 + 0.02 * jax.random.normal(k, shape, f32)

    return {
        "x": jax.random.normal(ks[0], (BATCH, SEQ, D_MODEL), f32),
        "ln1_g": gain(ks[1], (L, D_MODEL)),
        "w_in": nrm(ks[2], (L, D_MODEL, IN_COLS), D_MODEL ** -0.5),
        "qn_a": gain(ks[3], (L, HEAD_DIM)),
        "kn_a": gain(ks[4], (L, HEAD_DIM)),
        "rpb_a": nrm(ks[5], (L, NA_HEADS, 2 * NA_WIN_ROWS_MAX - 1, 2 * NA_WIN_COLS - 1), 0.5),
        "on_a": gain(ks[6], (L, NA_WIDTH)),
        "qn_b": gain(ks[7], (L, HEAD_DIM)),
        "kn_b": gain(ks[8], (L, HEAD_DIM)),
        "lam_q1": nrm(ks[9], (L, HEAD_DIM), 0.1),
        "lam_k1": nrm(ks[10], (L, HEAD_DIM), 0.1),
        "lam_q2": nrm(ks[11], (L, HEAD_DIM), 0.1),
        "lam_k2": nrm(ks[12], (L, HEAD_DIM), 0.1),
        "subln_b": gain(ks[13], (L, DA_VDIM)),
        "w_out": nrm(ks[14], (L, MIX_WIDTH, D_MODEL), MIX_WIDTH ** -0.5),
        "ln2_g": gain(ks[15], (L, D_MODEL)),
        "w_router": nrm(ks[16], (L, D_MODEL, N_EXPERTS), D_MODEL ** -0.5),
        "w_gate": nrm(ks[17], (L, N_EXPERTS, D_MODEL, EXPERT_FF), D_MODEL ** -0.5),
        "w_up": nrm(ks[18], (L, N_EXPERTS, D_MODEL, EXPERT_FF), D_MODEL ** -0.5),
        "w_down": nrm(ks[19], (L, N_EXPERTS, EXPERT_FF, D_MODEL), EXPERT_FF ** -0.5),
    }


def reference(x, ln1_g, w_in, qn_a, kn_a, rpb_a, on_a, qn_b, kn_b, lam_q1, lam_k1,
              lam_q2, lam_k2, subln_b, w_out, ln2_g, w_router, w_gate, w_up, w_down):
    B, S, _ = x.shape
    splits = [NA_WIDTH, 2 * NA_WIDTH, 3 * NA_WIDTH,
              3 * NA_WIDTH + DA_QK, 3 * NA_WIDTH + 2 * DA_QK,
              3 * NA_WIDTH + 3 * DA_QK, 3 * NA_WIDTH + 4 * DA_QK]
    slopes = 2.0 ** (-8.0 * jnp.arange(1, DA_HEADS + 1, dtype=jnp.float32) / DA_HEADS)
    for l in range(DEPTH):
        h = rms_norm(x, ln1_g[l])
        proj = h @ w_in[l]
        qa, ka, va, q1, q2, k1, k2, vb = jnp.split(proj, splits, axis=-1)
        qa = rms_norm(qa.reshape(B, S, NA_HEADS, HEAD_DIM), qn_a[l])
        ka = rms_norm(ka.reshape(B, S, NA_HEADS, HEAD_DIM), kn_a[l])
        va = va.reshape(B, S, NA_HEADS, HEAD_DIM)
        oa = rms_norm(neighborhood_attention(qa, ka, va, rpb_a[l]), on_a[l])
        lam_i = lambda_init(l)
        lam = (jnp.exp(jnp.sum(lam_q1[l].astype(jnp.float32) * lam_k1[l].astype(jnp.float32)))
               - jnp.exp(jnp.sum(lam_q2[l].astype(jnp.float32) * lam_k2[l].astype(jnp.float32)))
               + lam_i)
        q1 = rms_norm(q1.reshape(B, S, DA_HEADS, HEAD_DIM), qn_b[l])
        q2 = rms_norm(q2.reshape(B, S, DA_HEADS, HEAD_DIM), qn_b[l])
        k1 = rms_norm(k1.reshape(B, S, DA_HEADS, HEAD_DIM), kn_b[l])
        k2 = rms_norm(k2.reshape(B, S, DA_HEADS, HEAD_DIM), kn_b[l])
        vb = vb.reshape(B, S, DA_HEADS, DA_VDIM)
        ob = differential_attention(q1, q2, k1, k2, vb, lam, slopes)
        ob = (rms_norm(ob, subln_b[l]) * (1.0 - lam_i)).reshape(B, S, DA_WIDTH)
        x = x + jnp.concatenate([oa, ob], axis=-1) @ w_out[l]
        h2 = rms_norm(x, ln2_g[l])
        x = x + expert_choice_ffn(h2, w_router[l], w_gate[l], w_up[l], w_down[l])
    return x
```

```python
import functools
import math

import jax
import jax.numpy as jnp
from jax import lax
from jax.experimental import pallas as pl
from jax.experimental.pallas import tpu as pltpu

F32 = jnp.float32
BF16 = jnp.bfloat16

GRID_W = 64
HEAD_DIM = 128
NA_HEADS = 8
NA_WIDTH = NA_HEADS * HEAD_DIM
NA_WIN_ROWS = 8
NA_WIN_COLS = 16
DA_HEADS = 4
DA_VDIM = 2 * HEAD_DIM
DA_WIDTH = DA_HEADS * DA_VDIM
DA_QK = DA_HEADS * HEAD_DIM
DA_LAYER_LAMBDA_BASE = 0.8
IN_COLS = 3 * NA_WIDTH + 4 * DA_QK + DA_WIDTH
N_EXPERTS = 16
EC_CAPACITY_FACTOR = 2
RMS_EPS = 1e-6
ATTN_SCALE = HEAD_DIM ** -0.5

V7X_LANES = 128
V7X_VMEM_LIMIT_BYTES = 56 * 1024 * 1024

NT_DIMS = (((1,), (1,)), ((), ()))


def _lambda_init(layer_idx):
    return DA_LAYER_LAMBDA_BASE - 0.6 * math.exp(-0.3 * layer_idx)


def _rms(y):
    return y * lax.rsqrt(jnp.mean(y * y, axis=-1, keepdims=True) + RMS_EPS)


def _params(*semantics):
    return pltpu.CompilerParams(dimension_semantics=semantics,
                                vmem_limit_bytes=V7X_VMEM_LIMIT_BYTES)


def _inproj_kernel(x_ref, g1_ref, w_ref, gain_ref, o_ref, h_ref, *, tn):
    j = pl.program_id(1)

    @pl.when(j == 0)
    def _():
        h_ref[...] = (_rms(x_ref[...]) * g1_ref[...]).astype(BF16)

    acc = jnp.dot(h_ref[...], w_ref[...], preferred_element_type=F32)
    col0 = j * tn
    qk_cols = (col0 < 2 * NA_WIDTH) | ((col0 >= 3 * NA_WIDTH) & (col0 < 3 * NA_WIDTH + 4 * DA_QK))

    @pl.when(qk_cols)
    def _():
        for c in range(tn // HEAD_DIM):
            sl = slice(c * HEAD_DIM, (c + 1) * HEAD_DIM)
            o_ref[:, sl] = (_rms(acc[:, sl]) * gain_ref[:, sl]).astype(o_ref.dtype)

    @pl.when(jnp.logical_not(qk_cols))
    def _():
        o_ref[...] = acc.astype(o_ref.dtype)


def _in_projection(x2d, ln1_g, w_in, col_gain, *, tm, tn):
    n, d = x2d.shape
    cols = w_in.shape[1]
    return pl.pallas_call(
        functools.partial(_inproj_kernel, tn=tn),
        grid=(n // tm, cols // tn),
        in_specs=[
            pl.BlockSpec((tm, d), lambda i, j: (i, 0)),
            pl.BlockSpec((1, d), lambda i, j: (0, 0)),
            pl.BlockSpec((d, tn), lambda i, j: (0, j)),
            pl.BlockSpec((1, tn), lambda i, j: (0, j)),
        ],
        out_specs=pl.BlockSpec((tm, tn), lambda i, j: (i, j)),
        out_shape=jax.ShapeDtypeStruct((n, cols), BF16),
        scratch_shapes=[pltpu.VMEM((tm, d), BF16)],
        compiler_params=_params("parallel", "arbitrary"),
        name="in_projection",
    )(x2d, ln1_g.reshape(1, d), w_in, col_gain)


def _na_bias_table(rpb):
    c = jnp.arange(GRID_W)
    col_off = jnp.clip(c[None, :] - c[:, None], -(NA_WIN_COLS - 1), NA_WIN_COLS - 1) + (NA_WIN_COLS - 1)
    row_off = jnp.arange(NA_WIN_ROWS)[:, None] + jnp.arange(NA_WIN_ROWS)[None, :]
    t = rpb[:, row_off[:, None, :, None], col_off[None, :, None, :]]
    return t.reshape(rpb.shape[0], NA_WIN_ROWS, GRID_W, NA_WIN_ROWS * GRID_W)


def _na_kernel(q_ref, k_ref, v_ref, bias_ref, o_ref, *, rows):
    win = NA_WIN_ROWS * GRID_W
    cq = lax.broadcasted_iota(jnp.int32, (GRID_W, win), 0)
    ck = lax.broadcasted_iota(jnp.int32, (GRID_W, win), 1) % GRID_W
    cs = jnp.clip(cq - NA_WIN_COLS // 2, 0, GRID_W - NA_WIN_COLS)
    col_valid = (ck >= cs) & (ck < cs + NA_WIN_COLS)

    def body(r, carry):
        rs = jnp.clip(r - NA_WIN_ROWS // 2, 0, rows - NA_WIN_ROWS)
        base = rs - r + (NA_WIN_ROWS - 1)
        q0 = pl.multiple_of(r * GRID_W, GRID_W)
        k0 = pl.multiple_of(rs * GRID_W, GRID_W)
        q = q_ref[pl.ds(q0, GRID_W), :]
        kw = k_ref[pl.ds(k0, win), :]
        vw = v_ref[pl.ds(k0, win), :]
        s = lax.dot_general(q, kw, NT_DIMS, preferred_element_type=F32) * ATTN_SCALE
        s = s + bias_ref[base]
        s = jnp.where(col_valid, s, -jnp.inf)
        p = jnp.exp(s - jnp.max(s, axis=-1, keepdims=True))
        l = jnp.sum(p, axis=-1, keepdims=True)
        o = jnp.dot(p.astype(BF16), vw, preferred_element_type=F32)
        o_ref[pl.ds(q0, GRID_W), :] = (o / l).astype(o_ref.dtype)
        return carry

    lax.fori_loop(0, rows, body, 0)


def _na_attention(proj3d, bias_table):
    b, s, _ = proj3d.shape
    rows = s // GRID_W
    assert rows >= NA_WIN_ROWS
    qkv_spec = lambda off: pl.BlockSpec((None, s, HEAD_DIM), lambda bi, h: (bi, 0, off + h))
    return pl.pallas_call(
        functools.partial(_na_kernel, rows=rows),
        grid=(b, NA_HEADS),
        in_specs=[
            qkv_spec(0), qkv_spec(NA_HEADS), qkv_spec(2 * NA_HEADS),
            pl.BlockSpec((None, NA_WIN_ROWS, GRID_W, NA_WIN_ROWS * GRID_W), lambda bi, h: (h, 0, 0, 0)),
        ],
        out_specs=pl.BlockSpec((None, s, HEAD_DIM), lambda bi, h: (bi, 0, h)),
        out_shape=jax.ShapeDtypeStruct((b, s, NA_WIDTH), BF16),
        compiler_params=_params("parallel", "parallel"),
        name="na_attention",
    )(proj3d, proj3d, proj3d, bias_table)


def _da_kernel(slopes_ref, q1_ref, q2_ref, k1_ref, k2_ref, v_ref,
               lq1_ref, lk1_ref, lq2_ref, lk2_ref, sub_ref, o_ref, *, lam_init):
    h = pl.program_id(1)
    qi = pl.program_id(2)
    tq = q1_ref.shape[0]
    s_len = k1_ref.shape[0]
    lam = (jnp.exp(jnp.sum(lq1_ref[...] * lk1_ref[...], axis=-1, keepdims=True))
           - jnp.exp(jnp.sum(lq2_ref[...] * lk2_ref[...], axis=-1, keepdims=True))
           + lam_init)
    row = lax.broadcasted_iota(jnp.int32, (tq, s_len), 0) + qi * tq
    col = lax.broadcasted_iota(jnp.int32, (tq, s_len), 1)
    alibi = -slopes_ref[h] * jnp.abs(row - col).astype(F32)

    def softmax_parts(q_ref, k_ref):
        s = lax.dot_general(q_ref[...], k_ref[...], NT_DIMS, preferred_element_type=F32)
        s = s * ATTN_SCALE + alibi
        p = jnp.exp(s - jnp.max(s, axis=-1, keepdims=True))
        return p, jnp.sum(p, axis=-1, keepdims=True)

    p1, l1 = softmax_parts(q1_ref, k1_ref)
    p2, l2 = softmax_parts(q2_ref, k2_ref)
    w = p1 * (1.0 / l1) - p2 * (lam / l2)
    o = jnp.dot(w.astype(BF16), v_ref[...], preferred_element_type=F32)
    o_ref[...] = (_rms(o) * sub_ref[...] * (1.0 - lam_init)).astype(o_ref.dtype)


def _da_attention(proj3d, slopes, lam_q1, lam_k1, lam_q2, lam_k2, subln, *, lam_init, tq):
    b, s, _ = proj3d.shape
    qk0 = 3 * NA_WIDTH // HEAD_DIM
    v0 = (3 * NA_WIDTH + 4 * DA_QK) // DA_VDIM
    q_spec = lambda off: pl.BlockSpec((None, tq, HEAD_DIM), lambda bi, h, qi, sl: (bi, qi, off + h))
    k_spec = lambda off: pl.BlockSpec((None, s, HEAD_DIM), lambda bi, h, qi, sl: (bi, 0, off + h))
    vec_spec = lambda n: pl.BlockSpec((1, n), lambda bi, h, qi, sl: (0, 0))
    grid_spec = pltpu.PrefetchScalarGridSpec(
        num_scalar_prefetch=1,
        grid=(b, DA_HEADS, s // tq),
        in_specs=[
            q_spec(qk0), q_spec(qk0 + DA_HEADS), k_spec(qk0 + 2 * DA_HEADS), k_spec(qk0 + 3 * DA_HEADS),
            pl.BlockSpec((None, s, DA_VDIM), lambda bi, h, qi, sl: (bi, 0, v0 + h)),
            vec_spec(HEAD_DIM), vec_spec(HEAD_DIM), vec_spec(HEAD_DIM), vec_spec(HEAD_DIM),
            vec_spec(DA_VDIM),
        ],
        out_specs=pl.BlockSpec((None, tq, DA_VDIM), lambda bi, h, qi, sl: (bi, qi, h)),
    )
    row = lambda a: a.reshape(1, -1).astype(F32)
    return pl.pallas_call(
        functools.partial(_da_kernel, lam_init=lam_init),
        grid_spec=grid_spec,
        out_shape=jax.ShapeDtypeStruct((b, s, DA_WIDTH), BF16),
        compiler_params=_params("parallel", "parallel", "arbitrary"),
        name="da_attention",
    )(slopes, proj3d, proj3d, proj3d, proj3d, proj3d,
      row(lam_q1), row(lam_k1), row(lam_q2), row(lam_k2), row(subln))


def _outproj_kernel(oa_ref, ob_ref, x_ref, ona_ref, wa_ref, wb_ref, g2_ref, wrt_ref,
                    x1_ref, h2_ref, lg_ref):
    oan = (_rms(oa_ref[...].astype(F32)) * ona_ref[...]).astype(BF16)
    acc = jnp.dot(oan, wa_ref[...], preferred_element_type=F32)
    acc = acc + jnp.dot(ob_ref[...], wb_ref[...], preferred_element_type=F32)
    x1 = x_ref[...] + acc
    x1_ref[...] = x1
    h2 = _rms(x1) * g2_ref[...]
    h2_ref[...] = h2
    lg_ref[...] = lax.dot_general(wrt_ref[...], h2, NT_DIMS, preferred_element_type=F32,
                                  precision=lax.Precision.HIGHEST)


def _out_projection(oa2d, ob2d, x2d, on_a, w_out, ln2_g, w_router_t, *, tm):
    n, d = x2d.shape
    n_exp = w_router_t.shape[0]
    wa, wb = w_out[:NA_WIDTH], w_out[NA_WIDTH:]
    row_spec = lambda width: pl.BlockSpec((tm, width), lambda i: (i, 0))
    full_spec = lambda a: pl.BlockSpec(a.shape, lambda i: (0, 0))
    on_a = on_a.reshape(1, -1)
    ln2_g = ln2_g.reshape(1, -1)
    return pl.pallas_call(
        _outproj_kernel,
        grid=(n // tm,),
        in_specs=[row_spec(NA_WIDTH), row_spec(DA_WIDTH), row_spec(d), full_spec(on_a),
                  full_spec(wa), full_spec(wb), full_spec(ln2_g), full_spec(w_router_t)],
        out_specs=[row_spec(d), row_spec(d), pl.BlockSpec((n_exp, tm), lambda i: (0, i))],
        out_shape=[jax.ShapeDtypeStruct((n, d), F32), jax.ShapeDtypeStruct((n, d), F32),
                   jax.ShapeDtypeStruct((n_exp, n), F32)],
        compiler_params=_params("parallel"),
        name="out_projection",
    )(oa2d, ob2d, x2d, on_a, wa, wb, ln2_g, w_router_t)


def _exclusive_prefix(mask):
    rows, s_len = mask.shape
    m = jnp.where(mask, 1.0, 0.0).astype(BF16)
    r = lax.broadcasted_iota(jnp.int32, (V7X_LANES, V7X_LANES), 0)
    c = lax.broadcasted_iota(jnp.int32, (V7X_LANES, V7X_LANES), 1)
    before = jnp.where(r < c, 1.0, 0.0).astype(BF16)
    ones = jnp.ones((V7X_LANES, V7X_LANES), BF16)
    run = jnp.zeros((rows, V7X_LANES), F32)
    out = []
    for blk in range(s_len // V7X_LANES):
        mb = m[:, blk * V7X_LANES:(blk + 1) * V7X_LANES]
        out.append(run + jnp.dot(mb, before, preferred_element_type=F32))
        run = run + jnp.dot(mb, ones, preferred_element_type=F32)
    return jnp.concatenate(out, axis=1)


def _route_kernel(lg_ref, idx_ref, gate_ref, posm_ref, aff_ref, *, cap, slot_tile):
    lg = lg_ref[...]
    n_exp, s_len = lg.shape
    ex = jnp.exp(lg - jnp.max(lg, axis=0, keepdims=True))
    aff = ex / jnp.sum(ex, axis=0, keepdims=True)
    bits = pltpu.bitcast(aff, jnp.int32)

    thr = jnp.zeros((n_exp, 1), jnp.int32)
    for bit in range(30, -1, -1):
        cand = thr | (1 << bit)
        cnt = jnp.sum(jnp.where(bits >= cand, 1.0, 0.0), axis=1, keepdims=True)
        thr = jnp.where(cnt >= cap, cand, thr)

    gt = bits > thr
    eq = bits == thr
    need = cap - jnp.sum(jnp.where(gt, 1.0, 0.0), axis=1, keepdims=True)
    sel = gt | (eq & (_exclusive_prefix(eq) < need))
    posm_ref[...] = jnp.where(sel, _exclusive_prefix(sel), -1.0)
    aff_ref[...] = aff

    tok = lax.broadcasted_iota(jnp.int32, (1, s_len), 1).astype(F32)

    def per_expert(e, carry):
        pos_e = posm_ref[pl.ds(e, 1), :]
        aff_e = aff_ref[pl.ds(e, 1), :]

        def per_tile(st, c2):
            s0 = pl.multiple_of(st * slot_tile, slot_tile)
            slots = (s0 + lax.broadcasted_iota(jnp.int32, (slot_tile, 1), 0)).astype(F32)
            hit = pos_e == slots
            idx = jnp.sum(jnp.where(hit, tok, 0.0), axis=1, keepdims=True)
            gate = jnp.sum(jnp.where(hit, aff_e, 0.0), axis=1, keepdims=True)
            idx_ref[e, pl.ds(s0, slot_tile), :] = idx.astype(jnp.int32)
            gate_ref[e, pl.ds(s0, slot_tile), :] = gate
            return c2

        return lax.fori_loop(0, cap // slot_tile, per_tile, carry)

    lax.fori_loop(0, n_exp, per_expert, 0)


def _ec_route(logits_t, batch, *, cap, slot_tile):
    n_exp, n = logits_t.shape
    s = n // batch
    out_spec = pl.BlockSpec((None, n_exp, cap, 1), lambda b: (b, 0, 0, 0))
    return pl.pallas_call(
        functools.partial(_route_kernel, cap=cap, slot_tile=slot_tile),
        grid=(batch,),
        in_specs=[pl.BlockSpec((n_exp, s), lambda b: (0, b))],
        out_specs=[out_spec, out_spec],
        out_shape=[jax.ShapeDtypeStruct((batch, n_exp, cap, 1), jnp.int32),
                   jax.ShapeDtypeStruct((batch, n_exp, cap, 1), F32)],
        scratch_shapes=[pltpu.VMEM((n_exp, s), F32), pltpu.VMEM((n_exp, s), F32)],
        compiler_params=_params("parallel"),
        name="ec_route",
    )(logits_t)


def _ffn_kernel(rows_ref, h2_hbm, gate_ref, wg_ref, wu_ref, wd_ref, x1_hbm, out_hbm,
                xs_ref, xsb_ref, acc_ref, sem_ref, *, m_rows):
    del x1_hbm
    e = pl.program_id(0)
    f = pl.program_id(1)
    base = e * m_rows

    def gather(src_hbm):
        def issue(i, carry):
            pltpu.make_async_copy(src_hbm.at[pl.ds(rows_ref[base + i], 1), :],
                                  xs_ref.at[pl.ds(i, 1), :], sem_ref.at[0]).start()
            return carry
        lax.fori_loop(0, m_rows, issue, 0)

        def drain(i, carry):
            pltpu.make_async_copy(src_hbm.at[pl.ds(0, 1), :], xs_ref.at[pl.ds(i, 1), :],
                                  sem_ref.at[0]).wait()
            return carry
        lax.fori_loop(0, m_rows, drain, 0)

    @pl.when(f == 0)
    def _():
        gather(h2_hbm)
        xsb_ref[...] = xs_ref[...].astype(BF16)
        acc_ref[...] = jnp.zeros_like(acc_ref)

    xs = xsb_ref[...]
    g = jnp.dot(xs, wg_ref[...].astype(BF16), preferred_element_type=F32)
    u = jnp.dot(xs, wu_ref[...].astype(BF16), preferred_element_type=F32)
    a = (g * jax.nn.sigmoid(g) * u).astype(BF16)
    acc_ref[...] += jnp.dot(a, wd_ref[...].astype(BF16), preferred_element_type=F32)

    @pl.when(f == pl.num_programs(1) - 1)
    def _():
        gather(out_hbm)
        xs_ref[...] += acc_ref[...] * gate_ref[...]

        def issue(i, carry):
            pltpu.make_async_copy(xs_ref.at[pl.ds(i, 1), :],
                                  out_hbm.at[pl.ds(rows_ref[base + i], 1), :], sem_ref.at[1]).start()
            return carry
        lax.fori_loop(0, m_rows, issue, 0)

        def drain(i, carry):
            pltpu.make_async_copy(xs_ref.at[pl.ds(i, 1), :], out_hbm.at[pl.ds(0, 1), :],
                                  sem_ref.at[1]).wait()
            return carry
        lax.fori_loop(0, m_rows, drain, 0)


def _ec_ffn(rows_flat, h2, gate, w_gate, w_up, w_down, x1, *, tf):
    n, d = x1.shape
    n_exp, _, ff = w_gate.shape
    m_rows = rows_flat.shape[0] // n_exp
    grid_spec = pltpu.PrefetchScalarGridSpec(
        num_scalar_prefetch=1,
        grid=(n_exp, ff // tf),
        in_specs=[
            pl.BlockSpec(memory_space=pl.ANY),
            pl.BlockSpec((None, m_rows, 1), lambda e, f, r: (e, 0, 0)),
            pl.BlockSpec((None, d, tf), lambda e, f, r: (e, 0, f)),
            pl.BlockSpec((None, d, tf), lambda e, f, r: (e, 0, f)),
            pl.BlockSpec((None, tf, d), lambda e, f, r: (e, f, 0)),
            pl.BlockSpec(memory_space=pl.ANY),
        ],
        out_specs=pl.BlockSpec(memory_space=pl.ANY),
        scratch_shapes=[pltpu.VMEM((m_rows, d), F32), pltpu.VMEM((m_rows, d), BF16),
                        pltpu.VMEM((m_rows, d), F32), pltpu.SemaphoreType.DMA((2,))],
    )
    return pl.pallas_call(
        functools.partial(_ffn_kernel, m_rows=m_rows),
        grid_spec=grid_spec,
        out_shape=jax.ShapeDtypeStruct((n, d), F32),
        input_output_aliases={6: 0},
        compiler_params=_params("arbitrary", "arbitrary"),
        name="ec_ffn",
    )(rows_flat, h2, gate, w_gate, w_up, w_down, x1)


def _tiles(n, d, ff):
    pick = lambda total, want: want if total % want == 0 else total
    return dict(tm_in=pick(n, 1024), tn_in=1024, tq=pick(n, 256), tm_out=pick(n, 512),
                tf=pick(ff, 256))


def kernel(x, ln1_g, w_in, qn_a, kn_a, rpb_a, on_a, qn_b, kn_b, lam_q1, lam_k1, lam_q2, lam_k2,
           subln_b, w_out, ln2_g, w_router, w_gate, w_up, w_down):
    b, s, d = x.shape
    n = b * s
    depth = w_in.shape[0]
    n_exp = w_router.shape[-1]
    cap = EC_CAPACITY_FACTOR * s // n_exp
    t = _tiles(n, d, w_gate.shape[-1])
    slopes = 2.0 ** (-8.0 * jnp.arange(1, DA_HEADS + 1, dtype=F32) / DA_HEADS)
    ones_v = jnp.ones((NA_WIDTH,), F32)
    x2d = x.reshape(n, d)
    for l in range(depth):
        tile_heads = lambda g, heads: jnp.tile(g.astype(F32), heads)
        col_gain = jnp.concatenate([
            tile_heads(qn_a[l], NA_HEADS), tile_heads(kn_a[l], NA_HEADS), ones_v,
            tile_heads(qn_b[l], 2 * DA_HEADS), tile_heads(kn_b[l], 2 * DA_HEADS),
            jnp.ones((DA_WIDTH,), F32)]).reshape(1, IN_COLS)
        proj = _in_projection(x2d, ln1_g[l], w_in[l].astype(BF16), col_gain,
                              tm=t["tm_in"], tn=t["tn_in"])
        proj3d = proj.reshape(b, s, IN_COLS)
        oa = _na_attention(proj3d, _na_bias_table(rpb_a[l]))
        ob = _da_attention(proj3d, slopes, lam_q1[l], lam_k1[l], lam_q2[l], lam_k2[l], subln_b[l],
                           lam_init=_lambda_init(l), tq=min(t["tq"], s))
        x1, h2, logits_t = _out_projection(
            oa.reshape(n, NA_WIDTH), ob.reshape(n, DA_WIDTH), x2d, on_a[l],
            w_out[l].astype(BF16), ln2_g[l], w_router[l].T, tm=t["tm_out"])
        idx, gate = _ec_route(logits_t, b, cap=cap, slot_tile=64)
        rows = idx[..., 0] + (jnp.arange(b, dtype=jnp.int32) * s)[:, None, None]
        rows_flat = rows.transpose(1, 0, 2).reshape(-1)
        gate_e = gate.transpose(1, 0, 2, 3).reshape(n_exp, b * cap, 1)
        x2d = _ec_ffn(rows_flat, h2, gate_e, w_gate[l], w_up[l], w_down[l], x1, tf=t["tf"])
    return x2d.reshape(b, s, d)
```

```python
import functools
import math

import jax
import jax.numpy as jnp
from jax import lax
from jax.experimental import pallas as pl
from jax.experimental.pallas import tpu as pltpu

F32 = jnp.float32
BF16 = jnp.bfloat16

GRID_W = 64
HEAD_DIM = 128
NA_HEADS = 8
NA_WIDTH = NA_HEADS * HEAD_DIM
NA_WIN_ROWS = 8
NA_WIN_COLS = 16
DA_HEADS = 4
DA_VDIM = 2 * HEAD_DIM
DA_WIDTH = DA_HEADS * DA_VDIM
DA_QK = DA_HEADS * HEAD_DIM
DA_LAYER_LAMBDA_BASE = 0.8
IN_COLS = 3 * NA_WIDTH + 4 * DA_QK + DA_WIDTH
N_EXPERTS = 16
EC_CAPACITY_FACTOR = 2
RMS_EPS = 1e-6
ATTN_SCALE = HEAD_DIM ** -0.5
LOG2_E = math.log2(math.e)

V7X_LANES = 128
V7X_VMEM_LIMIT_BYTES = 56 * 1024 * 1024

NT_DIMS = (((1,), (1,)), ((), ()))


def _lambda_init(layer_idx):
    return DA_LAYER_LAMBDA_BASE - 0.6 * math.exp(-0.3 * layer_idx)


def _rms(y):
    return y * lax.rsqrt(jnp.mean(y * y, axis=-1, keepdims=True) + RMS_EPS)


def _params(*semantics):
    return pltpu.CompilerParams(dimension_semantics=semantics,
                                vmem_limit_bytes=V7X_VMEM_LIMIT_BYTES)


def _inproj_kernel(x_ref, g1_ref, w_ref, gain_ref, o_ref, h_ref, *, tn):
    j = pl.program_id(1)

    @pl.when(j == 0)
    def _():
        h_ref[...] = (_rms(x_ref[...]) * g1_ref[...]).astype(BF16)

    acc = jnp.dot(h_ref[...], w_ref[...], preferred_element_type=F32)
    col0 = j * tn
    qk_cols = (col0 < 2 * NA_WIDTH) | ((col0 >= 3 * NA_WIDTH) & (col0 < 3 * NA_WIDTH + 4 * DA_QK))

    @pl.when(qk_cols)
    def _():
        for c in range(tn // HEAD_DIM):
            sl = slice(c * HEAD_DIM, (c + 1) * HEAD_DIM)
            o_ref[:, sl] = (_rms(acc[:, sl]) * gain_ref[:, sl]).astype(o_ref.dtype)

    @pl.when(jnp.logical_not(qk_cols))
    def _():
        o_ref[...] = acc.astype(o_ref.dtype)


def _in_projection(x2d, ln1_g, w_in, col_gain, *, tm, tn):
    n, d = x2d.shape
    cols = w_in.shape[1]
    return pl.pallas_call(
        functools.partial(_inproj_kernel, tn=tn),
        grid=(n // tm, cols // tn),
        in_specs=[
            pl.BlockSpec((tm, d), lambda i, j: (i, 0)),
            pl.BlockSpec((1, d), lambda i, j: (0, 0)),
            pl.BlockSpec((d, tn), lambda i, j: (0, j)),
            pl.BlockSpec((1, tn), lambda i, j: (0, j)),
        ],
        out_specs=pl.BlockSpec((tm, tn), lambda i, j: (i, j)),
        out_shape=jax.ShapeDtypeStruct((n, cols), BF16),
        scratch_shapes=[pltpu.VMEM((tm, d), BF16)],
        compiler_params=_params("parallel", "arbitrary"),
        name="in_projection",
    )(x2d, ln1_g.reshape(1, d), w_in, col_gain)


def _na_bias_table(rpb):
    c = jnp.arange(GRID_W)
    col_off = jnp.clip(c[None, :] - c[:, None], -(NA_WIN_COLS - 1), NA_WIN_COLS - 1) + (NA_WIN_COLS - 1)
    onehot = (col_off[None] == jnp.arange(2 * NA_WIN_COLS - 1)[:, None, None]).astype(F32)
    t = jnp.einsum("hro,oqk->hrqk", rpb.astype(F32), onehot, precision=lax.Precision.HIGHEST)
    cs = jnp.clip(c - NA_WIN_COLS // 2, 0, GRID_W - NA_WIN_COLS)
    col_valid = (c[None, :] >= cs[:, None]) & (c[None, :] < cs[:, None] + NA_WIN_COLS)
    t = jnp.where(col_valid, t, -jnp.inf)
    return jnp.stack([jnp.concatenate([t[:, base + j] for j in range(NA_WIN_ROWS)], axis=-1)
                      for base in range(NA_WIN_ROWS)], axis=1)


def _na_kernel(q_ref, k_ref, v_ref, bias_ref, o_ref, *, rows, group):
    win = NA_WIN_ROWS * GRID_W

    def body(it, carry):
        r = [it * group + g for g in range(group)]
        rs = [jnp.clip(ri - NA_WIN_ROWS // 2, 0, rows - NA_WIN_ROWS) for ri in r]
        q0 = [pl.multiple_of(ri * GRID_W, GRID_W) for ri in r]
        k0 = [pl.multiple_of(rsi * GRID_W, GRID_W) for rsi in rs]
        s = [lax.dot_general(q_ref[pl.ds(q0[g], GRID_W), :], k_ref[pl.ds(k0[g], win), :], NT_DIMS,
                             preferred_element_type=F32) for g in range(group)]
        p, l = [], []
        for g in range(group):
            sg = s[g] * ATTN_SCALE + bias_ref[rs[g] - r[g] + (NA_WIN_ROWS - 1)]
            pg = jnp.exp(sg - jnp.max(sg, axis=-1, keepdims=True))
            l.append(jnp.sum(pg, axis=-1, keepdims=True))
            p.append(pg.astype(BF16))
        for g in range(group):
            o = jnp.dot(p[g], v_ref[pl.ds(k0[g], win), :], preferred_element_type=F32)
            o_ref[pl.ds(q0[g], GRID_W), :] = (o / l[g]).astype(o_ref.dtype)
        return carry

    lax.fori_loop(0, rows // group, body, 0)


def _na_attention(proj3d, bias_table):
    b, s, _ = proj3d.shape
    rows = s // GRID_W
    assert rows >= NA_WIN_ROWS
    qkv_spec = lambda off: pl.BlockSpec((None, s, HEAD_DIM), lambda bi, h: (bi, 0, off + h))
    return pl.pallas_call(
        functools.partial(_na_kernel, rows=rows, group=8),
        grid=(b, NA_HEADS),
        in_specs=[
            qkv_spec(0), qkv_spec(NA_HEADS), qkv_spec(2 * NA_HEADS),
            pl.BlockSpec((None, NA_WIN_ROWS, GRID_W, NA_WIN_ROWS * GRID_W), lambda bi, h: (h, 0, 0, 0)),
        ],
        out_specs=pl.BlockSpec((None, s, HEAD_DIM), lambda bi, h: (bi, 0, h)),
        out_shape=jax.ShapeDtypeStruct((b, s, NA_WIDTH), BF16),
        compiler_params=_params("parallel", "parallel"),
        name="na_attention",
    )(proj3d, proj3d, proj3d, bias_table)


def _da_kernel(slopes_ref, q1_ref, q2_ref, k1_ref, k2_ref, v_ref,
               lq1_ref, lk1_ref, lq2_ref, lk2_ref, sub_ref, o_ref, alibi_ref, z_ref, p_ref,
               *, lam_init, kc):
    h = pl.program_id(1)
    qi = pl.program_id(2)
    tq = q1_ref.shape[0]
    s_len = k1_ref.shape[0]
    lam = (jnp.exp(jnp.sum(lq1_ref[...] * lk1_ref[...], axis=-1, keepdims=True))
           - jnp.exp(jnp.sum(lq2_ref[...] * lk2_ref[...], axis=-1, keepdims=True))
           + lam_init)

    @pl.when(qi == 0)
    def _():
        i = lax.broadcasted_iota(jnp.int32, alibi_ref.shape, 0)
        x = lax.broadcasted_iota(jnp.int32, alibi_ref.shape, 1)
        alibi_ref[...] = (-LOG2_E * slopes_ref[h]) * jnp.abs(x - i - (s_len - tq)).astype(F32)

    win0 = s_len - tq - qi * tq
    n_chunks = s_len // kc

    def softmax_parts(q_ref, k_ref):
        q = q_ref[...]
        m = jnp.full((tq, 1), -jnp.inf, F32)
        for c in range(n_chunks):
            s = lax.dot_general(q, k_ref[c * kc:(c + 1) * kc, :], NT_DIMS, preferred_element_type=F32)
            z = s * (ATTN_SCALE * LOG2_E) + alibi_ref[:, pl.ds(pl.multiple_of(win0 + c * kc, V7X_LANES), kc)]
            z_ref[:, c * kc:(c + 1) * kc] = z
            m = jnp.maximum(m, jnp.max(z, axis=-1, keepdims=True))
        l = jnp.zeros((tq, 1), F32)
        for c in range(n_chunks):
            p = jnp.exp2(z_ref[:, c * kc:(c + 1) * kc] - m)
            l = l + jnp.sum(p, axis=-1, keepdims=True)
            p_ref[:, c * kc:(c + 1) * kc] = p.astype(BF16)
        o = jnp.dot(p_ref[...], v_ref[...], preferred_element_type=F32)
        return o, l

    o1, l1 = softmax_parts(q1_ref, k1_ref)
    o2, l2 = softmax_parts(q2_ref, k2_ref)
    o = o1 * (1.0 / l1) - o2 * (lam / l2)
    o_ref[...] = (_rms(o) * sub_ref[...] * (1.0 - lam_init)).astype(o_ref.dtype)


def _da_attention(proj3d, slopes, lam_q1, lam_k1, lam_q2, lam_k2, subln, *, lam_init, tq):
    b, s, _ = proj3d.shape
    qk0 = 3 * NA_WIDTH // HEAD_DIM
    v0 = (3 * NA_WIDTH + 4 * DA_QK) // DA_VDIM
    q_spec = lambda off: pl.BlockSpec((None, tq, HEAD_DIM), lambda bi, h, qi, sl: (bi, qi, off + h))
    k_spec = lambda off: pl.BlockSpec((None, s, HEAD_DIM), lambda bi, h, qi, sl: (bi, 0, off + h))
    vec_spec = lambda n: pl.BlockSpec((1, n), lambda bi, h, qi, sl: (0, 0))
    grid_spec = pltpu.PrefetchScalarGridSpec(
        num_scalar_prefetch=1,
        grid=(b, DA_HEADS, s // tq),
        in_specs=[
            q_spec(qk0), q_spec(qk0 + DA_HEADS), k_spec(qk0 + 2 * DA_HEADS), k_spec(qk0 + 3 * DA_HEADS),
            pl.BlockSpec((None, s, DA_VDIM), lambda bi, h, qi, sl: (bi, 0, v0 + h)),
            vec_spec(HEAD_DIM), vec_spec(HEAD_DIM), vec_spec(HEAD_DIM), vec_spec(HEAD_DIM),
            vec_spec(DA_VDIM),
        ],
        out_specs=pl.BlockSpec((None, tq, DA_VDIM), lambda bi, h, qi, sl: (bi, qi, h)),
        scratch_shapes=[pltpu.VMEM((tq, 2 * s - tq), F32), pltpu.VMEM((tq, s), F32),
                        pltpu.VMEM((tq, s), BF16)],
    )
    row = lambda a: a.reshape(1, -1).astype(F32)
    return pl.pallas_call(
        functools.partial(_da_kernel, lam_init=lam_init, kc=min(512, s)),
        grid_spec=grid_spec,
        out_shape=jax.ShapeDtypeStruct((b, s, DA_WIDTH), BF16),
        compiler_params=_params("parallel", "parallel", "arbitrary"),
        name="da_attention",
    )(slopes, proj3d, proj3d, proj3d, proj3d, proj3d,
      row(lam_q1), row(lam_k1), row(lam_q2), row(lam_k2), row(subln))


def _outproj_kernel(oa_ref, ob_ref, x_ref, ona_ref, wa_ref, wb_ref, g2_ref, wrt_ref,
                    x1_ref, lg_ref):
    oan = (_rms(oa_ref[...].astype(F32)) * ona_ref[...]).astype(BF16)
    acc = jnp.dot(oan, wa_ref[...], preferred_element_type=F32)
    acc = acc + jnp.dot(ob_ref[...], wb_ref[...], preferred_element_type=F32)
    x1 = x_ref[...] + acc
    x1_ref[...] = x1
    h2 = _rms(x1) * g2_ref[...]
    lg_ref[...] = lax.dot_general(wrt_ref[...], h2, NT_DIMS, preferred_element_type=F32,
                                  precision=lax.Precision.HIGHEST)


def _out_projection(oa2d, ob2d, x2d, on_a, w_out, ln2_g, w_router_t, *, tm):
    n, d = x2d.shape
    n_exp = w_router_t.shape[0]
    wa, wb = w_out[:NA_WIDTH], w_out[NA_WIDTH:]
    row_spec = lambda width: pl.BlockSpec((tm, width), lambda i: (i, 0))
    full_spec = lambda a: pl.BlockSpec(a.shape, lambda i: (0, 0))
    on_a = on_a.reshape(1, -1)
    ln2_g = ln2_g.reshape(1, -1)
    return pl.pallas_call(
        _outproj_kernel,
        grid=(n // tm,),
        in_specs=[row_spec(NA_WIDTH), row_spec(DA_WIDTH), row_spec(d), full_spec(on_a),
                  full_spec(wa), full_spec(wb), full_spec(ln2_g), full_spec(w_router_t)],
        out_specs=[row_spec(d), pl.BlockSpec((n_exp, tm), lambda i: (0, i))],
        out_shape=[jax.ShapeDtypeStruct((n, d), F32), jax.ShapeDtypeStruct((n_exp, n), F32)],
        compiler_params=_params("parallel"),
        name="out_projection",
    )(oa2d, ob2d, x2d, on_a, wa, wb, ln2_g, w_router_t)


def _exclusive_prefix(mask):
    rows, s_len = mask.shape
    m = jnp.where(mask, 1.0, 0.0).astype(BF16)
    r = lax.broadcasted_iota(jnp.int32, (V7X_LANES, V7X_LANES), 0)
    c = lax.broadcasted_iota(jnp.int32, (V7X_LANES, V7X_LANES), 1)
    before = jnp.where(r < c, 1.0, 0.0).astype(BF16)
    ones = jnp.ones((V7X_LANES, V7X_LANES), BF16)
    run = jnp.zeros((rows, V7X_LANES), F32)
    out = []
    for blk in range(s_len // V7X_LANES):
        mb = m[:, blk * V7X_LANES:(blk + 1) * V7X_LANES]
        out.append(run + jnp.dot(mb, before, preferred_element_type=F32))
        run = run + jnp.dot(mb, ones, preferred_element_type=F32)
    return jnp.concatenate(out, axis=1)


def _route_kernel(lg_ref, idx_ref, gate_ref, posm_ref, boff_ref, aff_ref, *, cap, slot_tile, tb):
    lg = lg_ref[...]
    n_exp, s_len = lg.shape
    ex = jnp.exp(lg - jnp.max(lg, axis=0, keepdims=True))
    aff = ex / jnp.sum(ex, axis=0, keepdims=True)
    bits = pltpu.bitcast(aff, jnp.int32)

    thr = jnp.zeros((n_exp, 1), jnp.int32)
    for bit in range(30, -1, -1):
        cand = thr | (1 << bit)
        cnt = jnp.sum(jnp.where(bits >= cand, 1.0, 0.0), axis=1, keepdims=True)
        thr = jnp.where(cnt >= cap, cand, thr)

    gt = bits > thr
    eq = bits == thr
    need = cap - jnp.sum(jnp.where(gt, 1.0, 0.0), axis=1, keepdims=True)
    sel = gt | (eq & (_exclusive_prefix(eq) < need))
    posm_ref[...] = jnp.where(sel, _exclusive_prefix(sel), -1.0)
    aff_ref[...] = aff

    t_idx = lax.broadcasted_iota(jnp.int32, (s_len, V7X_LANES), 0)
    j_idx = lax.broadcasted_iota(jnp.int32, (s_len, V7X_LANES), 1)
    before_block = jnp.where(t_idx < j_idx * tb, 1.0, 0.0).astype(BF16)
    boff = jnp.dot(jnp.where(sel, 1.0, 0.0).astype(BF16), before_block, preferred_element_type=F32)
    boff_ref[...] = boff.astype(jnp.int32)

    tok = lax.broadcasted_iota(jnp.int32, (1, s_len), 1).astype(F32)

    def per_expert(e, carry):
        pos_e = posm_ref[pl.ds(e, 1), :]
        aff_e = aff_ref[pl.ds(e, 1), :]

        def per_tile(st, c2):
            s0 = pl.multiple_of(st * slot_tile, slot_tile)
            slots = (s0 + lax.broadcasted_iota(jnp.int32, (slot_tile, 1), 0)).astype(F32)
            hit = pos_e == slots
            idx = jnp.sum(jnp.where(hit, tok, 0.0), axis=1, keepdims=True)
            gate = jnp.sum(jnp.where(hit, aff_e, 0.0), axis=1, keepdims=True)
            idx_ref[e, pl.ds(s0, slot_tile), :] = idx.astype(jnp.int32)
            gate_ref[e, pl.ds(s0, slot_tile), :] = gate
            return c2

        return lax.fori_loop(0, cap // slot_tile, per_tile, carry)

    lax.fori_loop(0, n_exp, per_expert, 0)


def _ec_route(logits_t, batch, *, cap, slot_tile, tb):
    n_exp, n = logits_t.shape
    s = n // batch
    assert s // tb + 1 <= V7X_LANES
    slot_spec = pl.BlockSpec((None, n_exp, cap, 1), lambda b: (b, 0, 0, 0))
    return pl.pallas_call(
        functools.partial(_route_kernel, cap=cap, slot_tile=slot_tile, tb=tb),
        grid=(batch,),
        in_specs=[pl.BlockSpec((n_exp, s), lambda b: (0, b))],
        out_specs=[slot_spec, slot_spec,
                   pl.BlockSpec((None, n_exp, s), lambda b: (b, 0, 0)),
                   pl.BlockSpec((None, n_exp, V7X_LANES), lambda b: (b, 0, 0))],
        out_shape=[jax.ShapeDtypeStruct((batch, n_exp, cap, 1), jnp.int32),
                   jax.ShapeDtypeStruct((batch, n_exp, cap, 1), F32),
                   jax.ShapeDtypeStruct((batch, n_exp, s), F32),
                   jax.ShapeDtypeStruct((batch, n_exp, V7X_LANES), jnp.int32)],
        scratch_shapes=[pltpu.VMEM((n_exp, s), F32)],
        compiler_params=_params("parallel"),
        name="ec_route",
    )(logits_t)


def _ffn_kernel(rows_ref, x1_hbm, g2_ref, gate_ref, wg_ref, wu_ref, wd_ref, y_ref,
                xs_ref, xsb_ref, acc_ref, sem_ref, *, m_rows, chunk):
    e = pl.program_id(0)
    f = pl.program_id(1)
    n_exp = pl.num_programs(0)
    n_f = pl.num_programs(1)
    unroll = 8

    def issue_rows(expert, lo, n_groups):
        base = expert * m_rows

        def group(ig, carry):
            for u in range(unroll):
                i = lo + ig * unroll + u
                pltpu.make_async_copy(x1_hbm.at[pl.ds(rows_ref[base + i], 1), :],
                                      xs_ref.at[pl.ds(i, 1), :], sem_ref.at[0]).start()
            return carry

        lax.fori_loop(0, n_groups, group, 0)

    @pl.when((e == 0) & (f == 0))
    def _():
        issue_rows(0, 0, m_rows // unroll)

    @pl.when(f == 0)
    def _():
        pltpu.make_async_copy(x1_hbm.at[pl.ds(0, m_rows), :], xs_ref, sem_ref.at[0]).wait()
        xsb_ref[...] = (_rms(xs_ref[...]) * g2_ref[...]).astype(BF16)
        acc_ref[...] = jnp.zeros_like(acc_ref)

    @pl.when((f > 0) & (e + 1 < n_exp))
    def _():
        lo = (f - 1) * chunk
        issue_rows(e + 1, lo, jnp.clip(m_rows - lo, 0, chunk) // unroll)

    xs = xsb_ref[...]
    g = jnp.dot(xs, wg_ref[...].astype(BF16), preferred_element_type=F32)
    u = jnp.dot(xs, wu_ref[...].astype(BF16), preferred_element_type=F32)
    a = (g * jax.nn.sigmoid(g) * u).astype(BF16)
    acc_ref[...] += jnp.dot(a, wd_ref[...].astype(BF16), preferred_element_type=F32)

    @pl.when(f == n_f - 1)
    def _():
        y_ref[...] = (acc_ref[...] * gate_ref[...]).astype(y_ref.dtype)


def _ec_ffn(rows_flat, x1, ln2_g, gate, w_gate, w_up, w_down, *, tf):
    n, d = x1.shape
    n_exp, _, ff = w_gate.shape
    m_rows = rows_flat.shape[0] // n_exp
    n_f = ff // tf
    assert n_f >= 2 and m_rows % 8 == 0
    chunk = -(-m_rows // (n_f - 1))
    chunk = -(-chunk // 8) * 8
    grid_spec = pltpu.PrefetchScalarGridSpec(
        num_scalar_prefetch=1,
        grid=(n_exp, n_f),
        in_specs=[
            pl.BlockSpec(memory_space=pl.ANY),
            pl.BlockSpec((1, d), lambda e, f, r: (0, 0)),
            pl.BlockSpec((None, m_rows, 1), lambda e, f, r: (e, 0, 0)),
            pl.BlockSpec((None, d, tf), lambda e, f, r: (e, 0, f)),
            pl.BlockSpec((None, d, tf), lambda e, f, r: (e, 0, f)),
            pl.BlockSpec((None, tf, d), lambda e, f, r: (e, f, 0)),
        ],
        out_specs=pl.BlockSpec((None, m_rows, d), lambda e, f, r: (e, 0, 0)),
        scratch_shapes=[pltpu.VMEM((m_rows, d), F32), pltpu.VMEM((m_rows, d), BF16),
                        pltpu.VMEM((m_rows, d), F32), pltpu.SemaphoreType.DMA((1,))],
    )
    return pl.pallas_call(
        functools.partial(_ffn_kernel, m_rows=m_rows, chunk=chunk),
        grid_spec=grid_spec,
        out_shape=jax.ShapeDtypeStruct((n_exp, m_rows, d), BF16),
        compiler_params=_params("arbitrary", "arbitrary"),
        name="ec_ffn",
    )(rows_flat, x1, ln2_g.reshape(1, d), gate, w_gate, w_up, w_down)


def _combine_kernel(boff_ref, x1_ref, posm_ref, y_hbm, o_ref, ywin_ref, sem_ref,
                    *, batch, cap, win, adv):
    n_exp, tb = posm_ref.shape
    i = pl.program_id(0)
    nblk = pl.num_programs(0) // batch
    b = i // nblk
    j = i % nblk
    offs = [boff_ref[(b * n_exp + e) * V7X_LANES + j] for e in range(n_exp)]
    nxts = [boff_ref[(b * n_exp + e) * V7X_LANES + j + 1] for e in range(n_exp)]
    most = functools.reduce(jnp.maximum, [nx - of for nx, of in zip(nxts, offs)])
    o_ref[...] = x1_ref[...]
    posm = posm_ref[...]
    prow = jnp.concatenate([jnp.broadcast_to(posm[e:e + 1, :], (win, tb)) for e in range(n_exp)], axis=0)

    def one_round(k, carry):
        copies, cols = [], []
        for e in range(n_exp):
            lower = offs[e] + k * adv
            start = jnp.minimum((lower // 16) * 16, cap - win)
            row0 = pl.multiple_of((e * batch + b) * cap + start, 16)
            cp = pltpu.make_async_copy(y_hbm.at[pl.ds(row0, win), :],
                                       ywin_ref.at[pl.ds(e * win, win), :], sem_ref.at[0])
            cp.start()
            copies.append(cp)
            slot = start + lax.broadcasted_iota(jnp.int32, (win, 1), 0)
            cols.append(jnp.where((slot >= lower) & (slot < lower + adv), slot, -2).astype(F32))
        hit_t = jnp.where(prow == jnp.concatenate(cols, axis=0), 1.0, 0.0).astype(BF16)
        for cp in copies:
            cp.wait()
        o_ref[...] += lax.dot_general(hit_t, ywin_ref[...], (((0,), (0,)), ((), ())),
                                      preferred_element_type=F32)
        return carry

    lax.fori_loop(0, (most + adv - 1) // adv, one_round, 0)


def _ec_combine(boff_flat, x1, posm, y2d, *, cap, tb, win, adv):
    n, d = x1.shape
    batch, n_exp, s = posm.shape
    nblk = s // tb
    assert cap % 16 == 0 and win % 16 == 0 and win >= adv + 16 and cap >= win
    grid_spec = pltpu.PrefetchScalarGridSpec(
        num_scalar_prefetch=1,
        grid=(n // tb,),
        in_specs=[
            pl.BlockSpec((tb, d), lambda i, bo: (i, 0)),
            pl.BlockSpec((None, n_exp, tb), lambda i, bo: (i // nblk, 0, i % nblk)),
            pl.BlockSpec(memory_space=pl.ANY),
        ],
        out_specs=pl.BlockSpec((tb, d), lambda i, bo: (i, 0)),
        scratch_shapes=[pltpu.VMEM((n_exp * win, d), BF16), pltpu.SemaphoreType.DMA((1,))],
    )
    return pl.pallas_call(
        functools.partial(_combine_kernel, batch=batch, cap=cap, win=win, adv=adv),
        grid_spec=grid_spec,
        out_shape=jax.ShapeDtypeStruct((n, d), F32),
        compiler_params=_params("arbitrary"),
        name="ec_combine",
    )(boff_flat, x1, posm, y2d)


COMBINE_TOKENS = 128
COMBINE_WINDOW = 48
COMBINE_ADVANCE = 32


def _tiles(n, d, ff):
    pick = lambda total, want: want if total % want == 0 else total
    return dict(tm_in=pick(n, 1024), tn_in=1024, tq=pick(n, 256), tm_out=pick(n, 512),
                tf=pick(ff, 256))


def kernel(x, ln1_g, w_in, qn_a, kn_a, rpb_a, on_a, qn_b, kn_b, lam_q1, lam_k1, lam_q2, lam_k2,
           subln_b, w_out, ln2_g, w_router, w_gate, w_up, w_down):
    b, s, d = x.shape
    n = b * s
    depth = w_in.shape[0]
    n_exp = w_router.shape[-1]
    cap = EC_CAPACITY_FACTOR * s // n_exp
    t = _tiles(n, d, w_gate.shape[-1])
    slopes = 2.0 ** (-8.0 * jnp.arange(1, DA_HEADS + 1, dtype=F32) / DA_HEADS)
    ones_v = jnp.ones((NA_WIDTH,), F32)
    x2d = x.reshape(n, d)
    for l in range(depth):
        tile_heads = lambda g, heads: jnp.tile(g.astype(F32), heads)
        col_gain = jnp.concatenate([
            tile_heads(qn_a[l], NA_HEADS), tile_heads(kn_a[l], NA_HEADS), ones_v,
            tile_heads(qn_b[l], 2 * DA_HEADS), tile_heads(kn_b[l], 2 * DA_HEADS),
            jnp.ones((DA_WIDTH,), F32)]).reshape(1, IN_COLS)
        proj = _in_projection(x2d, ln1_g[l], w_in[l].astype(BF16), col_gain,
                              tm=t["tm_in"], tn=t["tn_in"])
        proj3d = proj.reshape(b, s, IN_COLS)
        oa = _na_attention(proj3d, _na_bias_table(rpb_a[l]))
        ob = _da_attention(proj3d, slopes, lam_q1[l], lam_k1[l], lam_q2[l], lam_k2[l], subln_b[l],
                           lam_init=_lambda_init(l), tq=min(t["tq"], s))
        x1, logits_t = _out_projection(
            oa.reshape(n, NA_WIDTH), ob.reshape(n, DA_WIDTH), x2d, on_a[l],
            w_out[l].astype(BF16), ln2_g[l], w_router[l].T, tm=t["tm_out"])
        idx, gate, posm, boff = _ec_route(logits_t, b, cap=cap, slot_tile=64, tb=COMBINE_TOKENS)
        rows = idx[..., 0] + (jnp.arange(b, dtype=jnp.int32) * s)[:, None, None]
        rows_flat = rows.transpose(1, 0, 2).reshape(-1)
        gate_e = gate.transpose(1, 0, 2, 3).reshape(n_exp, b * cap, 1)
        y = _ec_ffn(rows_flat, x1, ln2_g[l], gate_e, w_gate[l], w_up[l], w_down[l], tf=t["tf"])
        x2d = _ec_combine(boff.reshape(-1), x1, posm, y.reshape(n_exp * b * cap, d), cap=cap,
                          tb=COMBINE_TOKENS, win=COMBINE_WINDOW, adv=COMBINE_ADVANCE)
    return x2d.reshape(b, s, d)
```

```python
import functools
import math

import jax
import jax.numpy as jnp
from jax import lax
from jax.experimental import pallas as pl
from jax.experimental.pallas import tpu as pltpu

F32 = jnp.float32
BF16 = jnp.bfloat16

GRID_W = 64
HEAD_DIM = 128
NA_HEADS = 8
NA_WIDTH = NA_HEADS * HEAD_DIM
NA_WIN_ROWS = 8
NA_WIN_COLS = 16
DA_HEADS = 4
DA_VDIM = 2 * HEAD_DIM
DA_WIDTH = DA_HEADS * DA_VDIM
DA_QK = DA_HEADS * HEAD_DIM
DA_LAYER_LAMBDA_BASE = 0.8
IN_COLS = 3 * NA_WIDTH + 4 * DA_QK + DA_WIDTH
N_EXPERTS = 16
EC_CAPACITY_FACTOR = 2
RMS_EPS = 1e-6
ATTN_SCALE = HEAD_DIM ** -0.5
LOG2_E = math.log2(math.e)
DA_MAX_STATIC_SHIFT = 30.0
DA_BOUND_MARGIN = 1.02

V7X_LANES = 128
V7X_VMEM_LIMIT_BYTES = 56 * 1024 * 1024

NT_DIMS = (((1,), (1,)), ((), ()))


def _lambda_init(layer_idx):
    return DA_LAYER_LAMBDA_BASE - 0.6 * math.exp(-0.3 * layer_idx)


def _rms(y):
    return y * lax.rsqrt(jnp.mean(y * y, axis=-1, keepdims=True) + RMS_EPS)


def _params(*semantics):
    return pltpu.CompilerParams(dimension_semantics=semantics,
                                vmem_limit_bytes=V7X_VMEM_LIMIT_BYTES)


def _inproj_kernel(x_ref, g1_ref, w_ref, gain_ref, o_ref, h_ref, *, tn):
    j = pl.program_id(1)

    @pl.when(j == 0)
    def _():
        h_ref[...] = (_rms(x_ref[...]) * g1_ref[...]).astype(BF16)

    acc = jnp.dot(h_ref[...], w_ref[...], preferred_element_type=F32)
    col0 = j * tn
    qk_cols = (col0 < 2 * NA_WIDTH) | ((col0 >= 3 * NA_WIDTH) & (col0 < 3 * NA_WIDTH + 4 * DA_QK))

    for c in range(tn // HEAD_DIM):
        sl = slice(c * HEAD_DIM, (c + 1) * HEAD_DIM)
        y = acc[:, sl]
        inv = lax.rsqrt(jnp.mean(y * y, axis=-1, keepdims=True) + RMS_EPS)
        o_ref[:, sl] = (y * jnp.where(qk_cols, inv, 1.0) * gain_ref[:, sl]).astype(o_ref.dtype)


def _in_projection(x2d, ln1_g, w_in, col_gain, *, tm, tn):
    n, d = x2d.shape
    cols = w_in.shape[1]
    return pl.pallas_call(
        functools.partial(_inproj_kernel, tn=tn),
        grid=(n // tm, cols // tn),
        in_specs=[
            pl.BlockSpec((tm, d), lambda i, j: (i, 0)),
            pl.BlockSpec((1, d), lambda i, j: (0, 0)),
            pl.BlockSpec((d, tn), lambda i, j: (0, j)),
            pl.BlockSpec((1, tn), lambda i, j: (0, j)),
        ],
        out_specs=pl.BlockSpec((tm, tn), lambda i, j: (i, j)),
        out_shape=jax.ShapeDtypeStruct((n, cols), BF16),
        scratch_shapes=[pltpu.VMEM((tm, d), BF16)],
        compiler_params=_params("parallel", "arbitrary"),
        name="in_projection",
    )(x2d, ln1_g.reshape(1, d), w_in, col_gain)


def _na_bias_table(rpb):
    c = jnp.arange(GRID_W)
    col_off = jnp.clip(c[None, :] - c[:, None], -(NA_WIN_COLS - 1), NA_WIN_COLS - 1) + (NA_WIN_COLS - 1)
    onehot = (col_off[None] == jnp.arange(2 * NA_WIN_COLS - 1)[:, None, None]).astype(F32)
    t = jnp.einsum("hro,oqk->hrqk", rpb.astype(F32), onehot, precision=lax.Precision.HIGHEST)
    cs = jnp.clip(c - NA_WIN_COLS // 2, 0, GRID_W - NA_WIN_COLS)
    col_valid = (c[None, :] >= cs[:, None]) & (c[None, :] < cs[:, None] + NA_WIN_COLS)
    t = jnp.where(col_valid, t, -jnp.inf)
    return jnp.stack([jnp.concatenate([t[:, base + j] for j in range(NA_WIN_ROWS)], axis=-1)
                      for base in range(NA_WIN_ROWS)], axis=1)


def _na_kernel(q_ref, k_ref, v_ref, bias_ref, o_ref, *, rows, group):
    win = NA_WIN_ROWS * GRID_W

    def body(it, carry):
        r = [it * group + g for g in range(group)]
        rs = [jnp.clip(ri - NA_WIN_ROWS // 2, 0, rows - NA_WIN_ROWS) for ri in r]
        q0 = [pl.multiple_of(ri * GRID_W, GRID_W) for ri in r]
        k0 = [pl.multiple_of(rsi * GRID_W, GRID_W) for rsi in rs]
        s = [lax.dot_general(q_ref[pl.ds(q0[g], GRID_W), :], k_ref[pl.ds(k0[g], win), :], NT_DIMS,
                             preferred_element_type=F32) for g in range(group)]
        p, l = [], []
        for g in range(group):
            sg = s[g] * ATTN_SCALE + bias_ref[rs[g] - r[g] + (NA_WIN_ROWS - 1)]
            pg = jnp.exp(sg - jnp.max(sg, axis=-1, keepdims=True))
            l.append(jnp.sum(pg, axis=-1, keepdims=True))
            p.append(pg.astype(BF16))
        for g in range(group):
            o = jnp.dot(p[g], v_ref[pl.ds(k0[g], win), :], preferred_element_type=F32)
            o_ref[pl.ds(q0[g], GRID_W), :] = (o / l[g]).astype(o_ref.dtype)
        return carry

    lax.fori_loop(0, rows // group, body, 0)


def _na_attention(proj3d, bias_table):
    b, s, _ = proj3d.shape
    rows = s // GRID_W
    assert rows >= NA_WIN_ROWS
    qkv_spec = lambda off: pl.BlockSpec((None, s, HEAD_DIM), lambda bi, h: (bi, 0, off + h))
    return pl.pallas_call(
        functools.partial(_na_kernel, rows=rows, group=8),
        grid=(b, NA_HEADS),
        in_specs=[
            qkv_spec(0), qkv_spec(NA_HEADS), qkv_spec(2 * NA_HEADS),
            pl.BlockSpec((None, NA_WIN_ROWS, GRID_W, NA_WIN_ROWS * GRID_W), lambda bi, h: (h, 0, 0, 0)),
        ],
        out_specs=pl.BlockSpec((None, s, HEAD_DIM), lambda bi, h: (bi, 0, h)),
        out_shape=jax.ShapeDtypeStruct((b, s, NA_WIDTH), BF16),
        compiler_params=_params("parallel", "parallel"),
        name="na_attention",
    )(proj3d, proj3d, proj3d, bias_table)


def _da_kernel(par_ref, q1_ref, q2_ref, k1_ref, k2_ref, v_ref,
               lq1_ref, lk1_ref, lq2_ref, lk2_ref, sub_ref, o_ref, alibi_ref, z_ref, p_ref,
               *, lam_init, kc):
    h = pl.program_id(1)
    qi = pl.program_id(2)
    tq = q1_ref.shape[0]
    s_len = k1_ref.shape[0]
    lam = (jnp.exp(jnp.sum(lq1_ref[...] * lk1_ref[...], axis=-1, keepdims=True))
           - jnp.exp(jnp.sum(lq2_ref[...] * lk2_ref[...], axis=-1, keepdims=True))
           + lam_init)

    bound = par_ref[DA_HEADS]
    static_shift = bound <= DA_MAX_STATIC_SHIFT

    @pl.when(qi == 0)
    def _():
        i = lax.broadcasted_iota(jnp.int32, alibi_ref.shape, 0)
        x = lax.broadcasted_iota(jnp.int32, alibi_ref.shape, 1)
        dist = jnp.abs(x - i - (s_len - tq)).astype(F32)
        alibi_ref[...] = -LOG2_E * (par_ref[h] * dist + jnp.where(static_shift, bound, 0.0))

    win0 = s_len - tq - qi * tq
    n_chunks = s_len // kc
    heads = ((q1_ref, k1_ref), (q2_ref, k2_ref))

    def scores(q, k_ref, c):
        s = lax.dot_general(q, k_ref[c * kc:(c + 1) * kc, :], NT_DIMS, preferred_element_type=F32)
        return s * (ATTN_SCALE * LOG2_E) + alibi_ref[:, pl.ds(pl.multiple_of(win0 + c * kc, V7X_LANES), kc)]

    def finish(o1, l1, o2, l2):
        o = o1 * (1.0 / l1) - o2 * (lam / l2)
        o_ref[...] = (_rms(o) * sub_ref[...] * (1.0 - lam_init)).astype(o_ref.dtype)

    @pl.when(static_shift)
    def _():
        outs = []
        for t, (q_ref, k_ref) in enumerate(heads):
            q = q_ref[...]
            l_lanes = jnp.zeros((tq, V7X_LANES), F32)
            for c in range(n_chunks):
                p = jnp.exp2(scores(q, k_ref, c))
                for j in range(kc // V7X_LANES):
                    l_lanes = l_lanes + p[:, j * V7X_LANES:(j + 1) * V7X_LANES]
                p_ref[t, :, c * kc:(c + 1) * kc] = p.astype(BF16)
            o = jnp.dot(p_ref[t], v_ref[...], preferred_element_type=F32)
            outs += [o, jnp.sum(l_lanes, axis=-1, keepdims=True)]
        finish(*outs)

    @pl.when(jnp.logical_not(static_shift))
    def _():
        outs = []
        for t, (q_ref, k_ref) in enumerate(heads):
            q = q_ref[...]
            m = jnp.full((tq, 1), -jnp.inf, F32)
            for c in range(n_chunks):
                z = scores(q, k_ref, c)
                z_ref[:, c * kc:(c + 1) * kc] = z
                m = jnp.maximum(m, jnp.max(z, axis=-1, keepdims=True))
            l = jnp.zeros((tq, 1), F32)
            for c in range(n_chunks):
                p = jnp.exp2(z_ref[:, c * kc:(c + 1) * kc] - m)
                l = l + jnp.sum(p, axis=-1, keepdims=True)
                p_ref[t, :, c * kc:(c + 1) * kc] = p.astype(BF16)
            outs += [jnp.dot(p_ref[t], v_ref[...], preferred_element_type=F32), l]
        finish(*outs)


def _da_attention(proj3d, slopes, qn_b, kn_b, lam_q1, lam_k1, lam_q2, lam_k2, subln, *, lam_init, tq):
    b, s, _ = proj3d.shape
    qk0 = 3 * NA_WIDTH // HEAD_DIM
    v0 = (3 * NA_WIDTH + 4 * DA_QK) // DA_VDIM
    q_spec = lambda off: pl.BlockSpec((None, tq, HEAD_DIM), lambda bi, h, qi, sl: (bi, qi, off + h))
    k_spec = lambda off: pl.BlockSpec((None, s, HEAD_DIM), lambda bi, h, qi, sl: (bi, 0, off + h))
    vec_spec = lambda n: pl.BlockSpec((1, n), lambda bi, h, qi, sl: (0, 0))
    kc = min(512, s)
    bound = DA_BOUND_MARGIN * math.sqrt(HEAD_DIM) * jnp.max(jnp.abs(qn_b)) * jnp.max(jnp.abs(kn_b))
    par = jnp.concatenate([slopes, bound.reshape(1).astype(F32)])
    grid_spec = pltpu.PrefetchScalarGridSpec(
        num_scalar_prefetch=1,
        grid=(b, DA_HEADS, s // tq),
        in_specs=[
            q_spec(qk0), q_spec(qk0 + DA_HEADS), k_spec(qk0 + 2 * DA_HEADS), k_spec(qk0 + 3 * DA_HEADS),
            pl.BlockSpec((None, s, DA_VDIM), lambda bi, h, qi, sl: (bi, 0, v0 + h)),
            vec_spec(HEAD_DIM), vec_spec(HEAD_DIM), vec_spec(HEAD_DIM), vec_spec(HEAD_DIM),
            vec_spec(DA_VDIM),
        ],
        out_specs=pl.BlockSpec((None, tq, DA_VDIM), lambda bi, h, qi, sl: (bi, qi, h)),
        scratch_shapes=[pltpu.VMEM((tq, 2 * s - tq), F32), pltpu.VMEM((tq, s), F32),
                        pltpu.VMEM((2, tq, s), BF16)],
    )
    row = lambda a: a.reshape(1, -1).astype(F32)
    return pl.pallas_call(
        functools.partial(_da_kernel, lam_init=lam_init, kc=kc),
        grid_spec=grid_spec,
        out_shape=jax.ShapeDtypeStruct((b, s, DA_WIDTH), BF16),
        compiler_params=_params("parallel", "parallel", "arbitrary"),
        name="da_attention",
    )(par, proj3d, proj3d, proj3d, proj3d, proj3d,
      row(lam_q1), row(lam_k1), row(lam_q2), row(lam_k2), row(subln))


def _outproj_kernel(oa_ref, ob_ref, x_ref, ona_ref, wa_ref, wb_ref, g2_ref, wr_ref,
                    x1_ref, lg_ref, *, sub):
    n_exp = lg_ref.shape[0]
    tm = x_ref.shape[0]
    for r in range(tm // sub):
        rows = slice(r * sub, (r + 1) * sub)
        oan = (_rms(oa_ref[rows, :].astype(F32)) * ona_ref[...]).astype(BF16)
        acc = jnp.dot(oan, wa_ref[...], preferred_element_type=F32)
        acc = acc + jnp.dot(ob_ref[rows, :], wb_ref[...], preferred_element_type=F32)
        x1 = x_ref[rows, :] + acc
        x1_ref[rows, :] = x1
        h2 = _rms(x1) * g2_ref[...]
        h_hi = h2.astype(BF16)
        h_lo = (h2 - h_hi.astype(F32)).astype(BF16)
        parts = (jnp.dot(h_hi, wr_ref[...], preferred_element_type=F32)
                 + jnp.dot(h_lo, wr_ref[...], preferred_element_type=F32))
        parts_t = parts.T
        lg_ref[:, rows] = parts_t[:n_exp] + parts_t[n_exp:2 * n_exp]


def _out_projection(oa2d, ob2d, x2d, on_a, w_out, ln2_g, w_router, *, tm):
    n, d = x2d.shape
    n_exp = w_router.shape[1]
    assert 2 * n_exp <= V7X_LANES
    wr_hi = w_router.astype(BF16)
    wr_lo = (w_router - wr_hi.astype(F32)).astype(BF16)
    wr_pieces = jnp.concatenate(
        [wr_hi, wr_lo, jnp.zeros((d, V7X_LANES - 2 * n_exp), BF16)], axis=1)
    wa, wb = w_out[:NA_WIDTH], w_out[NA_WIDTH:]
    row_spec = lambda width: pl.BlockSpec((tm, width), lambda i: (i, 0))
    full_spec = lambda a: pl.BlockSpec(a.shape, lambda i: (0, 0))
    on_a = on_a.reshape(1, -1)
    ln2_g = ln2_g.reshape(1, -1)
    return pl.pallas_call(
        functools.partial(_outproj_kernel, sub=min(256, tm)),
        grid=(n // tm,),
        in_specs=[row_spec(NA_WIDTH), row_spec(DA_WIDTH), row_spec(d), full_spec(on_a),
                  full_spec(wa), full_spec(wb), full_spec(ln2_g), full_spec(wr_pieces)],
        out_specs=[row_spec(d), pl.BlockSpec((n_exp, tm), lambda i: (0, i))],
        out_shape=[jax.ShapeDtypeStruct((n, d), F32), jax.ShapeDtypeStruct((n_exp, n), F32)],
        compiler_params=_params("parallel"),
        name="out_projection",
    )(oa2d, ob2d, x2d, on_a, wa, wb, ln2_g, wr_pieces)


def _exclusive_prefix(mask):
    rows, s_len = mask.shape
    m = jnp.where(mask, 1.0, 0.0).astype(BF16)
    r = lax.broadcasted_iota(jnp.int32, (V7X_LANES, V7X_LANES), 0)
    c = lax.broadcasted_iota(jnp.int32, (V7X_LANES, V7X_LANES), 1)
    before = jnp.where(r < c, 1.0, 0.0).astype(BF16)
    ones = jnp.ones((V7X_LANES, V7X_LANES), BF16)
    run = jnp.zeros((rows, V7X_LANES), F32)
    out = []
    for blk in range(s_len // V7X_LANES):
        mb = m[:, blk * V7X_LANES:(blk + 1) * V7X_LANES]
        out.append(run + jnp.dot(mb, before, preferred_element_type=F32))
        run = run + jnp.dot(mb, ones, preferred_element_type=F32)
    return jnp.concatenate(out, axis=1)


def _route_kernel(lg_ref, idx_ref, gate_ref, posm_ref, boff_ref, aff_ref, *, cap, slot_tile, tb):
    lg = lg_ref[...]
    n_exp, s_len = lg.shape
    ex = jnp.exp(lg - jnp.max(lg, axis=0, keepdims=True))
    aff = ex / jnp.sum(ex, axis=0, keepdims=True)
    bits = pltpu.bitcast(aff, jnp.int32)

    thr = jnp.zeros((n_exp, 1), jnp.int32)
    for bit in range(30, -1, -1):
        cand = thr | (1 << bit)
        cnt = jnp.sum(jnp.where(bits >= cand, 1.0, 0.0), axis=1, keepdims=True)
        thr = jnp.where(cnt >= cap, cand, thr)

    gt = bits > thr
    eq = bits == thr
    need = cap - jnp.sum(jnp.where(gt, 1.0, 0.0), axis=1, keepdims=True)
    sel = gt | (eq & (_exclusive_prefix(eq) < need))
    posm_ref[...] = jnp.where(sel, _exclusive_prefix(sel), -1.0)
    aff_ref[...] = aff

    t_idx = lax.broadcasted_iota(jnp.int32, (s_len, V7X_LANES), 0)
    j_idx = lax.broadcasted_iota(jnp.int32, (s_len, V7X_LANES), 1)
    before_block = jnp.where(t_idx < j_idx * tb, 1.0, 0.0).astype(BF16)
    boff = jnp.dot(jnp.where(sel, 1.0, 0.0).astype(BF16), before_block, preferred_element_type=F32)
    boff_ref[...] = boff.astype(jnp.int32)

    tok = lax.broadcasted_iota(jnp.int32, (1, s_len), 1).astype(F32)

    def per_expert(e, carry):
        pos_e = posm_ref[pl.ds(e, 1), :]
        aff_e = aff_ref[pl.ds(e, 1), :]

        def per_tile(st, c2):
            s0 = pl.multiple_of(st * slot_tile, slot_tile)
            slots = (s0 + lax.broadcasted_iota(jnp.int32, (slot_tile, 1), 0)).astype(F32)
            hit = pos_e == slots
            idx = jnp.sum(jnp.where(hit, tok, 0.0), axis=1, keepdims=True)
            gate = jnp.sum(jnp.where(hit, aff_e, 0.0), axis=1, keepdims=True)
            idx_ref[e, pl.ds(s0, slot_tile), :] = idx.astype(jnp.int32)
            gate_ref[e, pl.ds(s0, slot_tile), :] = gate
            return c2

        return lax.fori_loop(0, cap // slot_tile, per_tile, carry)

    lax.fori_loop(0, n_exp, per_expert, 0)


def _ec_route(logits_t, batch, *, cap, slot_tile, tb):
    n_exp, n = logits_t.shape
    s = n // batch
    assert s // tb + 1 <= V7X_LANES
    slot_spec = pl.BlockSpec((None, n_exp, cap, 1), lambda b: (b, 0, 0, 0))
    return pl.pallas_call(
        functools.partial(_route_kernel, cap=cap, slot_tile=slot_tile, tb=tb),
        grid=(batch,),
        in_specs=[pl.BlockSpec((n_exp, s), lambda b: (0, b))],
        out_specs=[slot_spec, slot_spec,
                   pl.BlockSpec((None, n_exp, s), lambda b: (b, 0, 0)),
                   pl.BlockSpec((None, n_exp, V7X_LANES), lambda b: (b, 0, 0))],
        out_shape=[jax.ShapeDtypeStruct((batch, n_exp, cap, 1), jnp.int32),
                   jax.ShapeDtypeStruct((batch, n_exp, cap, 1), F32),
                   jax.ShapeDtypeStruct((batch, n_exp, s), F32),
                   jax.ShapeDtypeStruct((batch, n_exp, V7X_LANES), jnp.int32)],
        scratch_shapes=[pltpu.VMEM((n_exp, s), F32)],
        compiler_params=_params("parallel"),
        name="ec_route",
    )(logits_t)


def _ffn_kernel(rows_ref, x1_hbm, g2_ref, gate_ref, wg_ref, wu_ref, wd_ref, y_ref,
                xs_ref, xsb_ref, acc_ref, sem_ref, *, m_rows, chunk, n_total):
    e = pl.program_id(0)
    f = pl.program_id(1)
    n_exp = pl.num_programs(0)
    n_f = pl.num_programs(1)

    def row_copy(expert, i):
        src = rows_ref[expert * m_rows + jnp.minimum(i, m_rows - 1)]
        return pltpu.make_async_copy(x1_hbm.at[pl.ds(src, 1), :], xs_ref.at[pl.ds(i, 1), :],
                                     sem_ref.at[0])

    def wait_rows(count):
        pltpu.make_async_copy(x1_hbm.at[pl.ds(0, count), :], xs_ref.at[pl.ds(0, count), :],
                              sem_ref.at[0]).wait()

    def normalise_rows():
        xsb_ref[...] = (_rms(xs_ref[pl.ds(0, m_rows), :]) * g2_ref[...]).astype(BF16)

    @pl.when((e == 0) & (f == 0))
    def _():
        def group(ig, carry):
            for u in range(8):
                row_copy(0, ig * 8 + u).start()
            return carry
        lax.fori_loop(0, m_rows // 8, group, 0)
        wait_rows(m_rows)
        normalise_rows()

    @pl.when(f == 0)
    def _():
        acc_ref[...] = jnp.zeros_like(acc_ref)

    nxt = jnp.minimum(e + 1, n_exp - 1)
    for u in range(chunk):
        row_copy(nxt, f * chunk + u).start()

    xs = xsb_ref[...]
    g = jnp.dot(xs, wg_ref[...].astype(BF16), preferred_element_type=F32)
    u = jnp.dot(xs, wu_ref[...].astype(BF16), preferred_element_type=F32)
    a = (g * jax.nn.sigmoid(g) * u).astype(BF16)
    acc_ref[...] += jnp.dot(a, wd_ref[...].astype(BF16), preferred_element_type=F32)

    @pl.when(f == n_f - 1)
    def _():
        y_ref[...] = (acc_ref[...] * gate_ref[...]).astype(y_ref.dtype)
        wait_rows(n_total)
        normalise_rows()


def _ec_ffn(rows_flat, x1, ln2_g, gate, w_gate, w_up, w_down, *, tf):
    n, d = x1.shape
    n_exp, _, ff = w_gate.shape
    m_rows = rows_flat.shape[0] // n_exp
    n_f = ff // tf
    assert m_rows % 8 == 0
    chunk = -(-m_rows // (8 * n_f)) * 8
    n_total = chunk * n_f
    assert n_total - m_rows < m_rows
    grid_spec = pltpu.PrefetchScalarGridSpec(
        num_scalar_prefetch=1,
        grid=(n_exp, n_f),
        in_specs=[
            pl.BlockSpec(memory_space=pl.ANY),
            pl.BlockSpec((1, d), lambda e, f, r: (0, 0)),
            pl.BlockSpec((None, m_rows, 1), lambda e, f, r: (e, 0, 0)),
            pl.BlockSpec((None, d, tf), lambda e, f, r: (e, 0, f)),
            pl.BlockSpec((None, d, tf), lambda e, f, r: (e, 0, f)),
            pl.BlockSpec((None, tf, d), lambda e, f, r: (e, f, 0)),
        ],
        out_specs=pl.BlockSpec((None, m_rows, d), lambda e, f, r: (e, 0, 0)),
        scratch_shapes=[pltpu.VMEM((n_total, d), F32), pltpu.VMEM((m_rows, d), BF16),
                        pltpu.VMEM((m_rows, d), F32), pltpu.SemaphoreType.DMA((1,))],
    )
    return pl.pallas_call(
        functools.partial(_ffn_kernel, m_rows=m_rows, chunk=chunk, n_total=n_total),
        grid_spec=grid_spec,
        out_shape=jax.ShapeDtypeStruct((n_exp, m_rows, d), BF16),
        compiler_params=_params("arbitrary", "arbitrary"),
        name="ec_ffn",
    )(rows_flat, x1, ln2_g.reshape(1, d), gate, w_gate, w_up, w_down)


def _combine_kernel(boff_ref, x1_ref, posm_ref, y_hbm, o_ref, ywin_ref, sem_ref,
                    *, batch, cap, win, adv):
    n_exp, tb = posm_ref.shape
    i = pl.program_id(0)
    n_steps = pl.num_programs(0)
    nblk = n_steps // batch
    buf = i % 2

    def block_offsets(step, shift):
        bb = step // nblk
        return bb, [boff_ref[(bb * n_exp + e) * V7X_LANES + step % nblk + shift] for e in range(n_exp)]

    def window_copies(bb, lowers, to_buf):
        copies, starts = [], []
        for e in range(n_exp):
            start = jnp.minimum((lowers[e] // 16) * 16, cap - win)
            row0 = pl.multiple_of((e * batch + bb) * cap + start, 16)
            copies.append(pltpu.make_async_copy(y_hbm.at[pl.ds(row0, win), :],
                                                ywin_ref.at[to_buf, pl.ds(e * win, win), :],
                                                sem_ref.at[to_buf]))
            starts.append(start)
        return copies, starts

    b, offs = block_offsets(i, 0)
    _, nxts = block_offsets(i, 1)
    most = functools.reduce(jnp.maximum, [nx - of for nx, of in zip(nxts, offs)])

    @pl.when(i == 0)
    def _():
        for cp in window_copies(b, offs, buf)[0]:
            cp.start()

    @pl.when(i + 1 < n_steps)
    def _():
        b_next, offs_next = block_offsets(i + 1, 0)
        for cp in window_copies(b_next, offs_next, 1 - buf)[0]:
            cp.start()

    o_ref[...] = x1_ref[...]
    posm = posm_ref[...]
    prow = jnp.concatenate([jnp.broadcast_to(posm[e:e + 1, :], (win, tb)) for e in range(n_exp)], axis=0)

    def scatter_round(lowers, fetch):
        copies, starts = window_copies(b, lowers, buf)
        if fetch:
            for cp in copies:
                cp.start()
        cols = []
        for e in range(n_exp):
            slot = starts[e] + lax.broadcasted_iota(jnp.int32, (win, 1), 0)
            cols.append(jnp.where((slot >= lowers[e]) & (slot < lowers[e] + adv), slot, -2).astype(F32))
        hit_t = jnp.where(prow == jnp.concatenate(cols, axis=0), 1.0, 0.0).astype(BF16)
        for cp in copies:
            cp.wait()
        o_ref[...] += lax.dot_general(hit_t, ywin_ref[buf], (((0,), (0,)), ((), ())),
                                      preferred_element_type=F32)

    scatter_round(offs, fetch=False)

    def extra_round(k, carry):
        scatter_round([of + k * adv for of in offs], fetch=True)
        return carry

    lax.fori_loop(1, (most + adv - 1) // adv, extra_round, 0)


def _ec_combine(boff_flat, x1, posm, y2d, *, cap, tb, win, adv):
    n, d = x1.shape
    batch, n_exp, s = posm.shape
    nblk = s // tb
    assert cap % 16 == 0 and win % 16 == 0 and win >= adv + 16 and cap >= win
    grid_spec = pltpu.PrefetchScalarGridSpec(
        num_scalar_prefetch=1,
        grid=(n // tb,),
        in_specs=[
            pl.BlockSpec((tb, d), lambda i, bo: (i, 0)),
            pl.BlockSpec((None, n_exp, tb), lambda i, bo: (i // nblk, 0, i % nblk)),
            pl.BlockSpec(memory_space=pl.ANY),
        ],
        out_specs=pl.BlockSpec((tb, d), lambda i, bo: (i, 0)),
        scratch_shapes=[pltpu.VMEM((2, n_exp * win, d), BF16), pltpu.SemaphoreType.DMA((2,))],
    )
    return pl.pallas_call(
        functools.partial(_combine_kernel, batch=batch, cap=cap, win=win, adv=adv),
        grid_spec=grid_spec,
        out_shape=jax.ShapeDtypeStruct((n, d), F32),
        compiler_params=_params("arbitrary"),
        name="ec_combine",
    )(boff_flat, x1, posm, y2d)


COMBINE_TOKENS = 128
COMBINE_WINDOW = 48
COMBINE_ADVANCE = 32


def _tiles(n, d, ff):
    pick = lambda total, want: want if total % want == 0 else total
    return dict(tm_in=pick(n, 1024), tn_in=1024, tq=pick(n, 256), tm_out=pick(n, 512),
                tf=pick(ff, 256))


def kernel(x, ln1_g, w_in, qn_a, kn_a, rpb_a, on_a, qn_b, kn_b, lam_q1, lam_k1, lam_q2, lam_k2,
           subln_b, w_out, ln2_g, w_router, w_gate, w_up, w_down):
    b, s, d = x.shape
    n = b * s
    depth = w_in.shape[0]
    n_exp = w_router.shape[-1]
    cap = EC_CAPACITY_FACTOR * s // n_exp
    t = _tiles(n, d, w_gate.shape[-1])
    slopes = 2.0 ** (-8.0 * jnp.arange(1, DA_HEADS + 1, dtype=F32) / DA_HEADS)
    ones_v = jnp.ones((NA_WIDTH,), F32)
    x2d = x.reshape(n, d)
    for l in range(depth):
        tile_heads = lambda g, heads: jnp.tile(g.astype(F32), heads)
        col_gain = jnp.concatenate([
            tile_heads(qn_a[l], NA_HEADS), tile_heads(kn_a[l], NA_HEADS), ones_v,
            tile_heads(qn_b[l], 2 * DA_HEADS), tile_heads(kn_b[l], 2 * DA_HEADS),
            jnp.ones((DA_WIDTH,), F32)]).reshape(1, IN_COLS)
        proj = _in_projection(x2d, ln1_g[l], w_in[l].astype(BF16), col_gain,
                              tm=t["tm_in"], tn=t["tn_in"])
        proj3d = proj.reshape(b, s, IN_COLS)
        oa = _na_attention(proj3d, _na_bias_table(rpb_a[l]))
        ob = _da_attention(proj3d, slopes, qn_b[l], kn_b[l], lam_q1[l], lam_k1[l], lam_q2[l], lam_k2[l], subln_b[l],
                           lam_init=_lambda_init(l), tq=min(t["tq"], s))
        x1, logits_t = _out_projection(
            oa.reshape(n, NA_WIDTH), ob.reshape(n, DA_WIDTH), x2d, on_a[l],
            w_out[l].astype(BF16), ln2_g[l], w_router[l].astype(F32), tm=t["tm_out"])
        idx, gate, posm, boff = _ec_route(logits_t, b, cap=cap, slot_tile=64, tb=COMBINE_TOKENS)
        rows = idx[..., 0] + (jnp.arange(b, dtype=jnp.int32) * s)[:, None, None]
        rows_flat = rows.transpose(1, 0, 2).reshape(-1)
        gate_e = gate.transpose(1, 0, 2, 3).reshape(n_exp, b * cap, 1)
        y = _ec_ffn(rows_flat, x1, ln2_g[l], gate_e, w_gate[l], w_up[l], w_down[l], tf=t["tf"])
        x2d = _ec_combine(boff.reshape(-1), x1, posm, y.reshape(n_exp * b * cap, d), cap=cap,
                          tb=COMBINE_TOKENS, win=COMBINE_WINDOW, adv=COMBINE_ADVANCE)
    return x2d.reshape(b, s, d)
```

```python
import functools
import math

import jax
import jax.numpy as jnp
from jax import lax
from jax.experimental import pallas as pl
from jax.experimental.pallas import tpu as pltpu

F32 = jnp.float32
BF16 = jnp.bfloat16

GRID_W = 64
HEAD_DIM = 128
NA_HEADS = 8
NA_WIDTH = NA_HEADS * HEAD_DIM
NA_WIN_ROWS = 8
NA_WIN_COLS = 16
DA_HEADS = 4
DA_VDIM = 2 * HEAD_DIM
DA_WIDTH = DA_HEADS * DA_VDIM
DA_QK = DA_HEADS * HEAD_DIM
DA_LAYER_LAMBDA_BASE = 0.8
IN_COLS = 3 * NA_WIDTH + 4 * DA_QK + DA_WIDTH
N_EXPERTS = 16
EC_CAPACITY_FACTOR = 2
RMS_EPS = 1e-6
ATTN_SCALE = HEAD_DIM ** -0.5
LOG2_E = math.log2(math.e)
DA_MAX_STATIC_SHIFT = 30.0
DA_BOUND_MARGIN = 1.02

V7X_LANES = 128
V7X_VMEM_LIMIT_BYTES = 56 * 1024 * 1024

NT_DIMS = (((1,), (1,)), ((), ()))


def _lambda_init(layer_idx):
    return DA_LAYER_LAMBDA_BASE - 0.6 * math.exp(-0.3 * layer_idx)


def _rms(y):
    return y * lax.rsqrt(jnp.mean(y * y, axis=-1, keepdims=True) + RMS_EPS)


def _params(*semantics):
    return pltpu.CompilerParams(dimension_semantics=semantics,
                                vmem_limit_bytes=V7X_VMEM_LIMIT_BYTES)


def _inproj_kernel(x_ref, g1_ref, w_ref, gain_ref, o_ref, h_ref, *, tn):
    j = pl.program_id(1)

    @pl.when(j == 0)
    def _():
        h_ref[...] = (_rms(x_ref[...]) * g1_ref[...]).astype(BF16)

    acc = jnp.dot(h_ref[...], w_ref[...].astype(BF16), preferred_element_type=F32)
    col0 = j * tn
    qk_cols = (col0 < 2 * NA_WIDTH) | ((col0 >= 3 * NA_WIDTH) & (col0 < 3 * NA_WIDTH + 4 * DA_QK))

    for c in range(tn // HEAD_DIM):
        sl = slice(c * HEAD_DIM, (c + 1) * HEAD_DIM)
        y = acc[:, sl]
        inv = lax.rsqrt(jnp.mean(y * y, axis=-1, keepdims=True) + RMS_EPS)
        o_ref[:, sl] = (y * jnp.where(qk_cols, inv, 1.0) * gain_ref[:, sl]).astype(o_ref.dtype)


def _in_projection(x2d, ln1_g, w_in, col_gain, *, tm, tn):
    n, d = x2d.shape
    cols = w_in.shape[1]
    return pl.pallas_call(
        functools.partial(_inproj_kernel, tn=tn),
        grid=(n // tm, cols // tn),
        in_specs=[
            pl.BlockSpec((tm, d), lambda i, j: (i, 0)),
            pl.BlockSpec((1, d), lambda i, j: (0, 0)),
            pl.BlockSpec((d, tn), lambda i, j: (0, j)),
            pl.BlockSpec((1, tn), lambda i, j: (0, j)),
        ],
        out_specs=pl.BlockSpec((tm, tn), lambda i, j: (i, j)),
        out_shape=jax.ShapeDtypeStruct((n, cols), BF16),
        scratch_shapes=[pltpu.VMEM((tm, d), BF16)],
        compiler_params=_params("parallel", "arbitrary"),
        name="in_projection",
    )(x2d, ln1_g.reshape(1, d), w_in, col_gain)


def _na_bias_table(rpb):
    c = jnp.arange(GRID_W)
    col_off = jnp.clip(c[None, :] - c[:, None], -(NA_WIN_COLS - 1), NA_WIN_COLS - 1) + (NA_WIN_COLS - 1)
    onehot = (col_off[None] == jnp.arange(2 * NA_WIN_COLS - 1)[:, None, None]).astype(F32)
    rows = jnp.stack([rpb[:, base:base + NA_WIN_ROWS] for base in range(NA_WIN_ROWS)], axis=1)
    t = jnp.einsum("hbjo,oqk->hbqjk", rows.astype(F32), onehot, precision=lax.Precision.HIGHEST)
    cs = jnp.clip(c - NA_WIN_COLS // 2, 0, GRID_W - NA_WIN_COLS)
    col_valid = (c[None, :] >= cs[:, None]) & (c[None, :] < cs[:, None] + NA_WIN_COLS)
    t = jnp.where(col_valid[:, None, :], t, -jnp.inf)
    return t.reshape(rpb.shape[0], NA_WIN_ROWS, GRID_W, NA_WIN_ROWS * GRID_W)


def _na_kernel(q_ref, k_ref, v_ref, bias_ref, o_ref, *, rows, group):
    win = NA_WIN_ROWS * GRID_W

    def body(it, carry):
        r = [it * group + g for g in range(group)]
        rs = [jnp.clip(ri - NA_WIN_ROWS // 2, 0, rows - NA_WIN_ROWS) for ri in r]
        q0 = [pl.multiple_of(ri * GRID_W, GRID_W) for ri in r]
        k0 = [pl.multiple_of(rsi * GRID_W, GRID_W) for rsi in rs]
        s = [lax.dot_general(q_ref[pl.ds(q0[g], GRID_W), :], k_ref[pl.ds(k0[g], win), :], NT_DIMS,
                             preferred_element_type=F32) for g in range(group)]
        p, l = [], []
        for g in range(group):
            sg = s[g] * ATTN_SCALE + bias_ref[rs[g] - r[g] + (NA_WIN_ROWS - 1)]
            pg = jnp.exp(sg - jnp.max(sg, axis=-1, keepdims=True))
            l.append(jnp.sum(pg, axis=-1, keepdims=True))
            p.append(pg.astype(BF16))
        for g in range(group):
            o = jnp.dot(p[g], v_ref[pl.ds(k0[g], win), :], preferred_element_type=F32)
            o_ref[pl.ds(q0[g], GRID_W), :] = (o / l[g]).astype(o_ref.dtype)
        return carry

    lax.fori_loop(0, rows // group, body, 0)


def _na_attention(proj3d, bias_table):
    b, s, _ = proj3d.shape
    rows = s // GRID_W
    assert rows >= NA_WIN_ROWS
    qkv_spec = lambda off: pl.BlockSpec((None, s, HEAD_DIM), lambda bi, h: (bi, 0, off + h))
    return pl.pallas_call(
        functools.partial(_na_kernel, rows=rows, group=8),
        grid=(b, NA_HEADS),
        in_specs=[
            qkv_spec(0), qkv_spec(NA_HEADS), qkv_spec(2 * NA_HEADS),
            pl.BlockSpec((None, NA_WIN_ROWS, GRID_W, NA_WIN_ROWS * GRID_W), lambda bi, h: (h, 0, 0, 0)),
        ],
        out_specs=pl.BlockSpec((None, s, HEAD_DIM), lambda bi, h: (bi, 0, h)),
        out_shape=jax.ShapeDtypeStruct((b, s, NA_WIDTH), BF16),
        compiler_params=_params("parallel", "parallel"),
        name="na_attention",
    )(proj3d, proj3d, proj3d, bias_table)


def _da_kernel(par_ref, q1_ref, q2_ref, k1_ref, k2_ref, v_ref,
               lq1_ref, lk1_ref, lq2_ref, lk2_ref, sub_ref, o_ref, alibi_ref, z_ref, p_ref,
               *, lam_init, kc):
    h = pl.program_id(1)
    qi = pl.program_id(2)
    tq = q1_ref.shape[0]
    s_len = k1_ref.shape[0]
    lam = (jnp.exp(jnp.sum(lq1_ref[...] * lk1_ref[...], axis=-1, keepdims=True))
           - jnp.exp(jnp.sum(lq2_ref[...] * lk2_ref[...], axis=-1, keepdims=True))
           + lam_init)

    bound = par_ref[DA_HEADS]
    static_shift = bound <= DA_MAX_STATIC_SHIFT

    @pl.when(qi == 0)
    def _():
        i = lax.broadcasted_iota(jnp.int32, alibi_ref.shape, 0)
        x = lax.broadcasted_iota(jnp.int32, alibi_ref.shape, 1)
        dist = jnp.abs(x - i - (s_len - tq)).astype(F32)
        alibi_ref[...] = -LOG2_E * (par_ref[h] * dist + jnp.where(static_shift, bound, 0.0))

    win0 = s_len - tq - qi * tq
    n_chunks = s_len // kc
    heads = ((q1_ref, k1_ref), (q2_ref, k2_ref))

    def scores(q, k_ref, c):
        s = lax.dot_general(q, k_ref[c * kc:(c + 1) * kc, :], NT_DIMS, preferred_element_type=F32)
        return s * (ATTN_SCALE * LOG2_E) + alibi_ref[:, pl.ds(pl.multiple_of(win0 + c * kc, V7X_LANES), kc)]

    def finish(o1, l1, o2, l2):
        o = o1 * (1.0 / l1) - o2 * (lam / l2)
        o_ref[...] = (_rms(o) * sub_ref[...] * (1.0 - lam_init)).astype(o_ref.dtype)

    @pl.when(static_shift)
    def _():
        outs = []
        for t, (q_ref, k_ref) in enumerate(heads):
            q = q_ref[...]
            l_lanes = jnp.zeros((tq, V7X_LANES), F32)
            for c in range(n_chunks):
                p = jnp.exp2(scores(q, k_ref, c))
                for j in range(kc // V7X_LANES):
                    l_lanes = l_lanes + p[:, j * V7X_LANES:(j + 1) * V7X_LANES]
                p_ref[t, :, c * kc:(c + 1) * kc] = p.astype(BF16)
            o = jnp.dot(p_ref[t], v_ref[...], preferred_element_type=F32)
            outs += [o, jnp.sum(l_lanes, axis=-1, keepdims=True)]
        finish(*outs)

    @pl.when(jnp.logical_not(static_shift))
    def _():
        outs = []
        for t, (q_ref, k_ref) in enumerate(heads):
            q = q_ref[...]
            m = jnp.full((tq, 1), -jnp.inf, F32)
            for c in range(n_chunks):
                z = scores(q, k_ref, c)
                z_ref[:, c * kc:(c + 1) * kc] = z
                m = jnp.maximum(m, jnp.max(z, axis=-1, keepdims=True))
            l = jnp.zeros((tq, 1), F32)
            for c in range(n_chunks):
                p = jnp.exp2(z_ref[:, c * kc:(c + 1) * kc] - m)
                l = l + jnp.sum(p, axis=-1, keepdims=True)
                p_ref[t, :, c * kc:(c + 1) * kc] = p.astype(BF16)
            outs += [jnp.dot(p_ref[t], v_ref[...], preferred_element_type=F32), l]
        finish(*outs)


def _da_attention(proj3d, slopes, qn_b, kn_b, lam_q1, lam_k1, lam_q2, lam_k2, subln, *, lam_init, tq):
    b, s, _ = proj3d.shape
    qk0 = 3 * NA_WIDTH // HEAD_DIM
    v0 = (3 * NA_WIDTH + 4 * DA_QK) // DA_VDIM
    q_spec = lambda off: pl.BlockSpec((None, tq, HEAD_DIM), lambda bi, h, qi, sl: (bi, qi, off + h))
    k_spec = lambda off: pl.BlockSpec((None, s, HEAD_DIM), lambda bi, h, qi, sl: (bi, 0, off + h))
    vec_spec = lambda n: pl.BlockSpec((1, n), lambda bi, h, qi, sl: (0, 0))
    kc = min(512, s)
    bound = DA_BOUND_MARGIN * math.sqrt(HEAD_DIM) * jnp.max(jnp.abs(qn_b)) * jnp.max(jnp.abs(kn_b))
    par = jnp.concatenate([slopes, bound.reshape(1).astype(F32)])
    grid_spec = pltpu.PrefetchScalarGridSpec(
        num_scalar_prefetch=1,
        grid=(b, DA_HEADS, s // tq),
        in_specs=[
            q_spec(qk0), q_spec(qk0 + DA_HEADS), k_spec(qk0 + 2 * DA_HEADS), k_spec(qk0 + 3 * DA_HEADS),
            pl.BlockSpec((None, s, DA_VDIM), lambda bi, h, qi, sl: (bi, 0, v0 + h)),
            vec_spec(HEAD_DIM), vec_spec(HEAD_DIM), vec_spec(HEAD_DIM), vec_spec(HEAD_DIM),
            vec_spec(DA_VDIM),
        ],
        out_specs=pl.BlockSpec((None, tq, DA_VDIM), lambda bi, h, qi, sl: (bi, qi, h)),
        scratch_shapes=[pltpu.VMEM((tq, 2 * s - tq), F32), pltpu.VMEM((tq, s), F32),
                        pltpu.VMEM((2, tq, s), BF16)],
    )
    row = lambda a: a.reshape(1, -1).astype(F32)
    return pl.pallas_call(
        functools.partial(_da_kernel, lam_init=lam_init, kc=kc),
        grid_spec=grid_spec,
        out_shape=jax.ShapeDtypeStruct((b, s, DA_WIDTH), BF16),
        compiler_params=_params("parallel", "parallel", "arbitrary"),
        name="da_attention",
    )(par, proj3d, proj3d, proj3d, proj3d, proj3d,
      row(lam_q1), row(lam_k1), row(lam_q2), row(lam_k2), row(subln))


def _outproj_kernel(oa_ref, ob_ref, x_ref, ona_ref, wa_ref, wb_ref, g2_ref, wr_ref,
                    x1_ref, lg_ref, *, sub):
    n_exp = lg_ref.shape[0]
    tm = x_ref.shape[0]
    for r in range(tm // sub):
        rows = slice(r * sub, (r + 1) * sub)
        oan = (_rms(oa_ref[rows, :].astype(F32)) * ona_ref[...]).astype(BF16)
        acc = jnp.dot(oan, wa_ref[...], preferred_element_type=F32)
        acc = acc + jnp.dot(ob_ref[rows, :], wb_ref[...], preferred_element_type=F32)
        x1 = x_ref[rows, :] + acc
        x1_ref[rows, :] = x1
        h2 = _rms(x1) * g2_ref[...]
        h_hi = h2.astype(BF16)
        h_lo = (h2 - h_hi.astype(F32)).astype(BF16)
        parts = (jnp.dot(h_hi, wr_ref[...], preferred_element_type=F32)
                 + jnp.dot(h_lo, wr_ref[...], preferred_element_type=F32))
        parts_t = parts.T
        lg_ref[:, rows] = parts_t[:n_exp] + parts_t[n_exp:2 * n_exp]


def _out_projection(oa2d, ob2d, x2d, on_a, w_out, ln2_g, w_router, *, tm):
    n, d = x2d.shape
    n_exp = w_router.shape[1]
    assert 2 * n_exp <= V7X_LANES
    wr_hi = w_router.astype(BF16)
    wr_lo = (w_router - wr_hi.astype(F32)).astype(BF16)
    wr_pieces = jnp.concatenate(
        [wr_hi, wr_lo, jnp.zeros((d, V7X_LANES - 2 * n_exp), BF16)], axis=1)
    assert DA_WIDTH == NA_WIDTH
    half_spec = lambda half: pl.BlockSpec((NA_WIDTH, d), lambda i: (half, 0))
    row_spec = lambda width: pl.BlockSpec((tm, width), lambda i: (i, 0))
    full_spec = lambda a: pl.BlockSpec(a.shape, lambda i: (0, 0))
    on_a = on_a.reshape(1, -1)
    ln2_g = ln2_g.reshape(1, -1)
    return pl.pallas_call(
        functools.partial(_outproj_kernel, sub=min(256, tm)),
        grid=(n // tm,),
        in_specs=[row_spec(NA_WIDTH), row_spec(DA_WIDTH), row_spec(d), full_spec(on_a),
                  half_spec(0), half_spec(1), full_spec(ln2_g), full_spec(wr_pieces)],
        out_specs=[row_spec(d), pl.BlockSpec((n_exp, tm), lambda i: (0, i))],
        out_shape=[jax.ShapeDtypeStruct((n, d), F32), jax.ShapeDtypeStruct((n_exp, n), F32)],
        compiler_params=_params("parallel"),
        name="out_projection",
    )(oa2d, ob2d, x2d, on_a, w_out, w_out, ln2_g, wr_pieces)


def _exclusive_prefix(mask):
    rows, s_len = mask.shape
    m = jnp.where(mask, 1.0, 0.0).astype(BF16)
    r = lax.broadcasted_iota(jnp.int32, (V7X_LANES, V7X_LANES), 0)
    c = lax.broadcasted_iota(jnp.int32, (V7X_LANES, V7X_LANES), 1)
    before = jnp.where(r < c, 1.0, 0.0).astype(BF16)
    ones = jnp.ones((V7X_LANES, V7X_LANES), BF16)
    run = jnp.zeros((rows, V7X_LANES), F32)
    out = []
    for blk in range(s_len // V7X_LANES):
        mb = m[:, blk * V7X_LANES:(blk + 1) * V7X_LANES]
        out.append(run + jnp.dot(mb, before, preferred_element_type=F32))
        run = run + jnp.dot(mb, ones, preferred_element_type=F32)
    return jnp.concatenate(out, axis=1)


def _route_kernel(lg_ref, idx_ref, gate_ref, posm_ref, boff_ref, aff_ref, *, cap, slot_tile, tb):
    lg = lg_ref[...]
    n_exp, s_len = lg.shape
    ex = jnp.exp(lg - jnp.max(lg, axis=0, keepdims=True))
    aff = ex / jnp.sum(ex, axis=0, keepdims=True)
    bits = pltpu.bitcast(aff, jnp.int32)

    thr = jnp.zeros((n_exp, 1), jnp.int32)
    for bit in range(30, -1, -1):
        cand = thr | (1 << bit)
        cnt = jnp.sum(jnp.where(bits >= cand, 1.0, 0.0), axis=1, keepdims=True)
        thr = jnp.where(cnt >= cap, cand, thr)

    gt = bits > thr
    eq = bits == thr
    need = cap - jnp.sum(jnp.where(gt, 1.0, 0.0), axis=1, keepdims=True)
    sel = gt | (eq & (_exclusive_prefix(eq) < need))
    posm_ref[...] = jnp.where(sel, _exclusive_prefix(sel), -1.0)
    aff_ref[...] = aff

    t_idx = lax.broadcasted_iota(jnp.int32, (s_len, V7X_LANES), 0)
    j_idx = lax.broadcasted_iota(jnp.int32, (s_len, V7X_LANES), 1)
    before_block = jnp.where(t_idx < j_idx * tb, 1.0, 0.0).astype(BF16)
    boff = jnp.dot(jnp.where(sel, 1.0, 0.0).astype(BF16), before_block, preferred_element_type=F32)
    boff_ref[...] = boff.astype(jnp.int32)

    tok = lax.broadcasted_iota(jnp.int32, (1, s_len), 1).astype(F32)

    def per_expert(e, carry):
        pos_e = posm_ref[pl.ds(e, 1), :]
        aff_e = aff_ref[pl.ds(e, 1), :]

        def per_tile(st, c2):
            s0 = pl.multiple_of(st * slot_tile, slot_tile)
            slots = (s0 + lax.broadcasted_iota(jnp.int32, (slot_tile, 1), 0)).astype(F32)
            hit = pos_e == slots
            idx = jnp.sum(jnp.where(hit, tok, 0.0), axis=1, keepdims=True)
            gate = jnp.sum(jnp.where(hit, aff_e, 0.0), axis=1, keepdims=True)
            idx_ref[e, pl.ds(s0, slot_tile), :] = idx.astype(jnp.int32)
            gate_ref[e, pl.ds(s0, slot_tile), :] = gate
            return c2

        return lax.fori_loop(0, cap // slot_tile, per_tile, carry)

    lax.fori_loop(0, n_exp, per_expert, 0)


def _ec_route(logits_t, batch, *, cap, slot_tile, tb):
    n_exp, n = logits_t.shape
    s = n // batch
    assert s // tb + 1 <= V7X_LANES
    slot_spec = pl.BlockSpec((n_exp, None, cap, 1), lambda b: (0, b, 0, 0))
    return pl.pallas_call(
        functools.partial(_route_kernel, cap=cap, slot_tile=slot_tile, tb=tb),
        grid=(batch,),
        in_specs=[pl.BlockSpec((n_exp, s), lambda b: (0, b))],
        out_specs=[slot_spec, slot_spec,
                   pl.BlockSpec((None, n_exp, s), lambda b: (b, 0, 0)),
                   pl.BlockSpec((None, n_exp, V7X_LANES), lambda b: (b, 0, 0))],
        out_shape=[jax.ShapeDtypeStruct((n_exp, batch, cap, 1), jnp.int32),
                   jax.ShapeDtypeStruct((n_exp, batch, cap, 1), F32),
                   jax.ShapeDtypeStruct((batch, n_exp, s), F32),
                   jax.ShapeDtypeStruct((batch, n_exp, V7X_LANES), jnp.int32)],
        scratch_shapes=[pltpu.VMEM((n_exp, s), F32)],
        compiler_params=_params("parallel"),
        name="ec_route",
    )(logits_t)


def _ffn_kernel(rows_ref, x1_hbm, g2_ref, gate_ref, wg_ref, wu_ref, wd_ref, y_ref,
                xs_ref, xsb_ref, acc_ref, sem_ref, *, m_rows, chunk, n_total):
    e = pl.program_id(0)
    f = pl.program_id(1)
    n_exp = pl.num_programs(0)
    n_f = pl.num_programs(1)

    def row_copy(expert, i):
        src = rows_ref[expert * m_rows + jnp.minimum(i, m_rows - 1)]
        return pltpu.make_async_copy(x1_hbm.at[pl.ds(src, 1), :], xs_ref.at[pl.ds(i, 1), :],
                                     sem_ref.at[0])

    def wait_rows(count):
        pltpu.make_async_copy(x1_hbm.at[pl.ds(0, count), :], xs_ref.at[pl.ds(0, count), :],
                              sem_ref.at[0]).wait()

    def normalise_rows():
        xsb_ref[...] = (_rms(xs_ref[pl.ds(0, m_rows), :]) * g2_ref[...]).astype(BF16)

    @pl.when((e == 0) & (f == 0))
    def _():
        def group(ig, carry):
            for u in range(8):
                row_copy(0, ig * 8 + u).start()
            return carry
        lax.fori_loop(0, m_rows // 8, group, 0)
        wait_rows(m_rows)
        normalise_rows()

    @pl.when(f == 0)
    def _():
        acc_ref[...] = jnp.zeros_like(acc_ref)

    nxt = jnp.minimum(e + 1, n_exp - 1)
    for u in range(chunk):
        row_copy(nxt, f * chunk + u).start()

    xs = xsb_ref[...]
    g = jnp.dot(xs, wg_ref[...].astype(BF16), preferred_element_type=F32)
    u = jnp.dot(xs, wu_ref[...].astype(BF16), preferred_element_type=F32)
    a = (g * jax.nn.sigmoid(g) * u).astype(BF16)
    acc_ref[...] += jnp.dot(a, wd_ref[...].astype(BF16), preferred_element_type=F32)

    @pl.when(f == n_f - 1)
    def _():
        y_ref[...] = (acc_ref[...] * gate_ref[...]).astype(y_ref.dtype)
        wait_rows(n_total)
        normalise_rows()


def _ec_ffn(rows_flat, x1, ln2_g, gate, w_gate, w_up, w_down, *, tf):
    n, d = x1.shape
    n_exp, _, ff = w_gate.shape
    m_rows = rows_flat.shape[0] // n_exp
    n_f = ff // tf
    assert m_rows % 8 == 0
    chunk = -(-m_rows // (8 * n_f)) * 8
    n_total = chunk * n_f
    assert n_total - m_rows < m_rows
    grid_spec = pltpu.PrefetchScalarGridSpec(
        num_scalar_prefetch=1,
        grid=(n_exp, n_f),
        in_specs=[
            pl.BlockSpec(memory_space=pl.ANY),
            pl.BlockSpec((1, d), lambda e, f, r: (0, 0)),
            pl.BlockSpec((None, m_rows, 1), lambda e, f, r: (e, 0, 0)),
            pl.BlockSpec((None, d, tf), lambda e, f, r: (e, 0, f)),
            pl.BlockSpec((None, d, tf), lambda e, f, r: (e, 0, f)),
            pl.BlockSpec((None, tf, d), lambda e, f, r: (e, f, 0)),
        ],
        out_specs=pl.BlockSpec((None, m_rows, d), lambda e, f, r: (e, 0, 0)),
        scratch_shapes=[pltpu.VMEM((n_total, d), F32), pltpu.VMEM((m_rows, d), BF16),
                        pltpu.VMEM((m_rows, d), F32), pltpu.SemaphoreType.DMA((1,))],
    )
    return pl.pallas_call(
        functools.partial(_ffn_kernel, m_rows=m_rows, chunk=chunk, n_total=n_total),
        grid_spec=grid_spec,
        out_shape=jax.ShapeDtypeStruct((n_exp, m_rows, d), BF16),
        compiler_params=_params("arbitrary", "arbitrary"),
        name="ec_ffn",
    )(rows_flat, x1, ln2_g.reshape(1, d), gate, w_gate, w_up, w_down)


def _combine_kernel(boff_ref, x1_ref, posm_ref, y_hbm, o_ref, ywin_ref, sem_ref,
                    *, batch, cap, win, adv):
    n_exp, tb = posm_ref.shape
    i = pl.program_id(0)
    n_steps = pl.num_programs(0)
    nblk = n_steps // batch
    buf = i % 2

    def block_offsets(step, shift):
        bb = step // nblk
        return bb, [boff_ref[(bb * n_exp + e) * V7X_LANES + step % nblk + shift] for e in range(n_exp)]

    def window_copies(bb, lowers, to_buf):
        copies, starts = [], []
        for e in range(n_exp):
            start = jnp.minimum((lowers[e] // 16) * 16, cap - win)
            row0 = pl.multiple_of((e * batch + bb) * cap + start, 16)
            copies.append(pltpu.make_async_copy(y_hbm.at[pl.ds(row0, win), :],
                                                ywin_ref.at[to_buf, pl.ds(e * win, win), :],
                                                sem_ref.at[to_buf]))
            starts.append(start)
        return copies, starts

    b, offs = block_offsets(i, 0)
    _, nxts = block_offsets(i, 1)
    most = functools.reduce(jnp.maximum, [nx - of for nx, of in zip(nxts, offs)])

    @pl.when(i == 0)
    def _():
        for cp in window_copies(b, offs, buf)[0]:
            cp.start()

    @pl.when(i + 1 < n_steps)
    def _():
        b_next, offs_next = block_offsets(i + 1, 0)
        for cp in window_copies(b_next, offs_next, 1 - buf)[0]:
            cp.start()

    o_ref[...] = x1_ref[...]
    posm = posm_ref[...]
    prow = jnp.concatenate([jnp.broadcast_to(posm[e:e + 1, :], (win, tb)) for e in range(n_exp)], axis=0)

    def scatter_round(lowers, fetch):
        copies, starts = window_copies(b, lowers, buf)
        if fetch:
            for cp in copies:
                cp.start()
        cols = []
        for e in range(n_exp):
            slot = starts[e] + lax.broadcasted_iota(jnp.int32, (win, 1), 0)
            cols.append(jnp.where((slot >= lowers[e]) & (slot < lowers[e] + adv), slot, -2).astype(F32))
        hit_t = jnp.where(prow == jnp.concatenate(cols, axis=0), 1.0, 0.0).astype(BF16)
        for cp in copies:
            cp.wait()
        o_ref[...] += lax.dot_general(hit_t, ywin_ref[buf], (((0,), (0,)), ((), ())),
                                      preferred_element_type=F32)

    scatter_round(offs, fetch=False)

    def extra_round(k, carry):
        scatter_round([of + k * adv for of in offs], fetch=True)
        return carry

    lax.fori_loop(1, (most + adv - 1) // adv, extra_round, 0)


def _ec_combine(boff_flat, x1, posm, y2d, *, cap, tb, win, adv):
    n, d = x1.shape
    batch, n_exp, s = posm.shape
    nblk = s // tb
    assert cap % 16 == 0 and win % 16 == 0 and win >= adv + 16 and cap >= win
    grid_spec = pltpu.PrefetchScalarGridSpec(
        num_scalar_prefetch=1,
        grid=(n // tb,),
        in_specs=[
            pl.BlockSpec((tb, d), lambda i, bo: (i, 0)),
            pl.BlockSpec((None, n_exp, tb), lambda i, bo: (i // nblk, 0, i % nblk)),
            pl.BlockSpec(memory_space=pl.ANY),
        ],
        out_specs=pl.BlockSpec((tb, d), lambda i, bo: (i, 0)),
        scratch_shapes=[pltpu.VMEM((2, n_exp * win, d), BF16), pltpu.SemaphoreType.DMA((2,))],
    )
    return pl.pallas_call(
        functools.partial(_combine_kernel, batch=batch, cap=cap, win=win, adv=adv),
        grid_spec=grid_spec,
        out_shape=jax.ShapeDtypeStruct((n, d), F32),
        compiler_params=_params("arbitrary"),
        name="ec_combine",
    )(boff_flat, x1, posm, y2d)


COMBINE_TOKENS = 128
COMBINE_WINDOW = 48
COMBINE_ADVANCE = 32


def _tiles(n, d, ff):
    pick = lambda total, want: want if total % want == 0 else total
    return dict(tm_in=pick(n, 1024), tn_in=1024, tq=pick(n, 512), tm_out=pick(n, 512),
                tf=pick(ff, 256))


def kernel(x, ln1_g, w_in, qn_a, kn_a, rpb_a, on_a, qn_b, kn_b, lam_q1, lam_k1, lam_q2, lam_k2,
           subln_b, w_out, ln2_g, w_router, w_gate, w_up, w_down):
    b, s, d = x.shape
    n = b * s
    depth = w_in.shape[0]
    n_exp = w_router.shape[-1]
    cap = EC_CAPACITY_FACTOR * s // n_exp
    t = _tiles(n, d, w_gate.shape[-1])
    slopes = 2.0 ** (-8.0 * jnp.arange(1, DA_HEADS + 1, dtype=F32) / DA_HEADS)
    ones_v = jnp.ones((NA_WIDTH,), F32)
    x2d = x.reshape(n, d)
    for l in range(depth):
        tile_heads = lambda g, heads: jnp.tile(g.astype(F32), heads)
        col_gain = jnp.concatenate([
            tile_heads(qn_a[l], NA_HEADS), tile_heads(kn_a[l], NA_HEADS), ones_v,
            tile_heads(qn_b[l], 2 * DA_HEADS), tile_heads(kn_b[l], 2 * DA_HEADS),
            jnp.ones((DA_WIDTH,), F32)]).reshape(1, IN_COLS)
        proj = _in_projection(x2d, ln1_g[l], w_in[l], col_gain,
                              tm=t["tm_in"], tn=t["tn_in"])
        proj3d = proj.reshape(b, s, IN_COLS)
        oa = _na_attention(proj3d, _na_bias_table(rpb_a[l]))
        ob = _da_attention(proj3d, slopes, qn_b[l], kn_b[l], lam_q1[l], lam_k1[l], lam_q2[l], lam_k2[l], subln_b[l],
                           lam_init=_lambda_init(l), tq=min(t["tq"], s))
        x1, logits_t = _out_projection(
            oa.reshape(n, NA_WIDTH), ob.reshape(n, DA_WIDTH), x2d, on_a[l],
            w_out[l].astype(BF16), ln2_g[l], w_router[l].astype(F32), tm=t["tm_out"])
        idx, gate, posm, boff = _ec_route(logits_t, b, cap=cap, slot_tile=64, tb=COMBINE_TOKENS)
        rows_flat = (idx[..., 0] + (jnp.arange(b, dtype=jnp.int32) * s)[None, :, None]).reshape(-1)
        gate_e = gate.reshape(n_exp, b * cap, 1)
        y = _ec_ffn(rows_flat, x1, ln2_g[l], gate_e, w_gate[l], w_up[l], w_down[l], tf=t["tf"])
        x2d = _ec_combine(boff.reshape(-1), x1, posm, y.reshape(n_exp * b * cap, d), cap=cap,
                          tb=COMBINE_TOKENS, win=COMBINE_WINDOW, adv=COMBINE_ADVANCE)
    return x2d.reshape(b, s, d)
```

```python
import functools
import math

import jax
import jax.numpy as jnp
from jax import lax
from jax.experimental import pallas as pl
from jax.experimental.pallas import tpu as pltpu

F32 = jnp.float32
BF16 = jnp.bfloat16

GRID_W = 64
HEAD_DIM = 128
NA_HEADS = 8
NA_WIDTH = NA_HEADS * HEAD_DIM
NA_WIN_ROWS = 8
NA_WIN_COLS = 16
DA_HEADS = 4
DA_VDIM = 2 * HEAD_DIM
DA_WIDTH = DA_HEADS * DA_VDIM
DA_QK = DA_HEADS * HEAD_DIM
DA_LAYER_LAMBDA_BASE = 0.8
IN_COLS = 3 * NA_WIDTH + 4 * DA_QK + DA_WIDTH
N_EXPERTS = 16
EC_CAPACITY_FACTOR = 2
RMS_EPS = 1e-6
ATTN_SCALE = HEAD_DIM ** -0.5
LOG2_E = math.log2(math.e)
MAX_STATIC_SHIFT = 30.0
QK_BOUND_MARGIN = 1.02

V7X_LANES = 128
V7X_VMEM_LIMIT_BYTES = 56 * 1024 * 1024

NT_DIMS = (((1,), (1,)), ((), ()))


def _lambda_init(layer_idx):
    return DA_LAYER_LAMBDA_BASE - 0.6 * math.exp(-0.3 * layer_idx)


def _rms(y):
    return y * lax.rsqrt(jnp.mean(y * y, axis=-1, keepdims=True) + RMS_EPS)


def _qk_score_bound(q_gain, k_gain):
    return (QK_BOUND_MARGIN * math.sqrt(HEAD_DIM) * jnp.max(jnp.abs(q_gain.astype(F32)))
            * jnp.max(jnp.abs(k_gain.astype(F32))))


def _params(*semantics):
    return pltpu.CompilerParams(dimension_semantics=semantics,
                                vmem_limit_bytes=V7X_VMEM_LIMIT_BYTES)


def _inproj_kernel(x_ref, g1_ref, w_ref, gain_ref, o_ref, h_ref, *, tn):
    j = pl.program_id(1)

    @pl.when(j == 0)
    def _():
        h_ref[...] = (_rms(x_ref[...]) * g1_ref[...]).astype(BF16)

    acc = jnp.dot(h_ref[...], w_ref[...].astype(BF16), preferred_element_type=F32)
    col0 = j * tn
    qk_cols = (col0 < 2 * NA_WIDTH) | ((col0 >= 3 * NA_WIDTH) & (col0 < 3 * NA_WIDTH + 4 * DA_QK))

    for c in range(tn // HEAD_DIM):
        sl = slice(c * HEAD_DIM, (c + 1) * HEAD_DIM)
        y = acc[:, sl]
        inv = lax.rsqrt(jnp.mean(y * y, axis=-1, keepdims=True) + RMS_EPS)
        o_ref[:, sl] = (y * jnp.where(qk_cols, inv, 1.0) * gain_ref[:, sl]).astype(o_ref.dtype)


def _in_projection(x2d, ln1_g, w_in, col_gain, *, tm, tn):
    n, d = x2d.shape
    cols = w_in.shape[1]
    return pl.pallas_call(
        functools.partial(_inproj_kernel, tn=tn),
        grid=(n // tm, cols // tn),
        in_specs=[
            pl.BlockSpec((tm, d), lambda i, j: (i, 0)),
            pl.BlockSpec((1, d), lambda i, j: (0, 0)),
            pl.BlockSpec((d, tn), lambda i, j: (0, j)),
            pl.BlockSpec((1, tn), lambda i, j: (0, j)),
        ],
        out_specs=pl.BlockSpec((tm, tn), lambda i, j: (i, j)),
        out_shape=jax.ShapeDtypeStruct((n, cols), BF16),
        scratch_shapes=[pltpu.VMEM((tm, d), BF16)],
        compiler_params=_params("parallel", "arbitrary"),
        name="in_projection",
    )(x2d, ln1_g.reshape(1, d), w_in, col_gain)


def _na_bias_table(rpb, qn_a, kn_a):
    rpb = rpb.astype(F32)
    c = jnp.arange(GRID_W)
    col_off = jnp.clip(c[None, :] - c[:, None], -(NA_WIN_COLS - 1), NA_WIN_COLS - 1) + (NA_WIN_COLS - 1)
    onehot = (col_off[None] == jnp.arange(2 * NA_WIN_COLS - 1)[:, None, None]).astype(F32)
    t = jnp.einsum("hro,oqk->hrqk", rpb, onehot, precision=lax.Precision.HIGHEST)
    cs = jnp.clip(c - NA_WIN_COLS // 2, 0, GRID_W - NA_WIN_COLS)
    col_valid = (c[None, :] >= cs[:, None]) & (c[None, :] < cs[:, None] + NA_WIN_COLS)
    qk_bound = _qk_score_bound(qn_a, kn_a)
    static_shift = 2.0 * qk_bound + (jnp.max(rpb) - jnp.min(rpb)) <= 2.0 * MAX_STATIC_SHIFT
    shift = jnp.where(static_shift, qk_bound + jnp.max(rpb), 0.0)
    t = jnp.where(col_valid, (t - shift) * LOG2_E, -jnp.inf)
    pairs = jnp.concatenate([t[:, :-1], t[:, 1:]], axis=-1)
    return pairs, static_shift.astype(jnp.int32).reshape(1)


def _na_kernel(flag_ref, q_ref, k_ref, v_ref, bias_ref, o_ref, *, rows, group):
    win = NA_WIN_ROWS * GRID_W

    def make_body(static_shift):
        def body(it, carry):
            r = [it * group + g for g in range(group)]
            rs = [jnp.clip(ri - NA_WIN_ROWS // 2, 0, rows - NA_WIN_ROWS) for ri in r]
            q0 = [pl.multiple_of(ri * GRID_W, GRID_W) for ri in r]
            k0 = [pl.multiple_of(rsi * GRID_W, GRID_W) for rsi in rs]
            s = [lax.dot_general(q_ref[pl.ds(q0[g], GRID_W), :], k_ref[pl.ds(k0[g], win), :], NT_DIMS,
                                 preferred_element_type=F32) for g in range(group)]
            p, l = [], []
            for g in range(group):
                base = rs[g] - r[g] + (NA_WIN_ROWS - 1)
                bias = jnp.concatenate([bias_ref[base + 2 * jj] for jj in range(NA_WIN_ROWS // 2)], axis=1)
                z = s[g] * (ATTN_SCALE * LOG2_E) + bias
                if not static_shift:
                    z = z - jnp.max(z, axis=-1, keepdims=True)
                pg = jnp.exp2(z)
                l.append(jnp.sum(pg, axis=-1, keepdims=True))
                p.append(pg.astype(BF16))
            for g in range(group):
                o = jnp.dot(p[g], v_ref[pl.ds(k0[g], win), :], preferred_element_type=F32)
                o_ref[pl.ds(q0[g], GRID_W), :] = (o / l[g]).astype(o_ref.dtype)
            return carry
        return body

    @pl.when(flag_ref[0] == 1)
    def _():
        lax.fori_loop(0, rows // group, make_body(True), 0)

    @pl.when(flag_ref[0] != 1)
    def _():
        lax.fori_loop(0, rows // group, make_body(False), 0)


def _na_attention(proj3d, bias_pairs, static_shift):
    b, s, _ = proj3d.shape
    rows = s // GRID_W
    assert rows >= NA_WIN_ROWS
    qkv_spec = lambda off: pl.BlockSpec((None, s, HEAD_DIM), lambda bi, h, fl: (bi, 0, off + h))
    grid_spec = pltpu.PrefetchScalarGridSpec(
        num_scalar_prefetch=1,
        grid=(b, NA_HEADS),
        in_specs=[
            qkv_spec(0), qkv_spec(NA_HEADS), qkv_spec(2 * NA_HEADS),
            pl.BlockSpec((None,) + bias_pairs.shape[1:], lambda bi, h, fl: (h, 0, 0, 0)),
        ],
        out_specs=pl.BlockSpec((None, s, HEAD_DIM), lambda bi, h, fl: (bi, 0, h)),
    )
    return pl.pallas_call(
        functools.partial(_na_kernel, rows=rows, group=8),
        grid_spec=grid_spec,
        out_shape=jax.ShapeDtypeStruct((b, s, NA_WIDTH), BF16),
        compiler_params=_params("parallel", "parallel"),
        name="na_attention",
    )(static_shift, proj3d, proj3d, proj3d, bias_pairs)


def _da_kernel(par_ref, q1_ref, q2_ref, k1_ref, k2_ref, v_ref,
               lq1_ref, lk1_ref, lq2_ref, lk2_ref, sub_ref, o_ref, alibi_ref, z_ref, p_ref,
               *, lam_init, kc):
    h = pl.program_id(0)
    qi = pl.program_id(2)
    tq = q1_ref.shape[0]
    s_len = k1_ref.shape[0]
    lam = (jnp.exp(jnp.sum(lq1_ref[...] * lk1_ref[...], axis=-1, keepdims=True))
           - jnp.exp(jnp.sum(lq2_ref[...] * lk2_ref[...], axis=-1, keepdims=True))
           + lam_init)

    bound = par_ref[DA_HEADS]
    static_shift = bound <= MAX_STATIC_SHIFT

    @pl.when((pl.program_id(1) == 0) & (qi == 0))
    def _():
        i = lax.broadcasted_iota(jnp.int32, (tq, 1), 0).astype(F32)
        x = (lax.broadcasted_iota(jnp.int32, (1, alibi_ref.shape[1]), 1) - (s_len - tq)).astype(F32)
        alibi_ref[...] = (jnp.abs(x - i) * (-LOG2_E * par_ref[h])
                          - LOG2_E * jnp.where(static_shift, bound, 0.0))

    win0 = s_len - tq - qi * tq
    n_chunks = s_len // kc
    heads = ((q1_ref, k1_ref), (q2_ref, k2_ref))

    def scores(q, k_ref, c):
        s = lax.dot_general(q, k_ref[c * kc:(c + 1) * kc, :], NT_DIMS, preferred_element_type=F32)
        return s * (ATTN_SCALE * LOG2_E) + alibi_ref[:, pl.ds(pl.multiple_of(win0 + c * kc, V7X_LANES), kc)]

    def finish(o1, l1, o2, l2):
        o = o1 * (1.0 / l1) - o2 * (lam / l2)
        o_ref[...] = (_rms(o) * sub_ref[...] * (1.0 - lam_init)).astype(o_ref.dtype)

    @pl.when(static_shift)
    def _():
        outs = []
        for t, (q_ref, k_ref) in enumerate(heads):
            q = q_ref[...]
            l_lanes = jnp.zeros((tq, V7X_LANES), F32)
            for c in range(n_chunks):
                p = jnp.exp2(scores(q, k_ref, c))
                for j in range(kc // V7X_LANES):
                    l_lanes = l_lanes + p[:, j * V7X_LANES:(j + 1) * V7X_LANES]
                p_ref[t, :, c * kc:(c + 1) * kc] = p.astype(BF16)
            o = jnp.dot(p_ref[t], v_ref[...], preferred_element_type=F32)
            outs += [o, jnp.sum(l_lanes, axis=-1, keepdims=True)]
        finish(*outs)

    @pl.when(jnp.logical_not(static_shift))
    def _():
        outs = []
        for t, (q_ref, k_ref) in enumerate(heads):
            q = q_ref[...]
            m = jnp.full((tq, 1), -jnp.inf, F32)
            for c in range(n_chunks):
                z = scores(q, k_ref, c)
                z_ref[:, c * kc:(c + 1) * kc] = z
                m = jnp.maximum(m, jnp.max(z, axis=-1, keepdims=True))
            l = jnp.zeros((tq, 1), F32)
            for c in range(n_chunks):
                p = jnp.exp2(z_ref[:, c * kc:(c + 1) * kc] - m)
                l = l + jnp.sum(p, axis=-1, keepdims=True)
                p_ref[t, :, c * kc:(c + 1) * kc] = p.astype(BF16)
            outs += [jnp.dot(p_ref[t], v_ref[...], preferred_element_type=F32), l]
        finish(*outs)


def _da_attention(proj3d, slopes, qn_b, kn_b, lam_q1, lam_k1, lam_q2, lam_k2, subln, *, lam_init, tq):
    b, s, _ = proj3d.shape
    qk0 = 3 * NA_WIDTH // HEAD_DIM
    v0 = (3 * NA_WIDTH + 4 * DA_QK) // DA_VDIM
    q_spec = lambda off: pl.BlockSpec((None, tq, HEAD_DIM), lambda h, bi, qi, sl: (bi, qi, off + h))
    k_spec = lambda off: pl.BlockSpec((None, s, HEAD_DIM), lambda h, bi, qi, sl: (bi, 0, off + h))
    vec_spec = lambda n: pl.BlockSpec((1, n), lambda h, bi, qi, sl: (0, 0))
    kc = min(512, s)
    par = jnp.concatenate([slopes, _qk_score_bound(qn_b, kn_b).reshape(1)])
    grid_spec = pltpu.PrefetchScalarGridSpec(
        num_scalar_prefetch=1,
        grid=(DA_HEADS, b, s // tq),
        in_specs=[
            q_spec(qk0), q_spec(qk0 + DA_HEADS), k_spec(qk0 + 2 * DA_HEADS), k_spec(qk0 + 3 * DA_HEADS),
            pl.BlockSpec((None, s, DA_VDIM), lambda h, bi, qi, sl: (bi, 0, v0 + h)),
            vec_spec(HEAD_DIM), vec_spec(HEAD_DIM), vec_spec(HEAD_DIM), vec_spec(HEAD_DIM),
            vec_spec(DA_VDIM),
        ],
        out_specs=pl.BlockSpec((None, tq, DA_VDIM), lambda h, bi, qi, sl: (bi, qi, h)),
        scratch_shapes=[pltpu.VMEM((tq, 2 * s - tq), F32), pltpu.VMEM((tq, s), F32),
                        pltpu.VMEM((2, tq, s), BF16)],
    )
    row = lambda a: a.reshape(1, -1).astype(F32)
    return pl.pallas_call(
        functools.partial(_da_kernel, lam_init=lam_init, kc=kc),
        grid_spec=grid_spec,
        out_shape=jax.ShapeDtypeStruct((b, s, DA_WIDTH), BF16),
        compiler_params=_params("arbitrary", "arbitrary", "arbitrary"),
        name="da_attention",
    )(par, proj3d, proj3d, proj3d, proj3d, proj3d,
      row(lam_q1), row(lam_k1), row(lam_q2), row(lam_k2), row(subln))


def _outproj_kernel(oa_ref, ob_ref, x_ref, ona_ref, wa_ref, wb_ref, g2_ref, wr_ref,
                    x1_ref, lg_ref, *, sub):
    n_exp = lg_ref.shape[0]
    tm = x_ref.shape[0]
    for r in range(tm // sub):
        rows = slice(r * sub, (r + 1) * sub)
        oan = (_rms(oa_ref[rows, :].astype(F32)) * ona_ref[...]).astype(BF16)
        acc = jnp.dot(oan, wa_ref[...], preferred_element_type=F32)
        acc = acc + jnp.dot(ob_ref[rows, :], wb_ref[...], preferred_element_type=F32)
        x1 = x_ref[rows, :] + acc
        x1_ref[rows, :] = x1
        h2 = _rms(x1) * g2_ref[...]
        h_hi = h2.astype(BF16)
        h_lo = (h2 - h_hi.astype(F32)).astype(BF16)
        parts = (jnp.dot(h_hi, wr_ref[...], preferred_element_type=F32)
                 + jnp.dot(h_lo, wr_ref[...], preferred_element_type=F32))
        parts_t = parts.T
        lg_ref[:, rows] = parts_t[:n_exp] + parts_t[n_exp:2 * n_exp]


def _out_projection(oa2d, ob2d, x2d, on_a, w_out, ln2_g, w_router, *, tm):
    n, d = x2d.shape
    n_exp = w_router.shape[1]
    assert 2 * n_exp <= V7X_LANES
    wr_hi = w_router.astype(BF16)
    wr_lo = (w_router - wr_hi.astype(F32)).astype(BF16)
    wr_pieces = jnp.concatenate(
        [wr_hi, wr_lo, jnp.zeros((d, V7X_LANES - 2 * n_exp), BF16)], axis=1)
    assert DA_WIDTH == NA_WIDTH
    half_spec = lambda half: pl.BlockSpec((NA_WIDTH, d), lambda i: (half, 0))
    row_spec = lambda width: pl.BlockSpec((tm, width), lambda i: (i, 0))
    full_spec = lambda a: pl.BlockSpec(a.shape, lambda i: (0, 0))
    on_a = on_a.reshape(1, -1)
    ln2_g = ln2_g.reshape(1, -1)
    return pl.pallas_call(
        functools.partial(_outproj_kernel, sub=min(256, tm)),
        grid=(n // tm,),
        in_specs=[row_spec(NA_WIDTH), row_spec(DA_WIDTH), row_spec(d), full_spec(on_a),
                  half_spec(0), half_spec(1), full_spec(ln2_g), full_spec(wr_pieces)],
        out_specs=[row_spec(d), pl.BlockSpec((n_exp, tm), lambda i: (0, i))],
        out_shape=[jax.ShapeDtypeStruct((n, d), F32), jax.ShapeDtypeStruct((n_exp, n), F32)],
        compiler_params=_params("parallel"),
        name="out_projection",
    )(oa2d, ob2d, x2d, on_a, w_out, w_out, ln2_g, wr_pieces)


def _exclusive_prefix(mask):
    rows, s_len = mask.shape
    m = jnp.where(mask, 1.0, 0.0).astype(BF16)
    r = lax.broadcasted_iota(jnp.int32, (V7X_LANES, V7X_LANES), 0)
    c = lax.broadcasted_iota(jnp.int32, (V7X_LANES, V7X_LANES), 1)
    before = jnp.where(r < c, 1.0, 0.0).astype(BF16)
    ones = jnp.ones((V7X_LANES, V7X_LANES), BF16)
    run = jnp.zeros((rows, V7X_LANES), F32)
    out = []
    for blk in range(s_len // V7X_LANES):
        mb = m[:, blk * V7X_LANES:(blk + 1) * V7X_LANES]
        out.append(run + jnp.dot(mb, before, preferred_element_type=F32))
        run = run + jnp.dot(mb, ones, preferred_element_type=F32)
    return jnp.concatenate(out, axis=1)


def _route_kernel(lg_ref, idx_ref, gate_ref, posm_ref, boff_ref, aff_ref, *, cap, slot_tile, tb):
    lg = lg_ref[...]
    n_exp, s_len = lg.shape
    ex = jnp.exp(lg - jnp.max(lg, axis=0, keepdims=True))
    aff = ex / jnp.sum(ex, axis=0, keepdims=True)
    bits = pltpu.bitcast(aff, jnp.int32)

    thr = jnp.zeros((n_exp, 1), jnp.int32)
    for bit in range(30, -1, -1):
        cand = thr | (1 << bit)
        cnt = jnp.sum(jnp.where(bits >= cand, 1.0, 0.0), axis=1, keepdims=True)
        thr = jnp.where(cnt >= cap, cand, thr)

    gt = bits > thr
    eq = bits == thr
    need = cap - jnp.sum(jnp.where(gt, 1.0, 0.0), axis=1, keepdims=True)
    sel = gt | (eq & (_exclusive_prefix(eq) < need))
    posm_ref[...] = jnp.where(sel, _exclusive_prefix(sel), -1.0)
    aff_ref[...] = aff

    t_idx = lax.broadcasted_iota(jnp.int32, (s_len, V7X_LANES), 0)
    j_idx = lax.broadcasted_iota(jnp.int32, (s_len, V7X_LANES), 1)
    before_block = jnp.where(t_idx < j_idx * tb, 1.0, 0.0).astype(BF16)
    boff = jnp.dot(jnp.where(sel, 1.0, 0.0).astype(BF16), before_block, preferred_element_type=F32)
    boff_ref[...] = boff.astype(jnp.int32)

    tok = lax.broadcasted_iota(jnp.int32, (1, s_len), 1).astype(F32)

    def per_expert(e, carry):
        pos_e = posm_ref[pl.ds(e, 1), :]
        aff_e = aff_ref[pl.ds(e, 1), :]

        def per_tile(st, c2):
            s0 = pl.multiple_of(st * slot_tile, slot_tile)
            slots = (s0 + lax.broadcasted_iota(jnp.int32, (slot_tile, 1), 0)).astype(F32)
            hit = pos_e == slots
            idx = jnp.sum(jnp.where(hit, tok, 0.0), axis=1, keepdims=True)
            gate = jnp.sum(jnp.where(hit, aff_e, 0.0), axis=1, keepdims=True)
            idx_ref[e, pl.ds(s0, slot_tile), :] = idx.astype(jnp.int32)
            gate_ref[e, pl.ds(s0, slot_tile), :] = gate
            return c2

        return lax.fori_loop(0, cap // slot_tile, per_tile, carry)

    lax.fori_loop(0, n_exp, per_expert, 0)


def _ec_route(logits_t, batch, *, cap, slot_tile, tb):
    n_exp, n = logits_t.shape
    s = n // batch
    assert s // tb + 1 <= V7X_LANES
    slot_spec = pl.BlockSpec((n_exp, None, cap, 1), lambda b: (0, b, 0, 0))
    return pl.pallas_call(
        functools.partial(_route_kernel, cap=cap, slot_tile=slot_tile, tb=tb),
        grid=(batch,),
        in_specs=[pl.BlockSpec((n_exp, s), lambda b: (0, b))],
        out_specs=[slot_spec, slot_spec,
                   pl.BlockSpec((None, n_exp, s), lambda b: (b, 0, 0)),
                   pl.BlockSpec((None, n_exp, V7X_LANES), lambda b: (b, 0, 0))],
        out_shape=[jax.ShapeDtypeStruct((n_exp, batch, cap, 1), jnp.int32),
                   jax.ShapeDtypeStruct((n_exp, batch, cap, 1), F32),
                   jax.ShapeDtypeStruct((batch, n_exp, s), F32),
                   jax.ShapeDtypeStruct((batch, n_exp, V7X_LANES), jnp.int32)],
        scratch_shapes=[pltpu.VMEM((n_exp, s), F32)],
        compiler_params=_params("parallel"),
        name="ec_route",
    )(logits_t)


def _ffn_kernel(rows_ref, x1_hbm, g2_ref, gate_ref, wg_ref, wu_ref, wd_ref, y_ref,
                xs_ref, xsb_ref, acc_ref, sem_ref, *, m_rows, chunk, n_total):
    e = pl.program_id(0)
    f = pl.program_id(1)
    n_exp = pl.num_programs(0)
    n_f = pl.num_programs(1)

    def row_copy(expert, i):
        src = rows_ref[expert * m_rows + jnp.minimum(i, m_rows - 1)]
        return pltpu.make_async_copy(x1_hbm.at[pl.ds(src, 1), :], xs_ref.at[pl.ds(i, 1), :],
                                     sem_ref.at[0])

    def wait_rows(count):
        pltpu.make_async_copy(x1_hbm.at[pl.ds(0, count), :], xs_ref.at[pl.ds(0, count), :],
                              sem_ref.at[0]).wait()

    def normalise_rows():
        xsb_ref[...] = (_rms(xs_ref[pl.ds(0, m_rows), :]) * g2_ref[...]).astype(BF16)

    @pl.when((e == 0) & (f == 0))
    def _():
        def group(ig, carry):
            for u in range(8):
                row_copy(0, ig * 8 + u).start()
            return carry
        lax.fori_loop(0, m_rows // 8, group, 0)
        wait_rows(m_rows)
        normalise_rows()

    @pl.when(f == 0)
    def _():
        acc_ref[...] = jnp.zeros_like(acc_ref)

    nxt = jnp.minimum(e + 1, n_exp - 1)
    for u in range(chunk):
        row_copy(nxt, f * chunk + u).start()

    xs = xsb_ref[...]
    g = jnp.dot(xs, wg_ref[...].astype(BF16), preferred_element_type=F32)
    u = jnp.dot(xs, wu_ref[...].astype(BF16), preferred_element_type=F32)
    a = (g * jax.nn.sigmoid(g) * u).astype(BF16)
    acc_ref[...] += jnp.dot(a, wd_ref[...].astype(BF16), preferred_element_type=F32)

    @pl.when(f == n_f - 1)
    def _():
        y_ref[...] = (acc_ref[...] * gate_ref[...]).astype(y_ref.dtype)
        wait_rows(n_total)
        normalise_rows()


def _ec_ffn(rows_flat, x1, ln2_g, gate, w_gate, w_up, w_down, *, tf):
    n, d = x1.shape
    n_exp, _, ff = w_gate.shape
    m_rows = rows_flat.shape[0] // n_exp
    n_f = ff // tf
    assert m_rows % 8 == 0
    chunk = -(-m_rows // (8 * n_f)) * 8
    n_total = chunk * n_f
    assert n_total - m_rows < m_rows
    grid_spec = pltpu.PrefetchScalarGridSpec(
        num_scalar_prefetch=1,
        grid=(n_exp, n_f),
        in_specs=[
            pl.BlockSpec(memory_space=pl.ANY),
            pl.BlockSpec((1, d), lambda e, f, r: (0, 0)),
            pl.BlockSpec((None, m_rows, 1), lambda e, f, r: (e, 0, 0)),
            pl.BlockSpec((None, d, tf), lambda e, f, r: (e, 0, f)),
            pl.BlockSpec((None, d, tf), lambda e, f, r: (e, 0, f)),
            pl.BlockSpec((None, tf, d), lambda e, f, r: (e, f, 0)),
        ],
        out_specs=pl.BlockSpec((None, m_rows, d), lambda e, f, r: (e, 0, 0)),
        scratch_shapes=[pltpu.VMEM((n_total, d), F32), pltpu.VMEM((m_rows, d), BF16),
                        pltpu.VMEM((m_rows, d), F32), pltpu.SemaphoreType.DMA((1,))],
    )
    return pl.pallas_call(
        functools.partial(_ffn_kernel, m_rows=m_rows, chunk=chunk, n_total=n_total),
        grid_spec=grid_spec,
        out_shape=jax.ShapeDtypeStruct((n_exp, m_rows, d), BF16),
        compiler_params=_params("arbitrary", "arbitrary"),
        name="ec_ffn",
    )(rows_flat, x1, ln2_g.reshape(1, d), gate, w_gate, w_up, w_down)


def _combine_kernel(boff_ref, x1_ref, posm_ref, y_hbm, o_ref, ywin_ref, sem_ref,
                    *, batch, cap, win, adv):
    n_exp, tb = posm_ref.shape
    i = pl.program_id(0)
    n_steps = pl.num_programs(0)
    nblk = n_steps // batch
    buf = i % 2

    def block_offsets(step, shift):
        bb = step // nblk
        return bb, [boff_ref[(bb * n_exp + e) * V7X_LANES + step % nblk + shift] for e in range(n_exp)]

    def window_copies(bb, lowers, to_buf):
        copies, starts = [], []
        for e in range(n_exp):
            start = jnp.minimum((lowers[e] // 16) * 16, cap - win)
            row0 = pl.multiple_of((e * batch + bb) * cap + start, 16)
            copies.append(pltpu.make_async_copy(y_hbm.at[pl.ds(row0, win), :],
                                                ywin_ref.at[to_buf, pl.ds(e * win, win), :],
                                                sem_ref.at[to_buf]))
            starts.append(start)
        return copies, starts

    b, offs = block_offsets(i, 0)
    _, nxts = block_offsets(i, 1)
    most = functools.reduce(jnp.maximum, [nx - of for nx, of in zip(nxts, offs)])

    @pl.when(i == 0)
    def _():
        for cp in window_copies(b, offs, buf)[0]:
            cp.start()

    @pl.when(i + 1 < n_steps)
    def _():
        b_next, offs_next = block_offsets(i + 1, 0)
        for cp in window_copies(b_next, offs_next, 1 - buf)[0]:
            cp.start()

    o_ref[...] = x1_ref[...]
    posm = posm_ref[...]
    prow = jnp.concatenate([jnp.broadcast_to(posm[e:e + 1, :], (win, tb)) for e in range(n_exp)], axis=0)

    def scatter_round(lowers, fetch):
        copies, starts = window_copies(b, lowers, buf)
        if fetch:
            for cp in copies:
                cp.start()
        cols = []
        for e in range(n_exp):
            slot = starts[e] + lax.broadcasted_iota(jnp.int32, (win, 1), 0)
            cols.append(jnp.where((slot >= lowers[e]) & (slot < lowers[e] + adv), slot, -2).astype(F32))
        hit_t = jnp.where(prow == jnp.concatenate(cols, axis=0), 1.0, 0.0).astype(BF16)
        for cp in copies:
            cp.wait()
        o_ref[...] += lax.dot_general(hit_t, ywin_ref[buf], (((0,), (0,)), ((), ())),
                                      preferred_element_type=F32)

    scatter_round(offs, fetch=False)

    def extra_round(k, carry):
        scatter_round([of + k * adv for of in offs], fetch=True)
        return carry

    lax.fori_loop(1, (most + adv - 1) // adv, extra_round, 0)


def _ec_combine(boff_flat, x1, posm, y2d, *, cap, tb, win, adv):
    n, d = x1.shape
    batch, n_exp, s = posm.shape
    nblk = s // tb
    assert cap % 16 == 0 and win % 16 == 0 and win >= adv + 16 and cap >= win
    grid_spec = pltpu.PrefetchScalarGridSpec(
        num_scalar_prefetch=1,
        grid=(n // tb,),
        in_specs=[
            pl.BlockSpec((tb, d), lambda i, bo: (i, 0)),
            pl.BlockSpec((None, n_exp, tb), lambda i, bo: (i // nblk, 0, i % nblk)),
            pl.BlockSpec(memory_space=pl.ANY),
        ],
        out_specs=pl.BlockSpec((tb, d), lambda i, bo: (i, 0)),
        scratch_shapes=[pltpu.VMEM((2, n_exp * win, d), BF16), pltpu.SemaphoreType.DMA((2,))],
    )
    return pl.pallas_call(
        functools.partial(_combine_kernel, batch=batch, cap=cap, win=win, adv=adv),
        grid_spec=grid_spec,
        out_shape=jax.ShapeDtypeStruct((n, d), F32),
        compiler_params=_params("arbitrary"),
        name="ec_combine",
    )(boff_flat, x1, posm, y2d)


COMBINE_TOKENS = 256
COMBINE_WINDOW = 80
COMBINE_ADVANCE = 64


def _tiles(n, d, ff):
    pick = lambda total, want: want if total % want == 0 else total
    return dict(tm_in=pick(n, 1024), tn_in=1024, tq=pick(n, 512), tm_out=pick(n, 512),
                tf=pick(ff, 256))


def kernel(x, ln1_g, w_in, qn_a, kn_a, rpb_a, on_a, qn_b, kn_b, lam_q1, lam_k1, lam_q2, lam_k2,
           subln_b, w_out, ln2_g, w_router, w_gate, w_up, w_down):
    b, s, d = x.shape
    n = b * s
    depth = w_in.shape[0]
    n_exp = w_router.shape[-1]
    cap = EC_CAPACITY_FACTOR * s // n_exp
    t = _tiles(n, d, w_gate.shape[-1])
    slopes = 2.0 ** (-8.0 * jnp.arange(1, DA_HEADS + 1, dtype=F32) / DA_HEADS)
    ones_v = jnp.ones((NA_WIDTH,), F32)
    x2d = x.reshape(n, d)
    for l in range(depth):
        tile_heads = lambda g, heads: jnp.tile(g.astype(F32), heads)
        col_gain = jnp.concatenate([
            tile_heads(qn_a[l], NA_HEADS), tile_heads(kn_a[l], NA_HEADS), ones_v,
            tile_heads(qn_b[l], 2 * DA_HEADS), tile_heads(kn_b[l], 2 * DA_HEADS),
            jnp.ones((DA_WIDTH,), F32)]).reshape(1, IN_COLS)
        proj = _in_projection(x2d, ln1_g[l], w_in[l], col_gain,
                              tm=t["tm_in"], tn=t["tn_in"])
        proj3d = proj.reshape(b, s, IN_COLS)
        oa = _na_attention(proj3d, *_na_bias_table(rpb_a[l], qn_a[l], kn_a[l]))
        ob = _da_attention(proj3d, slopes, qn_b[l], kn_b[l], lam_q1[l], lam_k1[l], lam_q2[l], lam_k2[l], subln_b[l],
                           lam_init=_lambda_init(l), tq=min(t["tq"], s))
        x1, logits_t = _out_projection(
            oa.reshape(n, NA_WIDTH), ob.reshape(n, DA_WIDTH), x2d, on_a[l],
            w_out[l].astype(BF16), ln2_g[l], w_router[l].astype(F32), tm=t["tm_out"])
        idx, gate, posm, boff = _ec_route(logits_t, b, cap=cap, slot_tile=64, tb=COMBINE_TOKENS)
        rows_flat = (idx[..., 0] + (jnp.arange(b, dtype=jnp.int32) * s)[None, :, None]).reshape(-1)
        gate_e = gate.reshape(n_exp, b * cap, 1)
        y = _ec_ffn(rows_flat, x1, ln2_g[l], gate_e, w_gate[l], w_up[l], w_down[l], tf=t["tf"])
        x2d = _ec_combine(boff.reshape(-1), x1, posm, y.reshape(n_exp * b * cap, d), cap=cap,
                          tb=COMBINE_TOKENS, win=COMBINE_WINDOW, adv=COMBINE_ADVANCE)
    return x2d.reshape(b, s, d)
```

```python
import functools
import math

import jax
import jax.numpy as jnp
from jax import lax
from jax.experimental import pallas as pl
from jax.experimental.pallas import tpu as pltpu

F32 = jnp.float32
BF16 = jnp.bfloat16

GRID_W = 64
HEAD_DIM = 128
NA_HEADS = 8
NA_WIDTH = NA_HEADS * HEAD_DIM
NA_WIN_ROWS = 8
NA_WIN_COLS = 16
DA_HEADS = 4
DA_VDIM = 2 * HEAD_DIM
DA_WIDTH = DA_HEADS * DA_VDIM
DA_QK = DA_HEADS * HEAD_DIM
DA_LAYER_LAMBDA_BASE = 0.8
IN_COLS = 3 * NA_WIDTH + 4 * DA_QK + DA_WIDTH
N_EXPERTS = 16
EC_CAPACITY_FACTOR = 2
RMS_EPS = 1e-6
ATTN_SCALE = HEAD_DIM ** -0.5
LOG2_E = math.log2(math.e)
MAX_STATIC_SHIFT = 30.0
QK_BOUND_MARGIN = 1.02

V7X_LANES = 128
V7X_VMEM_LIMIT_BYTES = 56 * 1024 * 1024

NT_DIMS = (((1,), (1,)), ((), ()))


def _lambda_init(layer_idx):
    return DA_LAYER_LAMBDA_BASE - 0.6 * math.exp(-0.3 * layer_idx)


def _rms(y):
    return y * lax.rsqrt(jnp.mean(y * y, axis=-1, keepdims=True) + RMS_EPS)


def _qk_score_bound(q_gain, k_gain):
    return (QK_BOUND_MARGIN * math.sqrt(HEAD_DIM) * jnp.max(jnp.abs(q_gain.astype(F32)))
            * jnp.max(jnp.abs(k_gain.astype(F32))))


def _params(*semantics):
    return pltpu.CompilerParams(dimension_semantics=semantics,
                                vmem_limit_bytes=V7X_VMEM_LIMIT_BYTES)


def _inproj_kernel(x_ref, g1_ref, w_ref, gain_ref, o_ref, h_ref, *, tn):
    j = pl.program_id(1)

    @pl.when(j == 0)
    def _():
        h_ref[...] = (_rms(x_ref[...]) * g1_ref[...]).astype(BF16)

    acc = jnp.dot(h_ref[...], w_ref[...].astype(BF16), preferred_element_type=F32)
    col0 = j * tn
    qk_cols = (col0 < 2 * NA_WIDTH) | ((col0 >= 3 * NA_WIDTH) & (col0 < 3 * NA_WIDTH + 4 * DA_QK))

    for c in range(tn // HEAD_DIM):
        sl = slice(c * HEAD_DIM, (c + 1) * HEAD_DIM)
        y = acc[:, sl]
        inv = lax.rsqrt(jnp.mean(y * y, axis=-1, keepdims=True) + RMS_EPS)
        o_ref[:, sl] = (y * jnp.where(qk_cols, inv, 1.0) * gain_ref[:, sl]).astype(o_ref.dtype)


def _in_projection(x2d, ln1_g, w_in, col_gain, *, tm, tn):
    n, d = x2d.shape
    cols = w_in.shape[1]
    return pl.pallas_call(
        functools.partial(_inproj_kernel, tn=tn),
        grid=(n // tm, cols // tn),
        in_specs=[
            pl.BlockSpec((tm, d), lambda i, j: (i, 0)),
            pl.BlockSpec((1, d), lambda i, j: (0, 0)),
            pl.BlockSpec((d, tn), lambda i, j: (0, j)),
            pl.BlockSpec((1, tn), lambda i, j: (0, j)),
        ],
        out_specs=pl.BlockSpec((tm, tn), lambda i, j: (i, j)),
        out_shape=jax.ShapeDtypeStruct((n, cols), BF16),
        scratch_shapes=[pltpu.VMEM((tm, d), BF16)],
        compiler_params=_params("parallel", "arbitrary"),
        name="in_projection",
    )(x2d, ln1_g.reshape(1, d), w_in, col_gain)


def _na_bias_table(rpb, qn_a, kn_a):
    rpb = rpb.astype(F32)
    c = jnp.arange(GRID_W)
    col_off = jnp.clip(c[None, :] - c[:, None], -(NA_WIN_COLS - 1), NA_WIN_COLS - 1) + (NA_WIN_COLS - 1)
    onehot = (col_off[None] == jnp.arange(2 * NA_WIN_COLS - 1)[:, None, None]).astype(F32)
    t = jnp.einsum("hro,oqk->hrqk", rpb, onehot, precision=lax.Precision.HIGHEST)
    cs = jnp.clip(c - NA_WIN_COLS // 2, 0, GRID_W - NA_WIN_COLS)
    col_valid = (c[None, :] >= cs[:, None]) & (c[None, :] < cs[:, None] + NA_WIN_COLS)
    qk_bound = _qk_score_bound(qn_a, kn_a)
    static_shift = 2.0 * qk_bound + (jnp.max(rpb) - jnp.min(rpb)) <= 2.0 * MAX_STATIC_SHIFT
    shift = jnp.where(static_shift, qk_bound + jnp.max(rpb), 0.0)
    t = jnp.where(col_valid, (t - shift) * LOG2_E, -jnp.inf)
    pairs = jnp.concatenate([t[:, :-1], t[:, 1:]], axis=-1)
    return pairs, static_shift.astype(jnp.int32).reshape(1)


def _na_kernel(flag_ref, q_ref, k_ref, v_ref, bias_ref, o_ref, *, rows, group):
    win = NA_WIN_ROWS * GRID_W

    def make_body(static_shift):
        def body(it, carry):
            r = [it * group + g for g in range(group)]
            rs = [jnp.clip(ri - NA_WIN_ROWS // 2, 0, rows - NA_WIN_ROWS) for ri in r]
            q0 = [pl.multiple_of(ri * GRID_W, GRID_W) for ri in r]
            k0 = [pl.multiple_of(rsi * GRID_W, GRID_W) for rsi in rs]
            s = [lax.dot_general(q_ref[pl.ds(q0[g], GRID_W), :], k_ref[pl.ds(k0[g], win), :], NT_DIMS,
                                 preferred_element_type=F32) for g in range(group)]
            p, l = [], []
            for g in range(group):
                base = rs[g] - r[g] + (NA_WIN_ROWS - 1)
                bias = jnp.concatenate([bias_ref[base + 2 * jj] for jj in range(NA_WIN_ROWS // 2)], axis=1)
                z = s[g] * (ATTN_SCALE * LOG2_E) + bias
                if not static_shift:
                    z = z - jnp.max(z, axis=-1, keepdims=True)
                pg = jnp.exp2(z)
                l.append(jnp.sum(pg, axis=-1, keepdims=True))
                p.append(pg.astype(BF16))
            for g in range(group):
                o = jnp.dot(p[g], v_ref[pl.ds(k0[g], win), :], preferred_element_type=F32)
                o_ref[pl.ds(q0[g], GRID_W), :] = (o / l[g]).astype(o_ref.dtype)
            return carry
        return body

    @pl.when(flag_ref[0] == 1)
    def _():
        lax.fori_loop(0, rows // group, make_body(True), 0)

    @pl.when(flag_ref[0] != 1)
    def _():
        lax.fori_loop(0, rows // group, make_body(False), 0)


def _na_attention(proj3d, bias_pairs, static_shift):
    b, s, _ = proj3d.shape
    rows = s // GRID_W
    assert rows >= NA_WIN_ROWS
    qkv_spec = lambda off: pl.BlockSpec((None, s, HEAD_DIM), lambda bi, h, fl: (bi, 0, off + h))
    grid_spec = pltpu.PrefetchScalarGridSpec(
        num_scalar_prefetch=1,
        grid=(b, NA_HEADS),
        in_specs=[
            qkv_spec(0), qkv_spec(NA_HEADS), qkv_spec(2 * NA_HEADS),
            pl.BlockSpec((None,) + bias_pairs.shape[1:], lambda bi, h, fl: (h, 0, 0, 0)),
        ],
        out_specs=pl.BlockSpec((None, s, HEAD_DIM), lambda bi, h, fl: (bi, 0, h)),
    )
    return pl.pallas_call(
        functools.partial(_na_kernel, rows=rows, group=16),
        grid_spec=grid_spec,
        out_shape=jax.ShapeDtypeStruct((b, s, NA_WIDTH), BF16),
        compiler_params=_params("parallel", "parallel"),
        name="na_attention",
    )(static_shift, proj3d, proj3d, proj3d, bias_pairs)


def _da_kernel(par_ref, q1_ref, q2_ref, k1_ref, k2_ref, v_ref,
               lq1_ref, lk1_ref, lq2_ref, lk2_ref, sub_ref, o_ref, alibi_ref, z_ref, p_ref,
               *, lam_init, kc):
    h = pl.program_id(0)
    qi = pl.program_id(2)
    tq = q1_ref.shape[0]
    s_len = k1_ref.shape[0]
    lam = (jnp.exp(jnp.sum(lq1_ref[...] * lk1_ref[...], axis=-1, keepdims=True))
           - jnp.exp(jnp.sum(lq2_ref[...] * lk2_ref[...], axis=-1, keepdims=True))
           + lam_init)

    bound = par_ref[DA_HEADS]
    static_shift = bound <= MAX_STATIC_SHIFT

    @pl.when((pl.program_id(1) == 0) & (qi == 0))
    def _():
        i = lax.broadcasted_iota(jnp.int32, (tq, 1), 0).astype(F32)
        x = (lax.broadcasted_iota(jnp.int32, (1, alibi_ref.shape[1]), 1) - (s_len - tq)).astype(F32)
        alibi_ref[...] = (jnp.abs(x - i) * (-LOG2_E * par_ref[h])
                          - LOG2_E * jnp.where(static_shift, bound, 0.0))

    win0 = s_len - tq - qi * tq
    n_chunks = s_len // kc
    heads = ((q1_ref, k1_ref), (q2_ref, k2_ref))

    def scores(q, k_ref, c):
        s = lax.dot_general(q, k_ref[c * kc:(c + 1) * kc, :], NT_DIMS, preferred_element_type=F32)
        return s * (ATTN_SCALE * LOG2_E) + alibi_ref[:, pl.ds(pl.multiple_of(win0 + c * kc, V7X_LANES), kc)]

    def finish(o1, l1, o2, l2):
        o = o1 * (1.0 / l1) - o2 * (lam / l2)
        o_ref[...] = (_rms(o) * sub_ref[...] * (1.0 - lam_init)).astype(o_ref.dtype)

    @pl.when(static_shift)
    def _():
        outs = []
        for t, (q_ref, k_ref) in enumerate(heads):
            q = q_ref[...]
            l_lanes = jnp.zeros((tq, V7X_LANES), F32)
            for c in range(n_chunks):
                p = jnp.exp2(scores(q, k_ref, c))
                for j in range(kc // V7X_LANES):
                    l_lanes = l_lanes + p[:, j * V7X_LANES:(j + 1) * V7X_LANES]
                p_ref[t, :, c * kc:(c + 1) * kc] = p.astype(BF16)
            o = jnp.dot(p_ref[t], v_ref[...], preferred_element_type=F32)
            outs += [o, jnp.sum(l_lanes, axis=-1, keepdims=True)]
        finish(*outs)

    @pl.when(jnp.logical_not(static_shift))
    def _():
        outs = []
        for t, (q_ref, k_ref) in enumerate(heads):
            q = q_ref[...]
            m = jnp.full((tq, 1), -jnp.inf, F32)
            for c in range(n_chunks):
                z = scores(q, k_ref, c)
                z_ref[:, c * kc:(c + 1) * kc] = z
                m = jnp.maximum(m, jnp.max(z, axis=-1, keepdims=True))
            l = jnp.zeros((tq, 1), F32)
            for c in range(n_chunks):
                p = jnp.exp2(z_ref[:, c * kc:(c + 1) * kc] - m)
                l = l + jnp.sum(p, axis=-1, keepdims=True)
                p_ref[t, :, c * kc:(c + 1) * kc] = p.astype(BF16)
            outs += [jnp.dot(p_ref[t], v_ref[...], preferred_element_type=F32), l]
        finish(*outs)


def _da_attention(proj3d, slopes, qn_b, kn_b, lam_q1, lam_k1, lam_q2, lam_k2, subln, *, lam_init, tq):
    b, s, _ = proj3d.shape
    qk0 = 3 * NA_WIDTH // HEAD_DIM
    v0 = (3 * NA_WIDTH + 4 * DA_QK) // DA_VDIM
    q_spec = lambda off: pl.BlockSpec((None, tq, HEAD_DIM), lambda h, bi, qi, sl: (bi, qi, off + h))
    k_spec = lambda off: pl.BlockSpec((None, s, HEAD_DIM), lambda h, bi, qi, sl: (bi, 0, off + h))
    vec_spec = lambda n: pl.BlockSpec((1, n), lambda h, bi, qi, sl: (0, 0))
    kc = min(512, s)
    par = jnp.concatenate([slopes, _qk_score_bound(qn_b, kn_b).reshape(1)])
    grid_spec = pltpu.PrefetchScalarGridSpec(
        num_scalar_prefetch=1,
        grid=(DA_HEADS, b, s // tq),
        in_specs=[
            q_spec(qk0), q_spec(qk0 + DA_HEADS), k_spec(qk0 + 2 * DA_HEADS), k_spec(qk0 + 3 * DA_HEADS),
            pl.BlockSpec((None, s, DA_VDIM), lambda h, bi, qi, sl: (bi, 0, v0 + h)),
            vec_spec(HEAD_DIM), vec_spec(HEAD_DIM), vec_spec(HEAD_DIM), vec_spec(HEAD_DIM),
            vec_spec(DA_VDIM),
        ],
        out_specs=pl.BlockSpec((None, tq, DA_VDIM), lambda h, bi, qi, sl: (bi, qi, h)),
        scratch_shapes=[pltpu.VMEM((tq, 2 * s - tq), F32), pltpu.VMEM((tq, s), F32),
                        pltpu.VMEM((2, tq, s), BF16)],
    )
    row = lambda a: a.reshape(1, -1).astype(F32)
    return pl.pallas_call(
        functools.partial(_da_kernel, lam_init=lam_init, kc=kc),
        grid_spec=grid_spec,
        out_shape=jax.ShapeDtypeStruct((b, s, DA_WIDTH), BF16),
        compiler_params=_params("arbitrary", "arbitrary", "arbitrary"),
        name="da_attention",
    )(par, proj3d, proj3d, proj3d, proj3d, proj3d,
      row(lam_q1), row(lam_k1), row(lam_q2), row(lam_k2), row(subln))


def _outproj_kernel(oa_ref, ob_ref, x_ref, ona_ref, wa_ref, wb_ref, g2_ref, wr_ref,
                    x1_ref, lg_ref, *, sub):
    n_exp = lg_ref.shape[0]
    tm = x_ref.shape[0]
    for r in range(tm // sub):
        rows = slice(r * sub, (r + 1) * sub)
        oan = (_rms(oa_ref[rows, :].astype(F32)) * ona_ref[...]).astype(BF16)
        acc = jnp.dot(oan, wa_ref[...], preferred_element_type=F32)
        acc = acc + jnp.dot(ob_ref[rows, :], wb_ref[...], preferred_element_type=F32)
        x1 = x_ref[rows, :] + acc
        x1_ref[rows, :] = x1
        h2 = _rms(x1) * g2_ref[...]
        h_hi = h2.astype(BF16)
        h_lo = (h2 - h_hi.astype(F32)).astype(BF16)
        parts = (jnp.dot(h_hi, wr_ref[...], preferred_element_type=F32)
                 + jnp.dot(h_lo, wr_ref[...], preferred_element_type=F32))
        parts_t = parts.T
        lg_ref[:, rows] = parts_t[:n_exp] + parts_t[n_exp:2 * n_exp]


def _out_projection(oa2d, ob2d, x2d, on_a, w_out, ln2_g, w_router, *, tm):
    n, d = x2d.shape
    n_exp = w_router.shape[1]
    assert 2 * n_exp <= V7X_LANES
    wr_hi = w_router.astype(BF16)
    wr_lo = (w_router - wr_hi.astype(F32)).astype(BF16)
    wr_pieces = jnp.concatenate(
        [wr_hi, wr_lo, jnp.zeros((d, V7X_LANES - 2 * n_exp), BF16)], axis=1)
    assert DA_WIDTH == NA_WIDTH
    half_spec = lambda half: pl.BlockSpec((NA_WIDTH, d), lambda i: (half, 0))
    row_spec = lambda width: pl.BlockSpec((tm, width), lambda i: (i, 0))
    full_spec = lambda a: pl.BlockSpec(a.shape, lambda i: (0, 0))
    on_a = on_a.reshape(1, -1)
    ln2_g = ln2_g.reshape(1, -1)
    return pl.pallas_call(
        functools.partial(_outproj_kernel, sub=min(256, tm)),
        grid=(n // tm,),
        in_specs=[row_spec(NA_WIDTH), row_spec(DA_WIDTH), row_spec(d), full_spec(on_a),
                  half_spec(0), half_spec(1), full_spec(ln2_g), full_spec(wr_pieces)],
        out_specs=[row_spec(d), pl.BlockSpec((n_exp, tm), lambda i: (0, i))],
        out_shape=[jax.ShapeDtypeStruct((n, d), F32), jax.ShapeDtypeStruct((n_exp, n), F32)],
        compiler_params=_params("parallel"),
        name="out_projection",
    )(oa2d, ob2d, x2d, on_a, w_out, w_out, ln2_g, wr_pieces)


def _exclusive_prefix(mask):
    rows, s_len = mask.shape
    m = jnp.where(mask, 1.0, 0.0).astype(BF16)
    r = lax.broadcasted_iota(jnp.int32, (V7X_LANES, V7X_LANES), 0)
    c = lax.broadcasted_iota(jnp.int32, (V7X_LANES, V7X_LANES), 1)
    before = jnp.where(r < c, 1.0, 0.0).astype(BF16)
    ones = jnp.ones((V7X_LANES, V7X_LANES), BF16)
    run = jnp.zeros((rows, V7X_LANES), F32)
    out = []
    for blk in range(s_len // V7X_LANES):
        mb = m[:, blk * V7X_LANES:(blk + 1) * V7X_LANES]
        out.append(run + jnp.dot(mb, before, preferred_element_type=F32))
        run = run + jnp.dot(mb, ones, preferred_element_type=F32)
    return jnp.concatenate(out, axis=1)


def _route_kernel(lg_ref, idx_ref, gate_ref, posm_ref, boff_ref, aff_ref, idx_acc, gate_acc,
                  boff_vmem, boff_smem, sem_ref, *, cap, slot_tile, tb):
    lg = lg_ref[...]
    n_exp, s_len = lg.shape
    ex = jnp.exp(lg - jnp.max(lg, axis=0, keepdims=True))
    aff = ex / jnp.sum(ex, axis=0, keepdims=True)
    bits = pltpu.bitcast(aff, jnp.int32)

    thr = jnp.zeros((n_exp, 1), jnp.int32)
    for bit in range(30, -1, -1):
        cand = thr | (1 << bit)
        cnt = jnp.sum(jnp.where(bits >= cand, 1.0, 0.0), axis=1, keepdims=True)
        thr = jnp.where(cnt >= cap, cand, thr)

    gt = bits > thr
    eq = bits == thr
    need = cap - jnp.sum(jnp.where(gt, 1.0, 0.0), axis=1, keepdims=True)
    sel = gt | (eq & (_exclusive_prefix(eq) < need))
    posm_ref[...] = jnp.where(sel, _exclusive_prefix(sel), -1.0)
    aff_ref[...] = aff

    t_idx = lax.broadcasted_iota(jnp.int32, (s_len, V7X_LANES), 0)
    j_idx = lax.broadcasted_iota(jnp.int32, (s_len, V7X_LANES), 1)
    before_block = jnp.where(t_idx < j_idx * tb, 1.0, 0.0).astype(BF16)
    boff = jnp.dot(jnp.where(sel, 1.0, 0.0).astype(BF16), before_block, preferred_element_type=F32)
    boff_ref[...] = boff.astype(jnp.int32)
    boff_vmem[...] = boff.astype(jnp.int32)
    to_smem = pltpu.make_async_copy(boff_vmem, boff_smem, sem_ref.at[0])
    to_smem.start()
    to_smem.wait()

    n_blk = s_len // tb
    n_tiles = cap // slot_tile
    span = min(s_len, 4 * tb)
    lane_iota = lax.broadcasted_iota(jnp.int32, (1, span), 1)

    def per_expert(e, carry):
        offs = [boff_smem[e, j] for j in range(n_blk + 1)]
        first_tok, n_rounds = [], 0
        for st in range(n_tiles):
            s0 = st * slot_tile
            j_lo = sum((offs[j + 1] <= s0).astype(jnp.int32) for j in range(n_blk))
            j_hi = sum((offs[j] < s0 + slot_tile).astype(jnp.int32) for j in range(n_blk))
            first_tok.append(j_lo * tb)
            n_rounds = jnp.maximum(n_rounds, ((j_hi - j_lo) * tb + span - 1) // span)
        idx_acc[...] = jnp.zeros_like(idx_acc)
        gate_acc[...] = jnp.zeros_like(gate_acc)

        def one_round(r, c2):
            for st in range(n_tiles):
                nominal = first_tok[st] + r * span
                start = pl.multiple_of(jnp.minimum(nominal, s_len - span), V7X_LANES)
                tok_w = (start + lane_iota).astype(F32)
                pos_w = posm_ref[pl.ds(e, 1), pl.ds(start, span)]
                pos_w = jnp.where(tok_w >= nominal.astype(F32), pos_w, -1.0)
                aff_w = aff_ref[pl.ds(e, 1), pl.ds(start, span)]
                slots = (st * slot_tile + lax.broadcasted_iota(jnp.int32, (slot_tile, 1), 0)).astype(F32)
                hit = pos_w == slots
                rows = pl.ds(st * slot_tile, slot_tile)
                idx_acc[rows, :] += jnp.sum(jnp.where(hit, tok_w, 0.0), axis=1, keepdims=True)
                gate_acc[rows, :] += jnp.sum(jnp.where(hit, aff_w, 0.0), axis=1, keepdims=True)
            return c2

        lax.fori_loop(0, n_rounds, one_round, 0)
        idx_ref[e] = idx_acc[...].astype(jnp.int32)
        gate_ref[e] = gate_acc[...]
        return carry

    lax.fori_loop(0, n_exp, per_expert, 0)


def _ec_route(logits_t, batch, *, cap, slot_tile, tb):
    n_exp, n = logits_t.shape
    s = n // batch
    assert s // tb + 1 <= V7X_LANES
    slot_spec = pl.BlockSpec((n_exp, None, cap, 1), lambda b: (0, b, 0, 0))
    return pl.pallas_call(
        functools.partial(_route_kernel, cap=cap, slot_tile=slot_tile, tb=tb),
        grid=(batch,),
        in_specs=[pl.BlockSpec((n_exp, s), lambda b: (0, b))],
        out_specs=[slot_spec, slot_spec,
                   pl.BlockSpec((None, n_exp, s), lambda b: (b, 0, 0)),
                   pl.BlockSpec((None, n_exp, V7X_LANES), lambda b: (b, 0, 0))],
        out_shape=[jax.ShapeDtypeStruct((n_exp, batch, cap, 1), jnp.int32),
                   jax.ShapeDtypeStruct((n_exp, batch, cap, 1), F32),
                   jax.ShapeDtypeStruct((batch, n_exp, s), F32),
                   jax.ShapeDtypeStruct((batch, n_exp, V7X_LANES), jnp.int32)],
        scratch_shapes=[pltpu.VMEM((n_exp, s), F32), pltpu.VMEM((cap, 1), F32), pltpu.VMEM((cap, 1), F32),
                        pltpu.VMEM((n_exp, V7X_LANES), jnp.int32),
                        pltpu.SMEM((n_exp, V7X_LANES), jnp.int32), pltpu.SemaphoreType.DMA((1,))],
        compiler_params=_params("parallel"),
        name="ec_route",
    )(logits_t)


def _ffn_kernel(rows_ref, x1_hbm, g2_ref, gate_ref, wg_ref, wu_ref, wd_ref, y_ref,
                xs_ref, xsb_ref, acc_ref, sem_ref, *, m_rows, chunk, n_total):
    e = pl.program_id(0)
    f = pl.program_id(1)
    n_exp = pl.num_programs(0)
    n_f = pl.num_programs(1)

    def row_copy(expert, i):
        src = rows_ref[expert * m_rows + jnp.minimum(i, m_rows - 1)]
        return pltpu.make_async_copy(x1_hbm.at[pl.ds(src, 1), :], xs_ref.at[pl.ds(i, 1), :],
                                     sem_ref.at[0])

    def wait_rows(count):
        pltpu.make_async_copy(x1_hbm.at[pl.ds(0, count), :], xs_ref.at[pl.ds(0, count), :],
                              sem_ref.at[0]).wait()

    def normalise_rows():
        xsb_ref[...] = (_rms(xs_ref[pl.ds(0, m_rows), :]) * g2_ref[...]).astype(BF16)

    @pl.when((e == 0) & (f == 0))
    def _():
        def group(ig, carry):
            for u in range(8):
                row_copy(0, ig * 8 + u).start()
            return carry
        lax.fori_loop(0, m_rows // 8, group, 0)
        wait_rows(m_rows)
        normalise_rows()

    @pl.when(f == 0)
    def _():
        acc_ref[...] = jnp.zeros_like(acc_ref)

    nxt = jnp.minimum(e + 1, n_exp - 1)
    for u in range(chunk):
        row_copy(nxt, f * chunk + u).start()

    xs = xsb_ref[...]
    g = jnp.dot(xs, wg_ref[...].astype(BF16), preferred_element_type=F32)
    u = jnp.dot(xs, wu_ref[...].astype(BF16), preferred_element_type=F32)
    a = (g * jax.nn.sigmoid(g) * u).astype(BF16)
    acc_ref[...] += jnp.dot(a, wd_ref[...].astype(BF16), preferred_element_type=F32)

    @pl.when(f == n_f - 1)
    def _():
        y_ref[...] = (acc_ref[...] * gate_ref[...]).astype(y_ref.dtype)
        wait_rows(n_total)
        normalise_rows()


def _ec_ffn(rows_flat, x1, ln2_g, gate, w_gate, w_up, w_down, *, tf):
    n, d = x1.shape
    n_exp, _, ff = w_gate.shape
    m_rows = rows_flat.shape[0] // n_exp
    n_f = ff // tf
    assert m_rows % 8 == 0
    chunk = -(-m_rows // (8 * n_f)) * 8
    n_total = chunk * n_f
    assert n_total - m_rows < m_rows
    grid_spec = pltpu.PrefetchScalarGridSpec(
        num_scalar_prefetch=1,
        grid=(n_exp, n_f),
        in_specs=[
            pl.BlockSpec(memory_space=pl.ANY),
            pl.BlockSpec((1, d), lambda e, f, r: (0, 0)),
            pl.BlockSpec((None, m_rows, 1), lambda e, f, r: (e, 0, 0)),
            pl.BlockSpec((None, d, tf), lambda e, f, r: (e, 0, f)),
            pl.BlockSpec((None, d, tf), lambda e, f, r: (e, 0, f)),
            pl.BlockSpec((None, tf, d), lambda e, f, r: (e, f, 0)),
        ],
        out_specs=pl.BlockSpec((None, m_rows, d), lambda e, f, r: (e, 0, 0)),
        scratch_shapes=[pltpu.VMEM((n_total, d), F32), pltpu.VMEM((m_rows, d), BF16),
                        pltpu.VMEM((m_rows, d), F32), pltpu.SemaphoreType.DMA((1,))],
    )
    return pl.pallas_call(
        functools.partial(_ffn_kernel, m_rows=m_rows, chunk=chunk, n_total=n_total),
        grid_spec=grid_spec,
        out_shape=jax.ShapeDtypeStruct((n_exp, m_rows, d), BF16),
        compiler_params=_params("arbitrary", "arbitrary"),
        name="ec_ffn",
    )(rows_flat, x1, ln2_g.reshape(1, d), gate, w_gate, w_up, w_down)


def _combine_kernel(boff_ref, x1_ref, posm_ref, y_hbm, o_ref, ywin_ref, sem_ref,
                    *, batch, cap, win, adv):
    n_exp, tb = posm_ref.shape
    i = pl.program_id(0)
    n_steps = pl.num_programs(0)
    nblk = n_steps // batch
    buf = i % 2

    def block_offsets(step, shift):
        bb = step // nblk
        return bb, [boff_ref[(bb * n_exp + e) * V7X_LANES + step % nblk + shift] for e in range(n_exp)]

    def window_copies(bb, lowers, to_buf):
        copies, starts = [], []
        for e in range(n_exp):
            start = jnp.minimum((lowers[e] // 16) * 16, cap - win)
            row0 = pl.multiple_of((e * batch + bb) * cap + start, 16)
            copies.append(pltpu.make_async_copy(y_hbm.at[pl.ds(row0, win), :],
                                                ywin_ref.at[to_buf, pl.ds(e * win, win), :],
                                                sem_ref.at[to_buf]))
            starts.append(start)
        return copies, starts

    b, offs = block_offsets(i, 0)
    _, nxts = block_offsets(i, 1)
    most = functools.reduce(jnp.maximum, [nx - of for nx, of in zip(nxts, offs)])

    @pl.when(i == 0)
    def _():
        for cp in window_copies(b, offs, buf)[0]:
            cp.start()

    @pl.when(i + 1 < n_steps)
    def _():
        b_next, offs_next = block_offsets(i + 1, 0)
        for cp in window_copies(b_next, offs_next, 1 - buf)[0]:
            cp.start()

    o_ref[...] = x1_ref[...]
    posm = posm_ref[...]
    prow = jnp.concatenate([jnp.broadcast_to(posm[e:e + 1, :], (win, tb)) for e in range(n_exp)], axis=0)

    def scatter_round(lowers, fetch):
        copies, starts = window_copies(b, lowers, buf)
        if fetch:
            for cp in copies:
                cp.start()
        cols = []
        for e in range(n_exp):
            slot = starts[e] + lax.broadcasted_iota(jnp.int32, (win, 1), 0)
            cols.append(jnp.where((slot >= lowers[e]) & (slot < lowers[e] + adv), slot, -2).astype(F32))
        hit_t = jnp.where(prow == jnp.concatenate(cols, axis=0), 1.0, 0.0).astype(BF16)
        for cp in copies:
            cp.wait()
        o_ref[...] += lax.dot_general(hit_t, ywin_ref[buf], (((0,), (0,)), ((), ())),
                                      preferred_element_type=F32)

    scatter_round(offs, fetch=False)

    def extra_round(k, carry):
        scatter_round([of + k * adv for of in offs], fetch=True)
        return carry

    lax.fori_loop(1, (most + adv - 1) // adv, extra_round, 0)


def _ec_combine(boff_flat, x1, posm, y2d, *, cap, tb, win, adv):
    n, d = x1.shape
    batch, n_exp, s = posm.shape
    nblk = s // tb
    assert cap % 16 == 0 and win % 16 == 0 and win >= adv + 16 and cap >= win
    grid_spec = pltpu.PrefetchScalarGridSpec(
        num_scalar_prefetch=1,
        grid=(n // tb,),
        in_specs=[
            pl.BlockSpec((tb, d), lambda i, bo: (i, 0)),
            pl.BlockSpec((None, n_exp, tb), lambda i, bo: (i // nblk, 0, i % nblk)),
            pl.BlockSpec(memory_space=pl.ANY),
        ],
        out_specs=pl.BlockSpec((tb, d), lambda i, bo: (i, 0)),
        scratch_shapes=[pltpu.VMEM((2, n_exp * win, d), BF16), pltpu.SemaphoreType.DMA((2,))],
    )
    return pl.pallas_call(
        functools.partial(_combine_kernel, batch=batch, cap=cap, win=win, adv=adv),
        grid_spec=grid_spec,
        out_shape=jax.ShapeDtypeStruct((n, d), F32),
        compiler_params=_params("arbitrary"),
        name="ec_combine",
    )(boff_flat, x1, posm, y2d)


COMBINE_TOKENS = 256
COMBINE_WINDOW = 80
COMBINE_ADVANCE = 64


def _tiles(n, d, ff):
    pick = lambda total, want: want if total % want == 0 else total
    return dict(tm_in=pick(n, 1024), tn_in=1024, tq=pick(n, 512), tm_out=pick(n, 512),
                tf=pick(ff, 256))


def kernel(x, ln1_g, w_in, qn_a, kn_a, rpb_a, on_a, qn_b, kn_b, lam_q1, lam_k1, lam_q2, lam_k2,
           subln_b, w_out, ln2_g, w_router, w_gate, w_up, w_down):
    b, s, d = x.shape
    n = b * s
    depth = w_in.shape[0]
    n_exp = w_router.shape[-1]
    cap = EC_CAPACITY_FACTOR * s // n_exp
    t = _tiles(n, d, w_gate.shape[-1])
    slopes = 2.0 ** (-8.0 * jnp.arange(1, DA_HEADS + 1, dtype=F32) / DA_HEADS)
    ones_v = jnp.ones((NA_WIDTH,), F32)
    x2d = x.reshape(n, d)
    for l in range(depth):
        tile_heads = lambda g, heads: jnp.tile(g.astype(F32), heads)
        col_gain = jnp.concatenate([
            tile_heads(qn_a[l], NA_HEADS), tile_heads(kn_a[l], NA_HEADS), ones_v,
            tile_heads(qn_b[l], 2 * DA_HEADS), tile_heads(kn_b[l], 2 * DA_HEADS),
            jnp.ones((DA_WIDTH,), F32)]).reshape(1, IN_COLS)
        proj = _in_projection(x2d, ln1_g[l], w_in[l], col_gain,
                              tm=t["tm_in"], tn=t["tn_in"])
        proj3d = proj.reshape(b, s, IN_COLS)
        oa = _na_attention(proj3d, *_na_bias_table(rpb_a[l], qn_a[l], kn_a[l]))
        ob = _da_attention(proj3d, slopes, qn_b[l], kn_b[l], lam_q1[l], lam_k1[l], lam_q2[l], lam_k2[l], subln_b[l],
                           lam_init=_lambda_init(l), tq=min(t["tq"], s))
        x1, logits_t = _out_projection(
            oa.reshape(n, NA_WIDTH), ob.reshape(n, DA_WIDTH), x2d, on_a[l],
            w_out[l].astype(BF16), ln2_g[l], w_router[l].astype(F32), tm=t["tm_out"])
        idx, gate, posm, boff = _ec_route(logits_t, b, cap=cap, slot_tile=64, tb=COMBINE_TOKENS)
        rows_flat = (idx[..., 0] + (jnp.arange(b, dtype=jnp.int32) * s)[None, :, None]).reshape(-1)
        gate_e = gate.reshape(n_exp, b * cap, 1)
        y = _ec_ffn(rows_flat, x1, ln2_g[l], gate_e, w_gate[l], w_up[l], w_down[l], tf=t["tf"])
        x2d = _ec_combine(boff.reshape(-1), x1, posm, y.reshape(n_exp * b * cap, d), cap=cap,
                          tb=COMBINE_TOKENS, win=COMBINE_WINDOW, adv=COMBINE_ADVANCE)
    return x2d.reshape(b, s, d)
```

```python
import functools
import math

import jax
import jax.numpy as jnp
from jax import lax
from jax.experimental import pallas as pl
from jax.experimental.pallas import tpu as pltpu

F32 = jnp.float32
BF16 = jnp.bfloat16

GRID_W = 64
HEAD_DIM = 128
NA_HEADS = 8
NA_WIDTH = NA_HEADS * HEAD_DIM
NA_WIN_ROWS = 8
NA_WIN_COLS = 16
DA_HEADS = 4
DA_VDIM = 2 * HEAD_DIM
DA_WIDTH = DA_HEADS * DA_VDIM
DA_QK = DA_HEADS * HEAD_DIM
DA_LAYER_LAMBDA_BASE = 0.8
IN_COLS = 3 * NA_WIDTH + 4 * DA_QK + DA_WIDTH
N_EXPERTS = 16
EC_CAPACITY_FACTOR = 2
RMS_EPS = 1e-6
ATTN_SCALE = HEAD_DIM ** -0.5
LOG2_E = math.log2(math.e)
MAX_STATIC_SHIFT = 30.0
QK_BOUND_MARGIN = 1.02

V7X_LANES = 128
V7X_VMEM_LIMIT_BYTES = 56 * 1024 * 1024
MATMUL_ROW_SUBTILE = 256
NA_ROW_GROUP = 16

NT_DIMS = (((1,), (1,)), ((), ()))


def _lambda_init(layer_idx):
    return DA_LAYER_LAMBDA_BASE - 0.6 * math.exp(-0.3 * layer_idx)


def _rms(y):
    return y * lax.rsqrt(jnp.mean(y * y, axis=-1, keepdims=True) + RMS_EPS)


def _qk_score_bound(q_gain, k_gain):
    return (QK_BOUND_MARGIN * math.sqrt(HEAD_DIM) * jnp.max(jnp.abs(q_gain.astype(F32)))
            * jnp.max(jnp.abs(k_gain.astype(F32))))


def _params(*semantics):
    return pltpu.CompilerParams(dimension_semantics=semantics,
                                vmem_limit_bytes=V7X_VMEM_LIMIT_BYTES)


def _inproj_kernel(x_ref, g1_ref, w_ref, gain_ref, o_ref, h_ref, *, tn, sub):
    j = pl.program_id(1)

    @pl.when(j == 0)
    def _():
        h_ref[...] = (_rms(x_ref[...]) * g1_ref[...]).astype(BF16)

    w = w_ref[...].astype(BF16)
    col0 = j * tn
    qk_cols = (col0 < 2 * NA_WIDTH) | ((col0 >= 3 * NA_WIDTH) & (col0 < 3 * NA_WIDTH + 4 * DA_QK))

    tm = h_ref.shape[0]
    for r in range(tm // sub):
        rows = slice(r * sub, (r + 1) * sub)
        acc = jnp.dot(h_ref[rows, :], w, preferred_element_type=F32)
        for c in range(tn // HEAD_DIM):
            sl = slice(c * HEAD_DIM, (c + 1) * HEAD_DIM)
            y = acc[:, sl]
            inv = lax.rsqrt(jnp.mean(y * y, axis=-1, keepdims=True) + RMS_EPS)
            o_ref[rows, sl] = (y * jnp.where(qk_cols, inv, 1.0) * gain_ref[:, sl]).astype(o_ref.dtype)


def _in_projection(x2d, ln1_g, w_in, col_gain, *, tm, tn):
    n, d = x2d.shape
    cols = w_in.shape[1]
    return pl.pallas_call(
        functools.partial(_inproj_kernel, tn=tn, sub=min(MATMUL_ROW_SUBTILE, tm)),
        grid=(n // tm, cols // tn),
        in_specs=[
            pl.BlockSpec((tm, d), lambda i, j: (i, 0)),
            pl.BlockSpec((1, d), lambda i, j: (0, 0)),
            pl.BlockSpec((d, tn), lambda i, j: (0, j)),
            pl.BlockSpec((1, tn), lambda i, j: (0, j)),
        ],
        out_specs=pl.BlockSpec((tm, tn), lambda i, j: (i, j)),
        out_shape=jax.ShapeDtypeStruct((n, cols), BF16),
        scratch_shapes=[pltpu.VMEM((tm, d), BF16)],
        compiler_params=_params("parallel", "arbitrary"),
        name="in_projection",
    )(x2d, ln1_g.reshape(1, d), w_in, col_gain)


def _na_bias_table(rpb, qn_a, kn_a):
    rpb = rpb.astype(F32)
    c = jnp.arange(GRID_W)
    col_off = jnp.clip(c[None, :] - c[:, None], -(NA_WIN_COLS - 1), NA_WIN_COLS - 1) + (NA_WIN_COLS - 1)
    onehot = (col_off[None] == jnp.arange(2 * NA_WIN_COLS - 1)[:, None, None]).astype(F32)
    t = jnp.einsum("hro,oqk->hrqk", rpb, onehot, precision=lax.Precision.HIGHEST)
    cs = jnp.clip(c - NA_WIN_COLS // 2, 0, GRID_W - NA_WIN_COLS)
    col_valid = (c[None, :] >= cs[:, None]) & (c[None, :] < cs[:, None] + NA_WIN_COLS)
    qk_bound = _qk_score_bound(qn_a, kn_a)
    static_shift = 2.0 * qk_bound + (jnp.max(rpb) - jnp.min(rpb)) <= 2.0 * MAX_STATIC_SHIFT
    shift = jnp.where(static_shift, qk_bound + jnp.max(rpb), 0.0)
    t = jnp.where(col_valid, (t - shift) * LOG2_E, -jnp.inf)
    pairs = jnp.concatenate([t[:, :-1], t[:, 1:]], axis=-1)
    return pairs, static_shift.astype(jnp.int32).reshape(1)


def _na_kernel(flag_ref, q_ref, k_ref, v_ref, bias_ref, o_ref, *, rows, group):
    win = NA_WIN_ROWS * GRID_W

    def make_body(static_shift):
        def body(it, carry):
            r = [it * group + g for g in range(group)]
            rs = [jnp.clip(ri - NA_WIN_ROWS // 2, 0, rows - NA_WIN_ROWS) for ri in r]
            q0 = [pl.multiple_of(ri * GRID_W, GRID_W) for ri in r]
            k0 = [pl.multiple_of(rsi * GRID_W, GRID_W) for rsi in rs]
            s = [lax.dot_general(q_ref[pl.ds(q0[g], GRID_W), :], k_ref[pl.ds(k0[g], win), :], NT_DIMS,
                                 preferred_element_type=F32) for g in range(group)]
            p, l = [], []
            for g in range(group):
                base = rs[g] - r[g] + (NA_WIN_ROWS - 1)
                bias = jnp.concatenate([bias_ref[base + 2 * jj] for jj in range(NA_WIN_ROWS // 2)], axis=1)
                z = s[g] * (ATTN_SCALE * LOG2_E) + bias
                if not static_shift:
                    z = z - jnp.max(z, axis=-1, keepdims=True)
                pg = jnp.exp2(z)
                l.append(jnp.sum(pg, axis=-1, keepdims=True))
                p.append(pg.astype(BF16))
            for g in range(group):
                o = jnp.dot(p[g], v_ref[pl.ds(k0[g], win), :], preferred_element_type=F32)
                o_ref[pl.ds(q0[g], GRID_W), :] = (o / l[g]).astype(o_ref.dtype)
            return carry
        return body

    @pl.when(flag_ref[0] == 1)
    def _():
        lax.fori_loop(0, rows // group, make_body(True), 0)

    @pl.when(flag_ref[0] != 1)
    def _():
        lax.fori_loop(0, rows // group, make_body(False), 0)


def _na_attention(proj3d, bias_pairs, static_shift):
    b, s, _ = proj3d.shape
    rows = s // GRID_W
    assert rows >= NA_WIN_ROWS
    qkv_spec = lambda off: pl.BlockSpec((None, s, HEAD_DIM), lambda bi, h, fl: (bi, 0, off + h))
    grid_spec = pltpu.PrefetchScalarGridSpec(
        num_scalar_prefetch=1,
        grid=(b, NA_HEADS),
        in_specs=[
            qkv_spec(0), qkv_spec(NA_HEADS), qkv_spec(2 * NA_HEADS),
            pl.BlockSpec((None,) + bias_pairs.shape[1:], lambda bi, h, fl: (h, 0, 0, 0)),
        ],
        out_specs=pl.BlockSpec((None, s, HEAD_DIM), lambda bi, h, fl: (bi, 0, h)),
    )
    return pl.pallas_call(
        functools.partial(_na_kernel, rows=rows, group=math.gcd(rows, NA_ROW_GROUP)),
        grid_spec=grid_spec,
        out_shape=jax.ShapeDtypeStruct((b, s, NA_WIDTH), BF16),
        compiler_params=_params("parallel", "parallel"),
        name="na_attention",
    )(static_shift, proj3d, proj3d, proj3d, bias_pairs)


def _da_kernel(par_ref, q1_ref, q2_ref, k1_ref, k2_ref, v_ref,
               lq1_ref, lk1_ref, lq2_ref, lk2_ref, sub_ref, o_ref, alibi_ref, z_ref, p_ref,
               *, lam_init, kc):
    h = pl.program_id(0)
    qi = pl.program_id(2)
    tq = q1_ref.shape[0]
    s_len = k1_ref.shape[0]
    lam = (jnp.exp(jnp.sum(lq1_ref[...] * lk1_ref[...], axis=-1, keepdims=True))
           - jnp.exp(jnp.sum(lq2_ref[...] * lk2_ref[...], axis=-1, keepdims=True))
           + lam_init)

    bound = par_ref[DA_HEADS]
    static_shift = bound <= MAX_STATIC_SHIFT

    @pl.when((pl.program_id(1) == 0) & (qi == 0))
    def _():
        i = lax.broadcasted_iota(jnp.int32, (tq, 1), 0).astype(F32)
        x = (lax.broadcasted_iota(jnp.int32, (1, alibi_ref.shape[1]), 1) - (s_len - tq)).astype(F32)
        alibi_ref[...] = (jnp.abs(x - i) * (-LOG2_E * par_ref[h])
                          - LOG2_E * jnp.where(static_shift, bound, 0.0))

    win0 = s_len - tq - qi * tq
    n_chunks = s_len // kc
    heads = ((q1_ref, k1_ref), (q2_ref, k2_ref))

    def scores(q, k_ref, c):
        s = lax.dot_general(q, k_ref[c * kc:(c + 1) * kc, :], NT_DIMS, preferred_element_type=F32)
        return s * (ATTN_SCALE * LOG2_E) + alibi_ref[:, pl.ds(pl.multiple_of(win0 + c * kc, V7X_LANES), kc)]

    def finish(o1, l1, o2, l2):
        o = o1 * (1.0 / l1) - o2 * (lam / l2)
        o_ref[...] = (_rms(o) * sub_ref[...] * (1.0 - lam_init)).astype(o_ref.dtype)

    @pl.when(static_shift)
    def _():
        outs = []
        for t, (q_ref, k_ref) in enumerate(heads):
            q = q_ref[...]
            l_lanes = jnp.zeros((tq, V7X_LANES), F32)
            for c in range(n_chunks):
                p = jnp.exp2(scores(q, k_ref, c))
                for j in range(kc // V7X_LANES):
                    l_lanes = l_lanes + p[:, j * V7X_LANES:(j + 1) * V7X_LANES]
                p_ref[t, :, c * kc:(c + 1) * kc] = p.astype(BF16)
            o = jnp.dot(p_ref[t], v_ref[...], preferred_element_type=F32)
            outs += [o, jnp.sum(l_lanes, axis=-1, keepdims=True)]
        finish(*outs)

    @pl.when(jnp.logical_not(static_shift))
    def _():
        outs = []
        for t, (q_ref, k_ref) in enumerate(heads):
            q = q_ref[...]
            m = jnp.full((tq, 1), -jnp.inf, F32)
            for c in range(n_chunks):
                z = scores(q, k_ref, c)
                z_ref[:, c * kc:(c + 1) * kc] = z
                m = jnp.maximum(m, jnp.max(z, axis=-1, keepdims=True))
            l = jnp.zeros((tq, 1), F32)
            for c in range(n_chunks):
                p = jnp.exp2(z_ref[:, c * kc:(c + 1) * kc] - m)
                l = l + jnp.sum(p, axis=-1, keepdims=True)
                p_ref[t, :, c * kc:(c + 1) * kc] = p.astype(BF16)
            outs += [jnp.dot(p_ref[t], v_ref[...], preferred_element_type=F32), l]
        finish(*outs)


def _da_attention(proj3d, slopes, qn_b, kn_b, lam_q1, lam_k1, lam_q2, lam_k2, subln, *, lam_init, tq):
    b, s, _ = proj3d.shape
    qk0 = 3 * NA_WIDTH // HEAD_DIM
    v0 = (3 * NA_WIDTH + 4 * DA_QK) // DA_VDIM
    q_spec = lambda off: pl.BlockSpec((None, tq, HEAD_DIM), lambda h, bi, qi, sl: (bi, qi, off + h))
    k_spec = lambda off: pl.BlockSpec((None, s, HEAD_DIM), lambda h, bi, qi, sl: (bi, 0, off + h))
    vec_spec = lambda n: pl.BlockSpec((1, n), lambda h, bi, qi, sl: (0, 0))
    kc = min(512, s)
    par = jnp.concatenate([slopes, _qk_score_bound(qn_b, kn_b).reshape(1)])
    grid_spec = pltpu.PrefetchScalarGridSpec(
        num_scalar_prefetch=1,
        grid=(DA_HEADS, b, s // tq),
        in_specs=[
            q_spec(qk0), q_spec(qk0 + DA_HEADS), k_spec(qk0 + 2 * DA_HEADS), k_spec(qk0 + 3 * DA_HEADS),
            pl.BlockSpec((None, s, DA_VDIM), lambda h, bi, qi, sl: (bi, 0, v0 + h)),
            vec_spec(HEAD_DIM), vec_spec(HEAD_DIM), vec_spec(HEAD_DIM), vec_spec(HEAD_DIM),
            vec_spec(DA_VDIM),
        ],
        out_specs=pl.BlockSpec((None, tq, DA_VDIM), lambda h, bi, qi, sl: (bi, qi, h)),
        scratch_shapes=[pltpu.VMEM((tq, 2 * s - tq), F32), pltpu.VMEM((tq, s), F32),
                        pltpu.VMEM((2, tq, s), BF16)],
    )
    row = lambda a: a.reshape(1, -1).astype(F32)
    return pl.pallas_call(
        functools.partial(_da_kernel, lam_init=lam_init, kc=kc),
        grid_spec=grid_spec,
        out_shape=jax.ShapeDtypeStruct((b, s, DA_WIDTH), BF16),
        compiler_params=_params("arbitrary", "arbitrary", "arbitrary"),
        name="da_attention",
    )(par, proj3d, proj3d, proj3d, proj3d, proj3d,
      row(lam_q1), row(lam_k1), row(lam_q2), row(lam_k2), row(subln))


def _outproj_kernel(oa_ref, ob_ref, x_ref, ona_ref, wa_ref, wb_ref, g2_ref, wr_ref,
                    x1_ref, lg_ref, *, sub):
    n_exp = lg_ref.shape[0]
    tm = x_ref.shape[0]
    for r in range(tm // sub):
        rows = slice(r * sub, (r + 1) * sub)
        oan = (_rms(oa_ref[rows, :].astype(F32)) * ona_ref[...]).astype(BF16)
        acc = jnp.dot(oan, wa_ref[...], preferred_element_type=F32)
        acc = acc + jnp.dot(ob_ref[rows, :], wb_ref[...], preferred_element_type=F32)
        x1 = x_ref[rows, :] + acc
        x1_ref[rows, :] = x1
        h2 = _rms(x1) * g2_ref[...]
        h_hi = h2.astype(BF16)
        h_lo = (h2 - h_hi.astype(F32)).astype(BF16)
        parts = (jnp.dot(h_hi, wr_ref[...], preferred_element_type=F32)
                 + jnp.dot(h_lo, wr_ref[...], preferred_element_type=F32))
        parts_t = parts.T
        lg_ref[:, rows] = parts_t[:n_exp] + parts_t[n_exp:2 * n_exp]


def _out_projection(oa2d, ob2d, x2d, on_a, w_out, ln2_g, w_router, *, tm):
    n, d = x2d.shape
    n_exp = w_router.shape[1]
    assert 2 * n_exp <= V7X_LANES
    wr_hi = w_router.astype(BF16)
    wr_lo = (w_router - wr_hi.astype(F32)).astype(BF16)
    wr_pieces = jnp.concatenate(
        [wr_hi, wr_lo, jnp.zeros((d, V7X_LANES - 2 * n_exp), BF16)], axis=1)
    assert DA_WIDTH == NA_WIDTH
    half_spec = lambda half: pl.BlockSpec((NA_WIDTH, d), lambda i: (half, 0))
    row_spec = lambda width: pl.BlockSpec((tm, width), lambda i: (i, 0))
    full_spec = lambda a: pl.BlockSpec(a.shape, lambda i: (0, 0))
    on_a = on_a.reshape(1, -1)
    ln2_g = ln2_g.reshape(1, -1)
    return pl.pallas_call(
        functools.partial(_outproj_kernel, sub=min(MATMUL_ROW_SUBTILE, tm)),
        grid=(n // tm,),
        in_specs=[row_spec(NA_WIDTH), row_spec(DA_WIDTH), row_spec(d), full_spec(on_a),
                  half_spec(0), half_spec(1), full_spec(ln2_g), full_spec(wr_pieces)],
        out_specs=[row_spec(d), pl.BlockSpec((n_exp, tm), lambda i: (0, i))],
        out_shape=[jax.ShapeDtypeStruct((n, d), F32), jax.ShapeDtypeStruct((n_exp, n), F32)],
        compiler_params=_params("parallel"),
        name="out_projection",
    )(oa2d, ob2d, x2d, on_a, w_out, w_out, ln2_g, wr_pieces)


def _exclusive_prefix(mask):
    rows, s_len = mask.shape
    m = jnp.where(mask, 1.0, 0.0).astype(BF16)
    r = lax.broadcasted_iota(jnp.int32, (V7X_LANES, V7X_LANES), 0)
    c = lax.broadcasted_iota(jnp.int32, (V7X_LANES, V7X_LANES), 1)
    before = jnp.where(r < c, 1.0, 0.0).astype(BF16)
    ones = jnp.ones((V7X_LANES, V7X_LANES), BF16)
    run = jnp.zeros((rows, V7X_LANES), F32)
    out = []
    for blk in range(s_len // V7X_LANES):
        mb = m[:, blk * V7X_LANES:(blk + 1) * V7X_LANES]
        out.append(run + jnp.dot(mb, before, preferred_element_type=F32))
        run = run + jnp.dot(mb, ones, preferred_element_type=F32)
    return jnp.concatenate(out, axis=1)


def _route_kernel(lg_ref, idx_ref, gate_ref, posm_ref, boff_ref, aff_ref, idx_acc, gate_acc,
                  boff_vmem, boff_smem, sem_ref, *, cap, slot_tile, tb):
    lg = lg_ref[...]
    n_exp, s_len = lg.shape
    ex = jnp.exp(lg - jnp.max(lg, axis=0, keepdims=True))
    aff_ref[...] = ex / jnp.sum(ex, axis=0, keepdims=True)
    aff = aff_ref[...]

    def enough(values, t):
        return jnp.sum(jnp.where(values >= t, 1.0, 0.0), axis=1, keepdims=True) >= cap

    thr = jnp.full((n_exp, 1), 2.0 ** -126, F32)
    for shift in (64, 32, 16, 8, 4, 2, 1):
        cand = thr * (2.0 ** shift)
        thr = jnp.where(enough(aff, cand), cand, thr)
    step = thr
    for _ in range(23):
        step = step * 0.5
        cand = thr + step
        thr = jnp.where(enough(aff, cand), cand, thr)
    thr = jnp.where(enough(aff, thr), thr, 0.0)
    res = aff - thr
    rho = jnp.zeros((n_exp, 1), F32)
    for _ in range(24):
        step = step * 0.5
        cand = rho + step
        rho = jnp.where(enough(res, cand), cand, rho)

    gt = res > rho
    eq = res == rho
    need = cap - jnp.sum(jnp.where(gt, 1.0, 0.0), axis=1, keepdims=True)
    sel = gt | (eq & (_exclusive_prefix(eq) < need))
    posm_ref[...] = jnp.where(sel, _exclusive_prefix(sel), -1.0)

    t_idx = lax.broadcasted_iota(jnp.int32, (s_len, V7X_LANES), 0)
    j_idx = lax.broadcasted_iota(jnp.int32, (s_len, V7X_LANES), 1)
    before_block = jnp.where(t_idx < j_idx * tb, 1.0, 0.0).astype(BF16)
    boff = jnp.dot(jnp.where(sel, 1.0, 0.0).astype(BF16), before_block, preferred_element_type=F32)
    boff_ref[...] = boff.astype(jnp.int32)
    boff_vmem[...] = boff.astype(jnp.int32)
    to_smem = pltpu.make_async_copy(boff_vmem, boff_smem, sem_ref.at[0])
    to_smem.start()
    to_smem.wait()

    n_blk = s_len // tb
    n_tiles = cap // slot_tile
    span = min(s_len, 4 * tb)
    lane_iota = lax.broadcasted_iota(jnp.int32, (1, span), 1)

    def per_expert(e, carry):
        offs = [boff_smem[e, j] for j in range(n_blk + 1)]
        first_tok, n_rounds = [], 0
        for st in range(n_tiles):
            s0 = st * slot_tile
            j_lo = sum((offs[j + 1] <= s0).astype(jnp.int32) for j in range(n_blk))
            j_hi = sum((offs[j] < s0 + slot_tile).astype(jnp.int32) for j in range(n_blk))
            first_tok.append(j_lo * tb)
            n_rounds = jnp.maximum(n_rounds, ((j_hi - j_lo) * tb + span - 1) // span)
        idx_acc[...] = jnp.zeros_like(idx_acc)
        gate_acc[...] = jnp.zeros_like(gate_acc)

        def one_round(r, c2):
            for st in range(n_tiles):
                nominal = first_tok[st] + r * span
                start = pl.multiple_of(jnp.minimum(nominal, s_len - span), V7X_LANES)
                tok_w = (start + lane_iota).astype(F32)
                pos_w = posm_ref[pl.ds(e, 1), pl.ds(start, span)]
                pos_w = jnp.where(tok_w >= nominal.astype(F32), pos_w, -1.0)
                aff_w = aff_ref[pl.ds(e, 1), pl.ds(start, span)]
                slots = (st * slot_tile + lax.broadcasted_iota(jnp.int32, (slot_tile, 1), 0)).astype(F32)
                hit = pos_w == slots
                rows = pl.ds(st * slot_tile, slot_tile)
                idx_acc[rows, :] += jnp.sum(jnp.where(hit, tok_w, 0.0), axis=1, keepdims=True)
                gate_acc[rows, :] += jnp.sum(jnp.where(hit, aff_w, 0.0), axis=1, keepdims=True)
            return c2

        lax.fori_loop(0, n_rounds, one_round, 0)
        idx_ref[e] = idx_acc[...].astype(jnp.int32)
        gate_ref[e] = gate_acc[...]
        return carry

    lax.fori_loop(0, n_exp, per_expert, 0)


def _ec_route(logits_t, batch, *, cap, slot_tile, tb):
    n_exp, n = logits_t.shape
    s = n // batch
    assert s // tb + 1 <= V7X_LANES
    slot_spec = pl.BlockSpec((n_exp, None, cap, 1), lambda b: (0, b, 0, 0))
    return pl.pallas_call(
        functools.partial(_route_kernel, cap=cap, slot_tile=slot_tile, tb=tb),
        grid=(batch,),
        in_specs=[pl.BlockSpec((n_exp, s), lambda b: (0, b))],
        out_specs=[slot_spec, slot_spec,
                   pl.BlockSpec((None, n_exp, s), lambda b: (b, 0, 0)),
                   pl.BlockSpec((None, n_exp, V7X_LANES), lambda b: (b, 0, 0))],
        out_shape=[jax.ShapeDtypeStruct((n_exp, batch, cap, 1), jnp.int32),
                   jax.ShapeDtypeStruct((n_exp, batch, cap, 1), F32),
                   jax.ShapeDtypeStruct((batch, n_exp, s), F32),
                   jax.ShapeDtypeStruct((batch, n_exp, V7X_LANES), jnp.int32)],
        scratch_shapes=[pltpu.VMEM((n_exp, s), F32), pltpu.VMEM((cap, 1), F32), pltpu.VMEM((cap, 1), F32),
                        pltpu.VMEM((n_exp, V7X_LANES), jnp.int32),
                        pltpu.SMEM((n_exp, V7X_LANES), jnp.int32), pltpu.SemaphoreType.DMA((1,))],
        compiler_params=_params("parallel"),
        name="ec_route",
    )(logits_t)


def _ffn_kernel(rows_ref, x1_hbm, g2_ref, gate_ref, wg_ref, wu_ref, wd_ref, y_ref,
                xs_ref, xsb_ref, acc_ref, sem_ref, *, m_rows, chunk, n_total):
    e = pl.program_id(0)
    f = pl.program_id(1)
    n_exp = pl.num_programs(0)
    n_f = pl.num_programs(1)

    def row_copy(expert, i):
        src = rows_ref[expert * m_rows + jnp.minimum(i, m_rows - 1)]
        return pltpu.make_async_copy(x1_hbm.at[pl.ds(src, 1), :], xs_ref.at[pl.ds(i, 1), :],
                                     sem_ref.at[0])

    def wait_rows(count):
        pltpu.make_async_copy(x1_hbm.at[pl.ds(0, count), :], xs_ref.at[pl.ds(0, count), :],
                              sem_ref.at[0]).wait()

    def normalise_rows():
        xsb_ref[...] = (_rms(xs_ref[pl.ds(0, m_rows), :]) * g2_ref[...]).astype(BF16)

    @pl.when((e == 0) & (f == 0))
    def _():
        def group(ig, carry):
            for u in range(8):
                row_copy(0, ig * 8 + u).start()
            return carry
        lax.fori_loop(0, m_rows // 8, group, 0)
        wait_rows(m_rows)
        normalise_rows()

    @pl.when(f == 0)
    def _():
        acc_ref[...] = jnp.zeros_like(acc_ref)

    nxt = jnp.minimum(e + 1, n_exp - 1)
    for u in range(chunk):
        row_copy(nxt, f * chunk + u).start()

    xs = xsb_ref[...]
    g = jnp.dot(xs, wg_ref[...].astype(BF16), preferred_element_type=F32)
    u = jnp.dot(xs, wu_ref[...].astype(BF16), preferred_element_type=F32)
    a = (g * jax.nn.sigmoid(g) * u).astype(BF16)
    acc_ref[...] += jnp.dot(a, wd_ref[...].astype(BF16), preferred_element_type=F32)

    @pl.when(f == n_f - 1)
    def _():
        y_ref[...] = (acc_ref[...] * gate_ref[...]).astype(y_ref.dtype)
        wait_rows(n_total)
        normalise_rows()


def _ec_ffn(rows_flat, x1, ln2_g, gate, w_gate, w_up, w_down, *, tf):
    n, d = x1.shape
    n_exp, _, ff = w_gate.shape
    m_rows = rows_flat.shape[0] // n_exp
    n_f = ff // tf
    assert m_rows % 8 == 0
    chunk = -(-m_rows // (8 * n_f)) * 8
    n_total = chunk * n_f
    assert n_total - m_rows < m_rows
    grid_spec = pltpu.PrefetchScalarGridSpec(
        num_scalar_prefetch=1,
        grid=(n_exp, n_f),
        in_specs=[
            pl.BlockSpec(memory_space=pl.ANY),
            pl.BlockSpec((1, d), lambda e, f, r: (0, 0)),
            pl.BlockSpec((None, m_rows, 1), lambda e, f, r: (e, 0, 0)),
            pl.BlockSpec((None, d, tf), lambda e, f, r: (e, 0, f)),
            pl.BlockSpec((None, d, tf), lambda e, f, r: (e, 0, f)),
            pl.BlockSpec((None, tf, d), lambda e, f, r: (e, f, 0)),
        ],
        out_specs=pl.BlockSpec((None, m_rows, d), lambda e, f, r: (e, 0, 0)),
        scratch_shapes=[pltpu.VMEM((n_total, d), F32), pltpu.VMEM((m_rows, d), BF16),
                        pltpu.VMEM((m_rows, d), F32), pltpu.SemaphoreType.DMA((1,))],
    )
    return pl.pallas_call(
        functools.partial(_ffn_kernel, m_rows=m_rows, chunk=chunk, n_total=n_total),
        grid_spec=grid_spec,
        out_shape=jax.ShapeDtypeStruct((n_exp, m_rows, d), BF16),
        compiler_params=_params("arbitrary", "arbitrary"),
        name="ec_ffn",
    )(rows_flat, x1, ln2_g.reshape(1, d), gate, w_gate, w_up, w_down)


def _combine_kernel(boff_ref, x1_ref, posm_ref, y_hbm, o_ref, ywin_ref, sem_ref,
                    *, batch, cap, win, adv):
    n_exp, tb = posm_ref.shape
    i = pl.program_id(0)
    n_steps = pl.num_programs(0)
    nblk = n_steps // batch
    buf = i % 2

    def block_offsets(step, shift):
        bb = step // nblk
        return bb, [boff_ref[(bb * n_exp + e) * V7X_LANES + step % nblk + shift] for e in range(n_exp)]

    def window_copies(bb, lowers, to_buf):
        copies, starts = [], []
        for e in range(n_exp):
            start = jnp.minimum((lowers[e] // 16) * 16, cap - win)
            row0 = pl.multiple_of((e * batch + bb) * cap + start, 16)
            copies.append(pltpu.make_async_copy(y_hbm.at[pl.ds(row0, win), :],
                                                ywin_ref.at[to_buf, pl.ds(e * win, win), :],
                                                sem_ref.at[to_buf]))
            starts.append(start)
        return copies, starts

    b, offs = block_offsets(i, 0)
    _, nxts = block_offsets(i, 1)
    most = functools.reduce(jnp.maximum, [nx - of for nx, of in zip(nxts, offs)])

    @pl.when(i == 0)
    def _():
        for cp in window_copies(b, offs, buf)[0]:
            cp.start()

    @pl.when(i + 1 < n_steps)
    def _():
        b_next, offs_next = block_offsets(i + 1, 0)
        for cp in window_copies(b_next, offs_next, 1 - buf)[0]:
            cp.start()

    o_ref[...] = x1_ref[...]
    posm = posm_ref[...]
    prow = jnp.concatenate([jnp.broadcast_to(posm[e:e + 1, :], (win, tb)) for e in range(n_exp)], axis=0)

    def scatter_round(lowers, fetch):
        copies, starts = window_copies(b, lowers, buf)
        if fetch:
            for cp in copies:
                cp.start()
        cols = []
        for e in range(n_exp):
            slot = starts[e] + lax.broadcasted_iota(jnp.int32, (win, 1), 0)
            cols.append(jnp.where((slot >= lowers[e]) & (slot < lowers[e] + adv), slot, -2).astype(F32))
        hit_t = jnp.where(prow == jnp.concatenate(cols, axis=0), 1.0, 0.0).astype(BF16)
        for cp in copies:
            cp.wait()
        o_ref[...] += lax.dot_general(hit_t, ywin_ref[buf], (((0,), (0,)), ((), ())),
                                      preferred_element_type=F32)

    scatter_round(offs, fetch=False)

    def extra_round(k, carry):
        scatter_round([of + k * adv for of in offs], fetch=True)
        return carry

    lax.fori_loop(1, (most + adv - 1) // adv, extra_round, 0)


def _ec_combine(boff_flat, x1, posm, y2d, *, cap, tb, win, adv):
    n, d = x1.shape
    batch, n_exp, s = posm.shape
    nblk = s // tb
    assert cap % 16 == 0 and win % 16 == 0 and win >= adv + 16 and cap >= win
    grid_spec = pltpu.PrefetchScalarGridSpec(
        num_scalar_prefetch=1,
        grid=(n // tb,),
        in_specs=[
            pl.BlockSpec((tb, d), lambda i, bo: (i, 0)),
            pl.BlockSpec((None, n_exp, tb), lambda i, bo: (i // nblk, 0, i % nblk)),
            pl.BlockSpec(memory_space=pl.ANY),
        ],
        out_specs=pl.BlockSpec((tb, d), lambda i, bo: (i, 0)),
        scratch_shapes=[pltpu.VMEM((2, n_exp * win, d), BF16), pltpu.SemaphoreType.DMA((2,))],
    )
    return pl.pallas_call(
        functools.partial(_combine_kernel, batch=batch, cap=cap, win=win, adv=adv),
        grid_spec=grid_spec,
        out_shape=jax.ShapeDtypeStruct((n, d), F32),
        compiler_params=_params("arbitrary"),
        name="ec_combine",
    )(boff_flat, x1, posm, y2d)


COMBINE_TOKENS = 256
COMBINE_WINDOW = 80
COMBINE_ADVANCE = 64
ROUTE_SLOT_TILE = 64


def _tiles(n, d, ff):
    pick = lambda total, want: want if total % want == 0 else total
    return dict(tm_in=pick(n, 1024), tn_in=1024, tq=pick(n, 512), tm_out=pick(n, 512),
                tf=pick(ff, 256))


def kernel(x, ln1_g, w_in, qn_a, kn_a, rpb_a, on_a, qn_b, kn_b, lam_q1, lam_k1, lam_q2, lam_k2,
           subln_b, w_out, ln2_g, w_router, w_gate, w_up, w_down):
    b, s, d = x.shape
    n = b * s
    depth = w_in.shape[0]
    n_exp = w_router.shape[-1]
    cap = EC_CAPACITY_FACTOR * s // n_exp
    t = _tiles(n, d, w_gate.shape[-1])
    slopes = 2.0 ** (-8.0 * jnp.arange(1, DA_HEADS + 1, dtype=F32) / DA_HEADS)
    ones_v = jnp.ones((NA_WIDTH,), F32)
    x2d = x.reshape(n, d)
    for l in range(depth):
        tile_heads = lambda g, heads: jnp.tile(g.astype(F32), heads)
        col_gain = jnp.concatenate([
            tile_heads(qn_a[l], NA_HEADS), tile_heads(kn_a[l], NA_HEADS), ones_v,
            tile_heads(qn_b[l], 2 * DA_HEADS), tile_heads(kn_b[l], 2 * DA_HEADS),
            jnp.ones((DA_WIDTH,), F32)]).reshape(1, IN_COLS)
        proj = _in_projection(x2d, ln1_g[l], w_in[l], col_gain,
                              tm=t["tm_in"], tn=t["tn_in"])
        proj3d = proj.reshape(b, s, IN_COLS)
        oa = _na_attention(proj3d, *_na_bias_table(rpb_a[l], qn_a[l], kn_a[l]))
        ob = _da_attention(proj3d, slopes, qn_b[l], kn_b[l], lam_q1[l], lam_k1[l], lam_q2[l], lam_k2[l], subln_b[l],
                           lam_init=_lambda_init(l), tq=min(t["tq"], s))
        x1, logits_t = _out_projection(
            oa.reshape(n, NA_WIDTH), ob.reshape(n, DA_WIDTH), x2d, on_a[l],
            w_out[l].astype(BF16), ln2_g[l], w_router[l].astype(F32), tm=t["tm_out"])
        idx, gate, posm, boff = _ec_route(logits_t, b, cap=cap, slot_tile=ROUTE_SLOT_TILE, tb=COMBINE_TOKENS)
        rows_flat = (idx[..., 0] + (jnp.arange(b, dtype=jnp.int32) * s)[None, :, None]).reshape(-1)
        gate_e = gate.reshape(n_exp, b * cap, 1)
        y = _ec_ffn(rows_flat, x1, ln2_g[l], gate_e, w_gate[l], w_up[l], w_down[l], tf=t["tf"])
        x2d = _ec_combine(boff.reshape(-1), x1, posm, y.reshape(n_exp * b * cap, d), cap=cap,
                          tb=COMBINE_TOKENS, win=COMBINE_WINDOW, adv=COMBINE_ADVANCE)
    return x2d.reshape(b, s, d)
```

```python
import functools
import math

import jax
import jax.numpy as jnp
from jax import lax
from jax.experimental import pallas as pl
from jax.experimental.pallas import tpu as pltpu

F32 = jnp.float32
BF16 = jnp.bfloat16

GRID_W = 64
HEAD_DIM = 128
NA_HEADS = 8
NA_WIDTH = NA_HEADS * HEAD_DIM
NA_WIN_ROWS = 8
NA_WIN_COLS = 16
DA_HEADS = 4
DA_VDIM = 2 * HEAD_DIM
DA_WIDTH = DA_HEADS * DA_VDIM
DA_QK = DA_HEADS * HEAD_DIM
DA_LAYER_LAMBDA_BASE = 0.8
IN_COLS = 3 * NA_WIDTH + 4 * DA_QK + DA_WIDTH
N_EXPERTS = 16
EC_CAPACITY_FACTOR = 2
RMS_EPS = 1e-6
ATTN_SCALE = HEAD_DIM ** -0.5
LOG2_E = math.log2(math.e)
MAX_STATIC_SHIFT = 30.0
QK_BOUND_MARGIN = 1.02

V7X_LANES = 128
V7X_VMEM_LIMIT_BYTES = 56 * 1024 * 1024
MATMUL_ROW_SUBTILE = 256
NA_ROW_GROUP = 16
ROUTE_WINDOW_TOKENS = 1024

NT_DIMS = (((1,), (1,)), ((), ()))


def _lambda_init(layer_idx):
    return DA_LAYER_LAMBDA_BASE - 0.6 * math.exp(-0.3 * layer_idx)


def _rms(y):
    return y * lax.rsqrt(jnp.mean(y * y, axis=-1, keepdims=True) + RMS_EPS)


def _qk_score_bound(q_gain, k_gain):
    return (QK_BOUND_MARGIN * math.sqrt(HEAD_DIM) * jnp.max(jnp.abs(q_gain.astype(F32)))
            * jnp.max(jnp.abs(k_gain.astype(F32))))


def _params(*semantics):
    return pltpu.CompilerParams(dimension_semantics=semantics,
                                vmem_limit_bytes=V7X_VMEM_LIMIT_BYTES)


def _inproj_kernel(x_ref, g1_ref, w_ref, gain_ref, o_ref, h_ref, *, tn, sub):
    j = pl.program_id(1)

    @pl.when(j == 0)
    def _():
        h_ref[...] = (_rms(x_ref[...]) * g1_ref[...]).astype(BF16)

    w = w_ref[...].astype(BF16)
    col0 = j * tn
    qk_cols = (col0 < 2 * NA_WIDTH) | ((col0 >= 3 * NA_WIDTH) & (col0 < 3 * NA_WIDTH + 4 * DA_QK))

    tm = h_ref.shape[0]
    for r in range(tm // sub):
        rows = slice(r * sub, (r + 1) * sub)
        acc = jnp.dot(h_ref[rows, :], w, preferred_element_type=F32)
        for c in range(tn // HEAD_DIM):
            sl = slice(c * HEAD_DIM, (c + 1) * HEAD_DIM)
            y = acc[:, sl]
            inv = lax.rsqrt(jnp.mean(y * y, axis=-1, keepdims=True) + RMS_EPS)
            o_ref[rows, sl] = (y * jnp.where(qk_cols, inv, 1.0) * gain_ref[:, sl]).astype(o_ref.dtype)


def _in_projection(x2d, ln1_g, w_in, col_gain, *, tm, tn):
    n, d = x2d.shape
    cols = w_in.shape[1]
    return pl.pallas_call(
        functools.partial(_inproj_kernel, tn=tn, sub=min(MATMUL_ROW_SUBTILE, tm)),
        grid=(n // tm, cols // tn),
        in_specs=[
            pl.BlockSpec((tm, d), lambda i, j: (i, 0)),
            pl.BlockSpec((1, d), lambda i, j: (0, 0)),
            pl.BlockSpec((d, tn), lambda i, j: (0, j)),
            pl.BlockSpec((1, tn), lambda i, j: (0, j)),
        ],
        out_specs=pl.BlockSpec((tm, tn), lambda i, j: (i, j)),
        out_shape=jax.ShapeDtypeStruct((n, cols), BF16),
        scratch_shapes=[pltpu.VMEM((tm, d), BF16)],
        compiler_params=_params("parallel", "arbitrary"),
        name="in_projection",
    )(x2d, ln1_g.reshape(1, d), w_in, col_gain)


def _na_bias_table(rpb, qn_a, kn_a):
    rpb = rpb.astype(F32)
    c = jnp.arange(GRID_W)
    col_off = jnp.clip(c[None, :] - c[:, None], -(NA_WIN_COLS - 1), NA_WIN_COLS - 1) + (NA_WIN_COLS - 1)
    onehot = (col_off[None] == jnp.arange(2 * NA_WIN_COLS - 1)[:, None, None]).astype(F32)
    t = jnp.einsum("hro,oqk->hrqk", rpb, onehot, precision=lax.Precision.HIGHEST)
    cs = jnp.clip(c - NA_WIN_COLS // 2, 0, GRID_W - NA_WIN_COLS)
    col_valid = (c[None, :] >= cs[:, None]) & (c[None, :] < cs[:, None] + NA_WIN_COLS)
    qk_bound = _qk_score_bound(qn_a, kn_a)
    static_shift = 2.0 * qk_bound + (jnp.max(rpb) - jnp.min(rpb)) <= 2.0 * MAX_STATIC_SHIFT
    shift = jnp.where(static_shift, qk_bound + jnp.max(rpb), 0.0)
    t = jnp.where(col_valid, (t - shift) * LOG2_E, -jnp.inf)
    pairs = jnp.concatenate([t[:, :-1], t[:, 1:]], axis=-1)
    return pairs, static_shift.astype(jnp.int32).reshape(1)


def _na_kernel(flag_ref, q_ref, k_ref, v_ref, bias_ref, o_ref, *, rows, group):
    win = NA_WIN_ROWS * GRID_W

    def make_body(static_shift):
        def body(it, carry):
            r = [it * group + g for g in range(group)]
            rs = [jnp.clip(ri - NA_WIN_ROWS // 2, 0, rows - NA_WIN_ROWS) for ri in r]
            q0 = [pl.multiple_of(ri * GRID_W, GRID_W) for ri in r]
            k0 = [pl.multiple_of(rsi * GRID_W, GRID_W) for rsi in rs]
            s = [lax.dot_general(q_ref[pl.ds(q0[g], GRID_W), :], k_ref[pl.ds(k0[g], win), :], NT_DIMS,
                                 preferred_element_type=F32) for g in range(group)]
            p, l = [], []
            for g in range(group):
                base = rs[g] - r[g] + (NA_WIN_ROWS - 1)
                bias = jnp.concatenate([bias_ref[base + 2 * jj] for jj in range(NA_WIN_ROWS // 2)], axis=1)
                z = s[g] * (ATTN_SCALE * LOG2_E) + bias
                if not static_shift:
                    z = z - jnp.max(z, axis=-1, keepdims=True)
                pg = jnp.exp2(z)
                l.append(jnp.sum(pg, axis=-1, keepdims=True))
                p.append(pg.astype(BF16))
            for g in range(group):
                o = jnp.dot(p[g], v_ref[pl.ds(k0[g], win), :], preferred_element_type=F32)
                o_ref[pl.ds(q0[g], GRID_W), :] = (o / l[g]).astype(o_ref.dtype)
            return carry
        return body

    @pl.when(flag_ref[0] == 1)
    def _():
        lax.fori_loop(0, rows // group, make_body(True), 0)

    @pl.when(flag_ref[0] != 1)
    def _():
        lax.fori_loop(0, rows // group, make_body(False), 0)


def _na_attention(proj3d, bias_pairs, static_shift):
    b, s, _ = proj3d.shape
    rows = s // GRID_W
    assert rows >= NA_WIN_ROWS
    qkv_spec = lambda off: pl.BlockSpec((None, s, HEAD_DIM), lambda bi, h, fl: (bi, 0, off + h))
    grid_spec = pltpu.PrefetchScalarGridSpec(
        num_scalar_prefetch=1,
        grid=(b, NA_HEADS),
        in_specs=[
            qkv_spec(0), qkv_spec(NA_HEADS), qkv_spec(2 * NA_HEADS),
            pl.BlockSpec((None,) + bias_pairs.shape[1:], lambda bi, h, fl: (h, 0, 0, 0)),
        ],
        out_specs=pl.BlockSpec((None, s, HEAD_DIM), lambda bi, h, fl: (bi, 0, h)),
    )
    return pl.pallas_call(
        functools.partial(_na_kernel, rows=rows, group=math.gcd(rows, NA_ROW_GROUP)),
        grid_spec=grid_spec,
        out_shape=jax.ShapeDtypeStruct((b, s, NA_WIDTH), BF16),
        compiler_params=_params("parallel", "parallel"),
        name="na_attention",
    )(static_shift, proj3d, proj3d, proj3d, bias_pairs)


def _da_kernel(par_ref, q1_ref, q2_ref, k1_ref, k2_ref, v_ref,
               lq1_ref, lk1_ref, lq2_ref, lk2_ref, sub_ref, o_ref, alibi_ref, z_ref, p_ref,
               *, lam_init, kc):
    h = pl.program_id(0)
    qi = pl.program_id(2)
    tq = q1_ref.shape[0]
    s_len = k1_ref.shape[0]
    lam = (jnp.exp(jnp.sum(lq1_ref[...] * lk1_ref[...], axis=-1, keepdims=True))
           - jnp.exp(jnp.sum(lq2_ref[...] * lk2_ref[...], axis=-1, keepdims=True))
           + lam_init)

    bound = par_ref[DA_HEADS]
    static_shift = bound <= MAX_STATIC_SHIFT

    @pl.when((pl.program_id(1) == 0) & (qi == 0))
    def _():
        i = lax.broadcasted_iota(jnp.int32, (tq, 1), 0).astype(F32)
        x = (lax.broadcasted_iota(jnp.int32, (1, alibi_ref.shape[1]), 1) - (s_len - tq)).astype(F32)
        alibi_ref[...] = (jnp.abs(x - i) * (-LOG2_E * par_ref[h])
                          - LOG2_E * jnp.where(static_shift, bound, 0.0))

    win0 = s_len - tq - qi * tq
    n_chunks = s_len // kc
    heads = ((q1_ref, k1_ref), (q2_ref, k2_ref))

    def scores(q, k_ref, c):
        s = lax.dot_general(q, k_ref[c * kc:(c + 1) * kc, :], NT_DIMS, preferred_element_type=F32)
        return s * (ATTN_SCALE * LOG2_E) + alibi_ref[:, pl.ds(pl.multiple_of(win0 + c * kc, V7X_LANES), kc)]

    def finish(o1, l1, o2, l2):
        o = o1 * (1.0 / l1) - o2 * (lam / l2)
        o_ref[...] = (_rms(o) * sub_ref[...] * (1.0 - lam_init)).astype(o_ref.dtype)

    @pl.when(static_shift)
    def _():
        outs = []
        for t, (q_ref, k_ref) in enumerate(heads):
            q = q_ref[...]
            l_lanes = jnp.zeros((tq, V7X_LANES), F32)
            for c in range(n_chunks):
                p = jnp.exp2(scores(q, k_ref, c))
                for j in range(kc // V7X_LANES):
                    l_lanes = l_lanes + p[:, j * V7X_LANES:(j + 1) * V7X_LANES]
                p_ref[t, :, c * kc:(c + 1) * kc] = p.astype(BF16)
            o = jnp.dot(p_ref[t], v_ref[...], preferred_element_type=F32)
            outs += [o, jnp.sum(l_lanes, axis=-1, keepdims=True)]
        finish(*outs)

    @pl.when(jnp.logical_not(static_shift))
    def _():
        outs = []
        for t, (q_ref, k_ref) in enumerate(heads):
            q = q_ref[...]
            m = jnp.full((tq, 1), -jnp.inf, F32)
            for c in range(n_chunks):
                z = scores(q, k_ref, c)
                z_ref[:, c * kc:(c + 1) * kc] = z
                m = jnp.maximum(m, jnp.max(z, axis=-1, keepdims=True))
            l = jnp.zeros((tq, 1), F32)
            for c in range(n_chunks):
                p = jnp.exp2(z_ref[:, c * kc:(c + 1) * kc] - m)
                l = l + jnp.sum(p, axis=-1, keepdims=True)
                p_ref[t, :, c * kc:(c + 1) * kc] = p.astype(BF16)
            outs += [jnp.dot(p_ref[t], v_ref[...], preferred_element_type=F32), l]
        finish(*outs)


def _da_attention(proj3d, slopes, qn_b, kn_b, lam_q1, lam_k1, lam_q2, lam_k2, subln, *, lam_init, tq):
    b, s, _ = proj3d.shape
    qk0 = 3 * NA_WIDTH // HEAD_DIM
    v0 = (3 * NA_WIDTH + 4 * DA_QK) // DA_VDIM
    q_spec = lambda off: pl.BlockSpec((None, tq, HEAD_DIM), lambda h, bi, qi, sl: (bi, qi, off + h))
    k_spec = lambda off: pl.BlockSpec((None, s, HEAD_DIM), lambda h, bi, qi, sl: (bi, 0, off + h))
    vec_spec = lambda n: pl.BlockSpec((1, n), lambda h, bi, qi, sl: (0, 0))
    kc = min(512, s)
    par = jnp.concatenate([slopes, _qk_score_bound(qn_b, kn_b).reshape(1)])
    grid_spec = pltpu.PrefetchScalarGridSpec(
        num_scalar_prefetch=1,
        grid=(DA_HEADS, b, s // tq),
        in_specs=[
            q_spec(qk0), q_spec(qk0 + DA_HEADS), k_spec(qk0 + 2 * DA_HEADS), k_spec(qk0 + 3 * DA_HEADS),
            pl.BlockSpec((None, s, DA_VDIM), lambda h, bi, qi, sl: (bi, 0, v0 + h)),
            vec_spec(HEAD_DIM), vec_spec(HEAD_DIM), vec_spec(HEAD_DIM), vec_spec(HEAD_DIM),
            vec_spec(DA_VDIM),
        ],
        out_specs=pl.BlockSpec((None, tq, DA_VDIM), lambda h, bi, qi, sl: (bi, qi, h)),
        scratch_shapes=[pltpu.VMEM((tq, 2 * s - tq), F32), pltpu.VMEM((tq, s), F32),
                        pltpu.VMEM((2, tq, s), BF16)],
    )
    row = lambda a: a.reshape(1, -1).astype(F32)
    return pl.pallas_call(
        functools.partial(_da_kernel, lam_init=lam_init, kc=kc),
        grid_spec=grid_spec,
        out_shape=jax.ShapeDtypeStruct((b, s, DA_WIDTH), BF16),
        compiler_params=_params("arbitrary", "arbitrary", "arbitrary"),
        name="da_attention",
    )(par, proj3d, proj3d, proj3d, proj3d, proj3d,
      row(lam_q1), row(lam_k1), row(lam_q2), row(lam_k2), row(subln))


def _outproj_kernel(oa_ref, ob_ref, x_ref, ona_ref, wa_ref, wb_ref, g2_ref, wr_ref,
                    x1_ref, lg_ref, *, sub):
    n_exp = lg_ref.shape[0]
    tm = x_ref.shape[0]
    for r in range(tm // sub):
        rows = slice(r * sub, (r + 1) * sub)
        oan = (_rms(oa_ref[rows, :].astype(F32)) * ona_ref[...]).astype(BF16)
        acc = jnp.dot(oan, wa_ref[...], preferred_element_type=F32)
        acc = acc + jnp.dot(ob_ref[rows, :], wb_ref[...], preferred_element_type=F32)
        x1 = x_ref[rows, :] + acc
        x1_ref[rows, :] = x1
        h2 = _rms(x1) * g2_ref[...]
        h_hi = h2.astype(BF16)
        h_lo = (h2 - h_hi.astype(F32)).astype(BF16)
        parts = (jnp.dot(h_hi, wr_ref[...], preferred_element_type=F32)
                 + jnp.dot(h_lo, wr_ref[...], preferred_element_type=F32))
        parts_t = parts.T
        lg_ref[:, rows] = parts_t[:n_exp] + parts_t[n_exp:2 * n_exp]


def _out_projection(oa2d, ob2d, x2d, on_a, w_out, ln2_g, w_router, *, tm):
    n, d = x2d.shape
    n_exp = w_router.shape[1]
    assert 2 * n_exp <= V7X_LANES
    wr_hi = w_router.astype(BF16)
    wr_lo = (w_router - wr_hi.astype(F32)).astype(BF16)
    wr_pieces = jnp.concatenate(
        [wr_hi, wr_lo, jnp.zeros((d, V7X_LANES - 2 * n_exp), BF16)], axis=1)
    assert DA_WIDTH == NA_WIDTH
    half_spec = lambda half: pl.BlockSpec((NA_WIDTH, d), lambda i: (half, 0))
    row_spec = lambda width: pl.BlockSpec((tm, width), lambda i: (i, 0))
    full_spec = lambda a: pl.BlockSpec(a.shape, lambda i: (0, 0))
    on_a = on_a.reshape(1, -1)
    ln2_g = ln2_g.reshape(1, -1)
    return pl.pallas_call(
        functools.partial(_outproj_kernel, sub=min(MATMUL_ROW_SUBTILE, tm)),
        grid=(n // tm,),
        in_specs=[row_spec(NA_WIDTH), row_spec(DA_WIDTH), row_spec(d), full_spec(on_a),
                  half_spec(0), half_spec(1), full_spec(ln2_g), full_spec(wr_pieces)],
        out_specs=[row_spec(d), pl.BlockSpec((n_exp, tm), lambda i: (0, i))],
        out_shape=[jax.ShapeDtypeStruct((n, d), F32), jax.ShapeDtypeStruct((n_exp, n), F32)],
        compiler_params=_params("parallel"),
        name="out_projection",
    )(oa2d, ob2d, x2d, on_a, w_out, w_out, ln2_g, wr_pieces)


def _exclusive_prefix(mask):
    rows, s_len = mask.shape
    m = jnp.where(mask, 1.0, 0.0).astype(BF16)
    r = lax.broadcasted_iota(jnp.int32, (V7X_LANES, V7X_LANES), 0)
    c = lax.broadcasted_iota(jnp.int32, (V7X_LANES, V7X_LANES), 1)
    before = jnp.where(r < c, 1.0, 0.0).astype(BF16)
    ones = jnp.ones((V7X_LANES, V7X_LANES), BF16)
    run = jnp.zeros((rows, V7X_LANES), F32)
    out = []
    for blk in range(s_len // V7X_LANES):
        mb = m[:, blk * V7X_LANES:(blk + 1) * V7X_LANES]
        out.append(run + jnp.dot(mb, before, preferred_element_type=F32))
        run = run + jnp.dot(mb, ones, preferred_element_type=F32)
    return jnp.concatenate(out, axis=1)


def _route_kernel(lg_ref, idx_ref, gate_ref, posm_ref, boff_ref, aff_ref, idx_acc, gate_acc,
                  boff_vmem, boff_smem, sem_ref, *, cap, slot_tile, tb):
    lg = lg_ref[...]
    n_exp, s_len = lg.shape
    ex = jnp.exp(lg - jnp.max(lg, axis=0, keepdims=True))
    aff_ref[...] = ex / jnp.sum(ex, axis=0, keepdims=True)
    aff = aff_ref[...]

    def enough(values, t):
        return jnp.sum(jnp.where(values >= t, 1.0, 0.0), axis=1, keepdims=True) >= cap

    thr = jnp.full((n_exp, 1), 2.0 ** -126, F32)
    for shift in (64, 32, 16, 8, 4, 2, 1):
        cand = thr * (2.0 ** shift)
        thr = jnp.where(enough(aff, cand), cand, thr)
    step = thr
    for _ in range(23):
        step = step * 0.5
        cand = thr + step
        thr = jnp.where(enough(aff, cand), cand, thr)
    thr = jnp.where(enough(aff, thr), thr, 0.0)
    res = aff - thr
    rho = jnp.zeros((n_exp, 1), F32)
    for _ in range(24):
        step = step * 0.5
        cand = rho + step
        rho = jnp.where(enough(res, cand), cand, rho)

    gt = res > rho
    eq = res == rho
    need = cap - jnp.sum(jnp.where(gt, 1.0, 0.0), axis=1, keepdims=True)
    sel = gt | (eq & (_exclusive_prefix(eq) < need))
    posm_ref[...] = jnp.where(sel, _exclusive_prefix(sel), -1.0)

    t_idx = lax.broadcasted_iota(jnp.int32, (s_len, V7X_LANES), 0)
    j_idx = lax.broadcasted_iota(jnp.int32, (s_len, V7X_LANES), 1)
    before_block = jnp.where(t_idx < j_idx * tb, 1.0, 0.0).astype(BF16)
    boff = jnp.dot(jnp.where(sel, 1.0, 0.0).astype(BF16), before_block, preferred_element_type=F32)
    boff_ref[...] = boff.astype(jnp.int32)
    boff_vmem[...] = boff.astype(jnp.int32)
    to_smem = pltpu.make_async_copy(boff_vmem, boff_smem, sem_ref.at[0])
    to_smem.start()
    to_smem.wait()

    n_blk = s_len // tb
    n_tiles = cap // slot_tile
    span = min(s_len, max(tb, ROUTE_WINDOW_TOKENS))
    lane_iota = lax.broadcasted_iota(jnp.int32, (1, span), 1)

    def per_expert(e, carry):
        offs = [boff_smem[e, j] for j in range(n_blk + 1)]
        first_tok, n_rounds = [], 0
        for st in range(n_tiles):
            s0 = st * slot_tile
            j_lo = sum((offs[j + 1] <= s0).astype(jnp.int32) for j in range(n_blk))
            j_hi = sum((offs[j] < s0 + slot_tile).astype(jnp.int32) for j in range(n_blk))
            first_tok.append(j_lo * tb)
            n_rounds = jnp.maximum(n_rounds, ((j_hi - j_lo) * tb + span - 1) // span)
        idx_acc[...] = jnp.zeros_like(idx_acc)
        gate_acc[...] = jnp.zeros_like(gate_acc)

        def one_round(r, c2):
            for st in range(n_tiles):
                nominal = first_tok[st] + r * span
                start = pl.multiple_of(jnp.minimum(nominal, s_len - span), V7X_LANES)
                tok_w = (start + lane_iota).astype(F32)
                pos_w = posm_ref[pl.ds(e, 1), pl.ds(start, span)]
                pos_w = jnp.where(tok_w >= nominal.astype(F32), pos_w, -1.0)
                aff_w = aff_ref[pl.ds(e, 1), pl.ds(start, span)]
                slots = (st * slot_tile + lax.broadcasted_iota(jnp.int32, (slot_tile, 1), 0)).astype(F32)
                hit = pos_w == slots
                rows = pl.ds(st * slot_tile, slot_tile)
                idx_acc[rows, :] += jnp.sum(jnp.where(hit, tok_w, 0.0), axis=1, keepdims=True)
                gate_acc[rows, :] += jnp.sum(jnp.where(hit, aff_w, 0.0), axis=1, keepdims=True)
            return c2

        lax.fori_loop(0, n_rounds, one_round, 0)
        idx_ref[e] = idx_acc[...].astype(jnp.int32)
        gate_ref[e] = gate_acc[...]
        return carry

    lax.fori_loop(0, n_exp, per_expert, 0)


def _ec_route(logits_t, batch, *, cap, slot_tile, tb):
    n_exp, n = logits_t.shape
    s = n // batch
    assert s // tb + 1 <= V7X_LANES
    slot_spec = pl.BlockSpec((n_exp, None, cap, 1), lambda b: (0, b, 0, 0))
    return pl.pallas_call(
        functools.partial(_route_kernel, cap=cap, slot_tile=slot_tile, tb=tb),
        grid=(batch,),
        in_specs=[pl.BlockSpec((n_exp, s), lambda b: (0, b))],
        out_specs=[slot_spec, slot_spec,
                   pl.BlockSpec((None, n_exp, s), lambda b: (b, 0, 0)),
                   pl.BlockSpec((None, n_exp, V7X_LANES), lambda b: (b, 0, 0))],
        out_shape=[jax.ShapeDtypeStruct((n_exp, batch, cap, 1), jnp.int32),
                   jax.ShapeDtypeStruct((n_exp, batch, cap, 1), F32),
                   jax.ShapeDtypeStruct((batch, n_exp, s), F32),
                   jax.ShapeDtypeStruct((batch, n_exp, V7X_LANES), jnp.int32)],
        scratch_shapes=[pltpu.VMEM((n_exp, s), F32), pltpu.VMEM((cap, 1), F32), pltpu.VMEM((cap, 1), F32),
                        pltpu.VMEM((n_exp, V7X_LANES), jnp.int32),
                        pltpu.SMEM((n_exp, V7X_LANES), jnp.int32), pltpu.SemaphoreType.DMA((1,))],
        compiler_params=_params("parallel"),
        name="ec_route",
    )(logits_t)


def _ffn_kernel(rows_ref, x1_hbm, g2_ref, gate_ref, wg_ref, wu_ref, wd_ref, y_ref,
                xs_ref, xsb_ref, acc_ref, sem_ref, *, m_rows, chunk, n_total):
    e = pl.program_id(0)
    f = pl.program_id(1)
    n_exp = pl.num_programs(0)
    n_f = pl.num_programs(1)

    def row_copy(expert, i):
        src = rows_ref[expert * m_rows + jnp.minimum(i, m_rows - 1)]
        return pltpu.make_async_copy(x1_hbm.at[pl.ds(src, 1), :], xs_ref.at[pl.ds(i, 1), :],
                                     sem_ref.at[0])

    def wait_rows(count):
        pltpu.make_async_copy(x1_hbm.at[pl.ds(0, count), :], xs_ref.at[pl.ds(0, count), :],
                              sem_ref.at[0]).wait()

    def normalise_rows():
        xsb_ref[...] = (_rms(xs_ref[pl.ds(0, m_rows), :]) * g2_ref[...]).astype(BF16)

    @pl.when((e == 0) & (f == 0))
    def _():
        def group(ig, carry):
            for u in range(8):
                row_copy(0, ig * 8 + u).start()
            return carry
        lax.fori_loop(0, m_rows // 8, group, 0)
        wait_rows(m_rows)
        normalise_rows()

    @pl.when(f == 0)
    def _():
        acc_ref[...] = jnp.zeros_like(acc_ref)

    nxt = jnp.minimum(e + 1, n_exp - 1)
    for u in range(chunk):
        row_copy(nxt, f * chunk + u).start()

    xs = xsb_ref[...]
    g = jnp.dot(xs, wg_ref[...].astype(BF16), preferred_element_type=F32)
    u = jnp.dot(xs, wu_ref[...].astype(BF16), preferred_element_type=F32)
    a = (g * jax.nn.sigmoid(g) * u).astype(BF16)
    acc_ref[...] += jnp.dot(a, wd_ref[...].astype(BF16), preferred_element_type=F32)

    @pl.when(f == n_f - 1)
    def _():
        y_ref[...] = (acc_ref[...] * gate_ref[...]).astype(y_ref.dtype)
        wait_rows(n_total)
        normalise_rows()


def _ec_ffn(rows_flat, x1, ln2_g, gate, w_gate, w_up, w_down, *, tf):
    n, d = x1.shape
    n_exp, _, ff = w_gate.shape
    m_rows = rows_flat.shape[0] // n_exp
    n_f = ff // tf
    assert m_rows % 8 == 0
    chunk = -(-m_rows // (8 * n_f)) * 8
    n_total = chunk * n_f
    assert n_total - m_rows < m_rows
    grid_spec = pltpu.PrefetchScalarGridSpec(
        num_scalar_prefetch=1,
        grid=(n_exp, n_f),
        in_specs=[
            pl.BlockSpec(memory_space=pl.ANY),
            pl.BlockSpec((1, d), lambda e, f, r: (0, 0)),
            pl.BlockSpec((None, m_rows, 1), lambda e, f, r: (e, 0, 0)),
            pl.BlockSpec((None, d, tf), lambda e, f, r: (e, 0, f)),
            pl.BlockSpec((None, d, tf), lambda e, f, r: (e, 0, f)),
            pl.BlockSpec((None, tf, d), lambda e, f, r: (e, f, 0)),
        ],
        out_specs=pl.BlockSpec((None, m_rows, d), lambda e, f, r: (e, 0, 0)),
        scratch_shapes=[pltpu.VMEM((n_total, d), F32), pltpu.VMEM((m_rows, d), BF16),
                        pltpu.VMEM((m_rows, d), F32), pltpu.SemaphoreType.DMA((1,))],
    )
    return pl.pallas_call(
        functools.partial(_ffn_kernel, m_rows=m_rows, chunk=chunk, n_total=n_total),
        grid_spec=grid_spec,
        out_shape=jax.ShapeDtypeStruct((n_exp, m_rows, d), BF16),
        compiler_params=_params("arbitrary", "arbitrary"),
        name="ec_ffn",
    )(rows_flat, x1, ln2_g.reshape(1, d), gate, w_gate, w_up, w_down)


def _combine_kernel(boff_ref, x1_ref, posm_ref, y_hbm, o_ref, ywin_ref, sem_ref,
                    *, batch, cap, win, adv):
    n_exp, tb = posm_ref.shape
    i = pl.program_id(0)
    n_steps = pl.num_programs(0)
    nblk = n_steps // batch
    buf = i % 2

    def block_offsets(step, shift):
        bb = step // nblk
        return bb, [boff_ref[(bb * n_exp + e) * V7X_LANES + step % nblk + shift] for e in range(n_exp)]

    def window_copies(bb, lowers, to_buf):
        copies, starts = [], []
        for e in range(n_exp):
            start = jnp.minimum((lowers[e] // 16) * 16, cap - win)
            row0 = pl.multiple_of((e * batch + bb) * cap + start, 16)
            copies.append(pltpu.make_async_copy(y_hbm.at[pl.ds(row0, win), :],
                                                ywin_ref.at[to_buf, pl.ds(e * win, win), :],
                                                sem_ref.at[to_buf]))
            starts.append(start)
        return copies, starts

    b, offs = block_offsets(i, 0)
    _, nxts = block_offsets(i, 1)
    most = functools.reduce(jnp.maximum, [nx - of for nx, of in zip(nxts, offs)])

    @pl.when(i == 0)
    def _():
        for cp in window_copies(b, offs, buf)[0]:
            cp.start()

    @pl.when(i + 1 < n_steps)
    def _():
        b_next, offs_next = block_offsets(i + 1, 0)
        for cp in window_copies(b_next, offs_next, 1 - buf)[0]:
            cp.start()

    o_ref[...] = x1_ref[...]
    posm = posm_ref[...]
    prow = jnp.concatenate([jnp.broadcast_to(posm[e:e + 1, :], (win, tb)) for e in range(n_exp)], axis=0)

    def scatter_round(lowers, fetch):
        copies, starts = window_copies(b, lowers, buf)
        if fetch:
            for cp in copies:
                cp.start()
        cols = []
        for e in range(n_exp):
            slot = starts[e] + lax.broadcasted_iota(jnp.int32, (win, 1), 0)
            cols.append(jnp.where((slot >= lowers[e]) & (slot < lowers[e] + adv), slot, -2).astype(F32))
        hit_t = jnp.where(prow == jnp.concatenate(cols, axis=0), 1.0, 0.0).astype(BF16)
        for cp in copies:
            cp.wait()
        o_ref[...] += lax.dot_general(hit_t, ywin_ref[buf], (((0,), (0,)), ((), ())),
                                      preferred_element_type=F32)

    scatter_round(offs, fetch=False)

    def extra_round(k, carry):
        scatter_round([of + k * adv for of in offs], fetch=True)
        return carry

    lax.fori_loop(1, (most + adv - 1) // adv, extra_round, 0)


def _ec_combine(boff_flat, x1, posm, y2d, *, cap, tb, win, adv):
    n, d = x1.shape
    batch, n_exp, s = posm.shape
    nblk = s // tb
    assert cap % 16 == 0 and win % 16 == 0 and win >= adv + 16 and cap >= win
    grid_spec = pltpu.PrefetchScalarGridSpec(
        num_scalar_prefetch=1,
        grid=(n // tb,),
        in_specs=[
            pl.BlockSpec((tb, d), lambda i, bo: (i, 0)),
            pl.BlockSpec((None, n_exp, tb), lambda i, bo: (i // nblk, 0, i % nblk)),
            pl.BlockSpec(memory_space=pl.ANY),
        ],
        out_specs=pl.BlockSpec((tb, d), lambda i, bo: (i, 0)),
        scratch_shapes=[pltpu.VMEM((2, n_exp * win, d), BF16), pltpu.SemaphoreType.DMA((2,))],
    )
    return pl.pallas_call(
        functools.partial(_combine_kernel, batch=batch, cap=cap, win=win, adv=adv),
        grid_spec=grid_spec,
        out_shape=jax.ShapeDtypeStruct((n, d), F32),
        compiler_params=_params("arbitrary"),
        name="ec_combine",
    )(boff_flat, x1, posm, y2d)


COMBINE_TOKENS = 512
COMBINE_WINDOW = 112
COMBINE_ADVANCE = 96
ROUTE_SLOT_TILE = 64


def _tiles(n, d, ff):
    pick = lambda total, want: want if total % want == 0 else total
    return dict(tm_in=pick(n, 1024), tn_in=1024, tq=pick(n, 512), tm_out=pick(n, 512),
                tf=pick(ff, 256))


def kernel(x, ln1_g, w_in, qn_a, kn_a, rpb_a, on_a, qn_b, kn_b, lam_q1, lam_k1, lam_q2, lam_k2,
           subln_b, w_out, ln2_g, w_router, w_gate, w_up, w_down):
    b, s, d = x.shape
    n = b * s
    depth = w_in.shape[0]
    n_exp = w_router.shape[-1]
    cap = EC_CAPACITY_FACTOR * s // n_exp
    t = _tiles(n, d, w_gate.shape[-1])
    slopes = 2.0 ** (-8.0 * jnp.arange(1, DA_HEADS + 1, dtype=F32) / DA_HEADS)
    ones_v = jnp.ones((NA_WIDTH,), F32)
    x2d = x.reshape(n, d)
    for l in range(depth):
        tile_heads = lambda g, heads: jnp.tile(g.astype(F32), heads)
        col_gain = jnp.concatenate([
            tile_heads(qn_a[l], NA_HEADS), tile_heads(kn_a[l], NA_HEADS), ones_v,
            tile_heads(qn_b[l], 2 * DA_HEADS), tile_heads(kn_b[l], 2 * DA_HEADS),
            jnp.ones((DA_WIDTH,), F32)]).reshape(1, IN_COLS)
        proj = _in_projection(x2d, ln1_g[l], w_in[l], col_gain,
                              tm=t["tm_in"], tn=t["tn_in"])
        proj3d = proj.reshape(b, s, IN_COLS)
        oa = _na_attention(proj3d, *_na_bias_table(rpb_a[l], qn_a[l], kn_a[l]))
        ob = _da_attention(proj3d, slopes, qn_b[l], kn_b[l], lam_q1[l], lam_k1[l], lam_q2[l], lam_k2[l], subln_b[l],
                           lam_init=_lambda_init(l), tq=min(t["tq"], s))
        x1, logits_t = _out_projection(
            oa.reshape(n, NA_WIDTH), ob.reshape(n, DA_WIDTH), x2d, on_a[l],
            w_out[l].astype(BF16), ln2_g[l], w_router[l].astype(F32), tm=t["tm_out"])
        idx, gate, posm, boff = _ec_route(logits_t, b, cap=cap, slot_tile=ROUTE_SLOT_TILE, tb=COMBINE_TOKENS)
        rows_flat = (idx[..., 0] + (jnp.arange(b, dtype=jnp.int32) * s)[None, :, None]).reshape(-1)
        gate_e = gate.reshape(n_exp, b * cap, 1)
        y = _ec_ffn(rows_flat, x1, ln2_g[l], gate_e, w_gate[l], w_up[l], w_down[l], tf=t["tf"])
        x2d = _ec_combine(boff.reshape(-1), x1, posm, y.reshape(n_exp * b * cap, d), cap=cap,
                          tb=COMBINE_TOKENS, win=COMBINE_WINDOW, adv=COMBINE_ADVANCE)
    return x2d.reshape(b, s, d)
```

```python
import functools
import math

import jax
import jax.numpy as jnp
from jax import lax
from jax.experimental import pallas as pl
from jax.experimental.pallas import tpu as pltpu

F32 = jnp.float32
BF16 = jnp.bfloat16

GRID_W = 64
HEAD_DIM = 128
NA_HEADS = 8
NA_WIDTH = NA_HEADS * HEAD_DIM
NA_WIN_ROWS = 8
NA_WIN_COLS = 16
DA_HEADS = 4
DA_VDIM = 2 * HEAD_DIM
DA_WIDTH = DA_HEADS * DA_VDIM
DA_QK = DA_HEADS * HEAD_DIM
DA_LAYER_LAMBDA_BASE = 0.8
IN_COLS = 3 * NA_WIDTH + 4 * DA_QK + DA_WIDTH
N_EXPERTS = 16
EC_CAPACITY_FACTOR = 2
RMS_EPS = 1e-6
ATTN_SCALE = HEAD_DIM ** -0.5
LOG2_E = math.log2(math.e)
MAX_STATIC_SHIFT = 30.0
QK_BOUND_MARGIN = 1.02

V7X_LANES = 128
V7X_VMEM_LIMIT_BYTES = 56 * 1024 * 1024
MATMUL_ROW_SUBTILE = 256
NA_ROW_GROUP = 32
ROUTE_WINDOW_TOKENS = 1024

NT_DIMS = (((1,), (1,)), ((), ()))


def _lambda_init(layer_idx):
    return DA_LAYER_LAMBDA_BASE - 0.6 * math.exp(-0.3 * layer_idx)


def _rms(y):
    return y * lax.rsqrt(jnp.mean(y * y, axis=-1, keepdims=True) + RMS_EPS)


def _qk_score_bound(q_gain, k_gain):
    return (QK_BOUND_MARGIN * math.sqrt(HEAD_DIM) * jnp.max(jnp.abs(q_gain.astype(F32)))
            * jnp.max(jnp.abs(k_gain.astype(F32))))


def _params(*semantics):
    return pltpu.CompilerParams(dimension_semantics=semantics,
                                vmem_limit_bytes=V7X_VMEM_LIMIT_BYTES)


def _inproj_kernel(x_ref, g1_ref, w_ref, gain_ref, o_ref, h_ref, *, tn, sub):
    j = pl.program_id(1)

    @pl.when(j == 0)
    def _():
        h_ref[...] = (_rms(x_ref[...]) * g1_ref[...]).astype(BF16)

    w = w_ref[...].astype(BF16)
    col0 = j * tn
    qk_cols = (col0 < 2 * NA_WIDTH) | ((col0 >= 3 * NA_WIDTH) & (col0 < 3 * NA_WIDTH + 4 * DA_QK))

    tm = h_ref.shape[0]
    for r in range(tm // sub):
        rows = slice(r * sub, (r + 1) * sub)
        acc = jnp.dot(h_ref[rows, :], w, preferred_element_type=F32)
        for c in range(tn // HEAD_DIM):
            sl = slice(c * HEAD_DIM, (c + 1) * HEAD_DIM)
            y = acc[:, sl]
            inv = lax.rsqrt(jnp.mean(y * y, axis=-1, keepdims=True) + RMS_EPS)
            o_ref[rows, sl] = (y * jnp.where(qk_cols, inv, 1.0) * gain_ref[:, sl]).astype(o_ref.dtype)


def _in_projection(x2d, ln1_g, w_in, col_gain, *, tm, tn):
    n, d = x2d.shape
    cols = w_in.shape[1]
    return pl.pallas_call(
        functools.partial(_inproj_kernel, tn=tn, sub=min(MATMUL_ROW_SUBTILE, tm)),
        grid=(n // tm, cols // tn),
        in_specs=[
            pl.BlockSpec((tm, d), lambda i, j: (i, 0)),
            pl.BlockSpec((1, d), lambda i, j: (0, 0)),
            pl.BlockSpec((d, tn), lambda i, j: (0, j)),
            pl.BlockSpec((1, tn), lambda i, j: (0, j)),
        ],
        out_specs=pl.BlockSpec((tm, tn), lambda i, j: (i, j)),
        out_shape=jax.ShapeDtypeStruct((n, cols), BF16),
        scratch_shapes=[pltpu.VMEM((tm, d), BF16)],
        compiler_params=_params("parallel", "arbitrary"),
        name="in_projection",
    )(x2d, ln1_g.reshape(1, d), w_in, col_gain)


def _na_bias_table(rpb, qn_a, kn_a):
    rpb = rpb.astype(F32)
    c = jnp.arange(GRID_W)
    col_off = jnp.clip(c[None, :] - c[:, None], -(NA_WIN_COLS - 1), NA_WIN_COLS - 1) + (NA_WIN_COLS - 1)
    onehot = (col_off[None] == jnp.arange(2 * NA_WIN_COLS - 1)[:, None, None]).astype(F32)
    t = jnp.einsum("hro,oqk->hrqk", rpb, onehot, precision=lax.Precision.HIGHEST)
    cs = jnp.clip(c - NA_WIN_COLS // 2, 0, GRID_W - NA_WIN_COLS)
    col_valid = (c[None, :] >= cs[:, None]) & (c[None, :] < cs[:, None] + NA_WIN_COLS)
    qk_bound = _qk_score_bound(qn_a, kn_a)
    static_shift = 2.0 * qk_bound + (jnp.max(rpb) - jnp.min(rpb)) <= 2.0 * MAX_STATIC_SHIFT
    shift = jnp.where(static_shift, qk_bound + jnp.max(rpb), 0.0)
    t = jnp.where(col_valid, (t - shift) * LOG2_E, -jnp.inf)
    pairs = jnp.concatenate([t[:, :-1], t[:, 1:]], axis=-1)
    return pairs, static_shift.astype(jnp.int32).reshape(1)


def _na_kernel(flag_ref, q_ref, k_ref, v_ref, bias_ref, o_ref, *, rows, group):
    win = NA_WIN_ROWS * GRID_W

    def make_body(static_shift):
        def body(it, carry):
            r = [it * group + g for g in range(group)]
            rs = [jnp.clip(ri - NA_WIN_ROWS // 2, 0, rows - NA_WIN_ROWS) for ri in r]
            q0 = [pl.multiple_of(ri * GRID_W, GRID_W) for ri in r]
            k0 = [pl.multiple_of(rsi * GRID_W, GRID_W) for rsi in rs]
            s = [lax.dot_general(q_ref[pl.ds(q0[g], GRID_W), :], k_ref[pl.ds(k0[g], win), :], NT_DIMS,
                                 preferred_element_type=F32) for g in range(group)]
            p, l = [], []
            for g in range(group):
                base = rs[g] - r[g] + (NA_WIN_ROWS - 1)
                bias = jnp.concatenate([bias_ref[base + 2 * jj] for jj in range(NA_WIN_ROWS // 2)], axis=1)
                z = s[g] * (ATTN_SCALE * LOG2_E) + bias
                if not static_shift:
                    z = z - jnp.max(z, axis=-1, keepdims=True)
                pg = jnp.exp2(z)
                l.append(jnp.sum(pg, axis=-1, keepdims=True))
                p.append(pg.astype(BF16))
            for g in range(group):
                o = jnp.dot(p[g], v_ref[pl.ds(k0[g], win), :], preferred_element_type=F32)
                o_ref[pl.ds(q0[g], GRID_W), :] = (o / l[g]).astype(o_ref.dtype)
            return carry
        return body

    @pl.when(flag_ref[0] == 1)
    def _():
        lax.fori_loop(0, rows // group, make_body(True), 0)

    @pl.when(flag_ref[0] != 1)
    def _():
        lax.fori_loop(0, rows // group, make_body(False), 0)


def _na_attention(proj3d, bias_pairs, static_shift):
    b, s, _ = proj3d.shape
    rows = s // GRID_W
    assert rows >= NA_WIN_ROWS
    qkv_spec = lambda off: pl.BlockSpec((None, s, HEAD_DIM), lambda bi, h, fl: (bi, 0, off + h))
    grid_spec = pltpu.PrefetchScalarGridSpec(
        num_scalar_prefetch=1,
        grid=(b, NA_HEADS),
        in_specs=[
            qkv_spec(0), qkv_spec(NA_HEADS), qkv_spec(2 * NA_HEADS),
            pl.BlockSpec((None,) + bias_pairs.shape[1:], lambda bi, h, fl: (h, 0, 0, 0)),
        ],
        out_specs=pl.BlockSpec((None, s, HEAD_DIM), lambda bi, h, fl: (bi, 0, h)),
    )
    return pl.pallas_call(
        functools.partial(_na_kernel, rows=rows, group=math.gcd(rows, NA_ROW_GROUP)),
        grid_spec=grid_spec,
        out_shape=jax.ShapeDtypeStruct((b, s, NA_WIDTH), BF16),
        compiler_params=_params("parallel", "parallel"),
        name="na_attention",
    )(static_shift, proj3d, proj3d, proj3d, bias_pairs)


def _da_kernel(par_ref, q1_ref, q2_ref, k1_ref, k2_ref, v_ref,
               lq1_ref, lk1_ref, lq2_ref, lk2_ref, sub_ref, o_ref, alibi_ref, z_ref, p_ref,
               *, lam_init, kc):
    h = pl.program_id(0)
    qi = pl.program_id(2)
    tq = q1_ref.shape[0]
    s_len = k1_ref.shape[0]
    lam = (jnp.exp(jnp.sum(lq1_ref[...] * lk1_ref[...], axis=-1, keepdims=True))
           - jnp.exp(jnp.sum(lq2_ref[...] * lk2_ref[...], axis=-1, keepdims=True))
           + lam_init)

    bound = par_ref[DA_HEADS]
    static_shift = bound <= MAX_STATIC_SHIFT

    @pl.when((pl.program_id(1) == 0) & (qi == 0))
    def _():
        i = lax.broadcasted_iota(jnp.int32, (tq, 1), 0).astype(F32)
        x = (lax.broadcasted_iota(jnp.int32, (1, alibi_ref.shape[1]), 1) - (s_len - tq)).astype(F32)
        alibi_ref[...] = (jnp.abs(x - i) * (-LOG2_E * par_ref[h])
                          - LOG2_E * jnp.where(static_shift, bound, 0.0))

    win0 = s_len - tq - qi * tq
    n_chunks = s_len // kc
    heads = ((q1_ref, k1_ref), (q2_ref, k2_ref))

    def scores(q, k_ref, c):
        s = lax.dot_general(q, k_ref[c * kc:(c + 1) * kc, :], NT_DIMS, preferred_element_type=F32)
        return s * (ATTN_SCALE * LOG2_E) + alibi_ref[:, pl.ds(pl.multiple_of(win0 + c * kc, V7X_LANES), kc)]

    def finish(o1, l1, o2, l2):
        o = o1 * (1.0 / l1) - o2 * (lam / l2)
        o_ref[...] = (_rms(o) * sub_ref[...] * (1.0 - lam_init)).astype(o_ref.dtype)

    @pl.when(static_shift)
    def _():
        outs = []
        for t, (q_ref, k_ref) in enumerate(heads):
            q = q_ref[...]
            l_lanes = jnp.zeros((tq, V7X_LANES), F32)
            for c in range(n_chunks):
                p = jnp.exp2(scores(q, k_ref, c))
                for j in range(kc // V7X_LANES):
                    l_lanes = l_lanes + p[:, j * V7X_LANES:(j + 1) * V7X_LANES]
                p_ref[t, :, c * kc:(c + 1) * kc] = p.astype(BF16)
            o = jnp.dot(p_ref[t], v_ref[...], preferred_element_type=F32)
            outs += [o, jnp.sum(l_lanes, axis=-1, keepdims=True)]
        finish(*outs)

    @pl.when(jnp.logical_not(static_shift))
    def _():
        outs = []
        for t, (q_ref, k_ref) in enumerate(heads):
            q = q_ref[...]
            m = jnp.full((tq, 1), -jnp.inf, F32)
            for c in range(n_chunks):
                z = scores(q, k_ref, c)
                z_ref[:, c * kc:(c + 1) * kc] = z
                m = jnp.maximum(m, jnp.max(z, axis=-1, keepdims=True))
            l = jnp.zeros((tq, 1), F32)
            for c in range(n_chunks):
                p = jnp.exp2(z_ref[:, c * kc:(c + 1) * kc] - m)
                l = l + jnp.sum(p, axis=-1, keepdims=True)
                p_ref[t, :, c * kc:(c + 1) * kc] = p.astype(BF16)
            outs += [jnp.dot(p_ref[t], v_ref[...], preferred_element_type=F32), l]
        finish(*outs)


def _da_attention(proj3d, slopes, qn_b, kn_b, lam_q1, lam_k1, lam_q2, lam_k2, subln, *, lam_init, tq):
    b, s, _ = proj3d.shape
    qk0 = 3 * NA_WIDTH // HEAD_DIM
    v0 = (3 * NA_WIDTH + 4 * DA_QK) // DA_VDIM
    q_spec = lambda off: pl.BlockSpec((None, tq, HEAD_DIM), lambda h, bi, qi, sl: (bi, qi, off + h))
    k_spec = lambda off: pl.BlockSpec((None, s, HEAD_DIM), lambda h, bi, qi, sl: (bi, 0, off + h))
    vec_spec = lambda n: pl.BlockSpec((1, n), lambda h, bi, qi, sl: (0, 0))
    kc = min(512, s)
    par = jnp.concatenate([slopes, _qk_score_bound(qn_b, kn_b).reshape(1)])
    grid_spec = pltpu.PrefetchScalarGridSpec(
        num_scalar_prefetch=1,
        grid=(DA_HEADS, b, s // tq),
        in_specs=[
            q_spec(qk0), q_spec(qk0 + DA_HEADS), k_spec(qk0 + 2 * DA_HEADS), k_spec(qk0 + 3 * DA_HEADS),
            pl.BlockSpec((None, s, DA_VDIM), lambda h, bi, qi, sl: (bi, 0, v0 + h)),
            vec_spec(HEAD_DIM), vec_spec(HEAD_DIM), vec_spec(HEAD_DIM), vec_spec(HEAD_DIM),
            vec_spec(DA_VDIM),
        ],
        out_specs=pl.BlockSpec((None, tq, DA_VDIM), lambda h, bi, qi, sl: (bi, qi, h)),
        scratch_shapes=[pltpu.VMEM((tq, 2 * s - tq), F32), pltpu.VMEM((tq, s), F32),
                        pltpu.VMEM((2, tq, s), BF16)],
    )
    row = lambda a: a.reshape(1, -1).astype(F32)
    return pl.pallas_call(
        functools.partial(_da_kernel, lam_init=lam_init, kc=kc),
        grid_spec=grid_spec,
        out_shape=jax.ShapeDtypeStruct((b, s, DA_WIDTH), BF16),
        compiler_params=_params("arbitrary", "arbitrary", "arbitrary"),
        name="da_attention",
    )(par, proj3d, proj3d, proj3d, proj3d, proj3d,
      row(lam_q1), row(lam_k1), row(lam_q2), row(lam_k2), row(subln))


def _outproj_kernel(oa_ref, ob_ref, x_ref, ona_ref, wa_ref, wb_ref, g2_ref, wr_ref,
                    x1_ref, lg_ref, *, sub):
    n_exp = lg_ref.shape[0]
    tm = x_ref.shape[0]
    for r in range(tm // sub):
        rows = slice(r * sub, (r + 1) * sub)
        oan = (_rms(oa_ref[rows, :].astype(F32)) * ona_ref[...]).astype(BF16)
        acc = jnp.dot(oan, wa_ref[...], preferred_element_type=F32)
        acc = acc + jnp.dot(ob_ref[rows, :], wb_ref[...], preferred_element_type=F32)
        x1 = x_ref[rows, :] + acc
        x1_ref[rows, :] = x1
        h2 = _rms(x1) * g2_ref[...]
        h_hi = h2.astype(BF16)
        h_lo = (h2 - h_hi.astype(F32)).astype(BF16)
        parts = (jnp.dot(h_hi, wr_ref[...], preferred_element_type=F32)
                 + jnp.dot(h_lo, wr_ref[...], preferred_element_type=F32))
        parts_t = parts.T
        lg_ref[:, rows] = parts_t[:n_exp] + parts_t[n_exp:2 * n_exp]


def _out_projection(oa2d, ob2d, x2d, on_a, w_out, ln2_g, w_router, *, tm):
    n, d = x2d.shape
    n_exp = w_router.shape[1]
    assert 2 * n_exp <= V7X_LANES
    wr_hi = w_router.astype(BF16)
    wr_lo = (w_router - wr_hi.astype(F32)).astype(BF16)
    wr_pieces = jnp.concatenate(
        [wr_hi, wr_lo, jnp.zeros((d, V7X_LANES - 2 * n_exp), BF16)], axis=1)
    assert DA_WIDTH == NA_WIDTH
    half_spec = lambda half: pl.BlockSpec((NA_WIDTH, d), lambda i: (half, 0))
    row_spec = lambda width: pl.BlockSpec((tm, width), lambda i: (i, 0))
    full_spec = lambda a: pl.BlockSpec(a.shape, lambda i: (0, 0))
    on_a = on_a.reshape(1, -1)
    ln2_g = ln2_g.reshape(1, -1)
    return pl.pallas_call(
        functools.partial(_outproj_kernel, sub=min(MATMUL_ROW_SUBTILE, tm)),
        grid=(n // tm,),
        in_specs=[row_spec(NA_WIDTH), row_spec(DA_WIDTH), row_spec(d), full_spec(on_a),
                  half_spec(0), half_spec(1), full_spec(ln2_g), full_spec(wr_pieces)],
        out_specs=[row_spec(d), pl.BlockSpec((n_exp, tm), lambda i: (0, i))],
        out_shape=[jax.ShapeDtypeStruct((n, d), F32), jax.ShapeDtypeStruct((n_exp, n), F32)],
        compiler_params=_params("parallel"),
        name="out_projection",
    )(oa2d, ob2d, x2d, on_a, w_out, w_out, ln2_g, wr_pieces)


def _exclusive_prefix(mask):
    rows, s_len = mask.shape
    m = jnp.where(mask, 1.0, 0.0).astype(BF16)
    r = lax.broadcasted_iota(jnp.int32, (V7X_LANES, V7X_LANES), 0)
    c = lax.broadcasted_iota(jnp.int32, (V7X_LANES, V7X_LANES), 1)
    before = jnp.where(r < c, 1.0, 0.0).astype(BF16)
    ones = jnp.ones((V7X_LANES, V7X_LANES), BF16)
    run = jnp.zeros((rows, V7X_LANES), F32)
    out = []
    for blk in range(s_len // V7X_LANES):
        mb = m[:, blk * V7X_LANES:(blk + 1) * V7X_LANES]
        out.append(run + jnp.dot(mb, before, preferred_element_type=F32))
        run = run + jnp.dot(mb, ones, preferred_element_type=F32)
    return jnp.concatenate(out, axis=1)


def _route_kernel(lg_ref, idx_ref, gate_ref, posm_ref, boff_ref, aff_ref, idx_acc, gate_acc,
                  boff_vmem, boff_smem, sem_ref, *, cap, slot_tile, tb):
    lg = lg_ref[...]
    n_exp, s_len = lg.shape
    ex = jnp.exp(lg - jnp.max(lg, axis=0, keepdims=True))
    aff_ref[...] = ex / jnp.sum(ex, axis=0, keepdims=True)
    aff = aff_ref[...]

    def enough(values, t):
        return jnp.sum(jnp.where(values >= t, 1.0, 0.0), axis=1, keepdims=True) >= cap

    thr = jnp.full((n_exp, 1), 2.0 ** -126, F32)
    for shift in (64, 32, 16, 8, 4, 2, 1):
        cand = thr * (2.0 ** shift)
        thr = jnp.where(enough(aff, cand), cand, thr)
    step = thr
    for _ in range(23):
        step = step * 0.5
        cand = thr + step
        thr = jnp.where(enough(aff, cand), cand, thr)
    thr = jnp.where(enough(aff, thr), thr, 0.0)
    res = aff - thr
    rho = jnp.zeros((n_exp, 1), F32)
    for _ in range(24):
        step = step * 0.5
        cand = rho + step
        rho = jnp.where(enough(res, cand), cand, rho)

    gt = res > rho
    eq = res == rho
    need = cap - jnp.sum(jnp.where(gt, 1.0, 0.0), axis=1, keepdims=True)
    sel = gt | (eq & (_exclusive_prefix(eq) < need))
    posm_ref[...] = jnp.where(sel, _exclusive_prefix(sel), -1.0)

    t_idx = lax.broadcasted_iota(jnp.int32, (s_len, V7X_LANES), 0)
    j_idx = lax.broadcasted_iota(jnp.int32, (s_len, V7X_LANES), 1)
    before_block = jnp.where(t_idx < j_idx * tb, 1.0, 0.0).astype(BF16)
    boff = jnp.dot(jnp.where(sel, 1.0, 0.0).astype(BF16), before_block, preferred_element_type=F32)
    boff_ref[...] = boff.astype(jnp.int32)
    boff_vmem[...] = boff.astype(jnp.int32)
    to_smem = pltpu.make_async_copy(boff_vmem, boff_smem, sem_ref.at[0])
    to_smem.start()
    to_smem.wait()

    n_blk = s_len // tb
    n_tiles = cap // slot_tile
    span = min(s_len, max(tb, ROUTE_WINDOW_TOKENS))
    lane_iota = lax.broadcasted_iota(jnp.int32, (1, span), 1)

    def per_expert(e, carry):
        offs = [boff_smem[e, j] for j in range(n_blk + 1)]
        first_tok, n_rounds = [], 0
        for st in range(n_tiles):
            s0 = st * slot_tile
            j_lo = sum((offs[j + 1] <= s0).astype(jnp.int32) for j in range(n_blk))
            j_hi = sum((offs[j] < s0 + slot_tile).astype(jnp.int32) for j in range(n_blk))
            first_tok.append(j_lo * tb)
            n_rounds = jnp.maximum(n_rounds, ((j_hi - j_lo) * tb + span - 1) // span)
        idx_acc[...] = jnp.zeros_like(idx_acc)
        gate_acc[...] = jnp.zeros_like(gate_acc)

        def one_round(r, c2):
            for st in range(n_tiles):
                nominal = first_tok[st] + r * span
                start = pl.multiple_of(jnp.minimum(nominal, s_len - span), V7X_LANES)
                tok_w = (start + lane_iota).astype(F32)
                pos_w = posm_ref[pl.ds(e, 1), pl.ds(start, span)]
                pos_w = jnp.where(tok_w >= nominal.astype(F32), pos_w, -1.0)
                aff_w = aff_ref[pl.ds(e, 1), pl.ds(start, span)]
                slots = (st * slot_tile + lax.broadcasted_iota(jnp.int32, (slot_tile, 1), 0)).astype(F32)
                hit = pos_w == slots
                rows = pl.ds(st * slot_tile, slot_tile)
                idx_acc[rows, :] += jnp.sum(jnp.where(hit, tok_w, 0.0), axis=1, keepdims=True)
                gate_acc[rows, :] += jnp.sum(jnp.where(hit, aff_w, 0.0), axis=1, keepdims=True)
            return c2

        lax.fori_loop(0, n_rounds, one_round, 0)
        idx_ref[e] = idx_acc[...].astype(jnp.int32)
        gate_ref[e] = gate_acc[...]
        return carry

    lax.fori_loop(0, n_exp, per_expert, 0)


def _ec_route(logits_t, batch, *, cap, slot_tile, tb):
    n_exp, n = logits_t.shape
    s = n // batch
    assert s // tb + 1 <= V7X_LANES
    slot_spec = pl.BlockSpec((n_exp, None, cap, 1), lambda b: (0, b, 0, 0))
    return pl.pallas_call(
        functools.partial(_route_kernel, cap=cap, slot_tile=slot_tile, tb=tb),
        grid=(batch,),
        in_specs=[pl.BlockSpec((n_exp, s), lambda b: (0, b))],
        out_specs=[slot_spec, slot_spec,
                   pl.BlockSpec((None, n_exp, s), lambda b: (b, 0, 0)),
                   pl.BlockSpec((None, n_exp, V7X_LANES), lambda b: (b, 0, 0))],
        out_shape=[jax.ShapeDtypeStruct((n_exp, batch, cap, 1), jnp.int32),
                   jax.ShapeDtypeStruct((n_exp, batch, cap, 1), F32),
                   jax.ShapeDtypeStruct((batch, n_exp, s), F32),
                   jax.ShapeDtypeStruct((batch, n_exp, V7X_LANES), jnp.int32)],
        scratch_shapes=[pltpu.VMEM((n_exp, s), F32), pltpu.VMEM((cap, 1), F32), pltpu.VMEM((cap, 1), F32),
                        pltpu.VMEM((n_exp, V7X_LANES), jnp.int32),
                        pltpu.SMEM((n_exp, V7X_LANES), jnp.int32), pltpu.SemaphoreType.DMA((1,))],
        compiler_params=_params("parallel"),
        name="ec_route",
    )(logits_t)


def _ffn_kernel(rows_ref, x1_hbm, g2_ref, gate_ref, wg_ref, wu_ref, wd_ref, y_ref,
                xs_ref, xsb_ref, acc_ref, sem_ref, *, m_rows, chunk, n_total):
    e = pl.program_id(0)
    f = pl.program_id(1)
    n_exp = pl.num_programs(0)
    n_f = pl.num_programs(1)

    def row_copy(expert, i):
        src = rows_ref[expert * m_rows + jnp.minimum(i, m_rows - 1)]
        return pltpu.make_async_copy(x1_hbm.at[pl.ds(src, 1), :], xs_ref.at[pl.ds(i, 1), :],
                                     sem_ref.at[0])

    def wait_rows(count):
        pltpu.make_async_copy(x1_hbm.at[pl.ds(0, count), :], xs_ref.at[pl.ds(0, count), :],
                              sem_ref.at[0]).wait()

    def normalise_rows():
        xsb_ref[...] = (_rms(xs_ref[pl.ds(0, m_rows), :]) * g2_ref[...]).astype(BF16)

    @pl.when((e == 0) & (f == 0))
    def _():
        def group(ig, carry):
            for u in range(8):
                row_copy(0, ig * 8 + u).start()
            return carry
        lax.fori_loop(0, m_rows // 8, group, 0)
        wait_rows(m_rows)
        normalise_rows()

    @pl.when(f == 0)
    def _():
        acc_ref[...] = jnp.zeros_like(acc_ref)

    nxt = jnp.minimum(e + 1, n_exp - 1)
    for u in range(chunk):
        row_copy(nxt, f * chunk + u).start()

    xs = xsb_ref[...]
    g = jnp.dot(xs, wg_ref[...].astype(BF16), preferred_element_type=F32)
    u = jnp.dot(xs, wu_ref[...].astype(BF16), preferred_element_type=F32)
    a = (g * jax.nn.sigmoid(g) * u).astype(BF16)
    acc_ref[...] += jnp.dot(a, wd_ref[...].astype(BF16), preferred_element_type=F32)

    @pl.when(f == n_f - 1)
    def _():
        y_ref[...] = (acc_ref[...] * gate_ref[...]).astype(y_ref.dtype)
        wait_rows(n_total)
        normalise_rows()


def _ec_ffn(rows_flat, x1, ln2_g, gate, w_gate, w_up, w_down, *, tf):
    n, d = x1.shape
    n_exp, _, ff = w_gate.shape
    m_rows = rows_flat.shape[0] // n_exp
    n_f = ff // tf
    assert m_rows % 8 == 0
    chunk = -(-m_rows // (8 * n_f)) * 8
    n_total = chunk * n_f
    assert n_total - m_rows < m_rows
    grid_spec = pltpu.PrefetchScalarGridSpec(
        num_scalar_prefetch=1,
        grid=(n_exp, n_f),
        in_specs=[
            pl.BlockSpec(memory_space=pl.ANY),
            pl.BlockSpec((1, d), lambda e, f, r: (0, 0)),
            pl.BlockSpec((None, m_rows, 1), lambda e, f, r: (e, 0, 0)),
            pl.BlockSpec((None, d, tf), lambda e, f, r: (e, 0, f)),
            pl.BlockSpec((None, d, tf), lambda e, f, r: (e, 0, f)),
            pl.BlockSpec((None, tf, d), lambda e, f, r: (e, f, 0)),
        ],
        out_specs=pl.BlockSpec((None, m_rows, d), lambda e, f, r: (e, 0, 0)),
        scratch_shapes=[pltpu.VMEM((n_total, d), F32), pltpu.VMEM((m_rows, d), BF16),
                        pltpu.VMEM((m_rows, d), F32), pltpu.SemaphoreType.DMA((1,))],
    )
    return pl.pallas_call(
        functools.partial(_ffn_kernel, m_rows=m_rows, chunk=chunk, n_total=n_total),
        grid_spec=grid_spec,
        out_shape=jax.ShapeDtypeStruct((n_exp, m_rows, d), BF16),
        compiler_params=_params("arbitrary", "arbitrary"),
        name="ec_ffn",
    )(rows_flat, x1, ln2_g.reshape(1, d), gate, w_gate, w_up, w_down)


def _combine_kernel(boff_ref, x1_ref, posm_ref, y_hbm, o_ref, ywin_ref, sem_ref,
                    *, batch, cap, win, adv):
    n_exp, tb = posm_ref.shape
    i = pl.program_id(0)
    n_steps = pl.num_programs(0)
    nblk = n_steps // batch
    buf = i % 2

    def block_offsets(step, shift):
        bb = step // nblk
        return bb, [boff_ref[(bb * n_exp + e) * V7X_LANES + step % nblk + shift] for e in range(n_exp)]

    def window_copies(bb, lowers, to_buf):
        copies, starts = [], []
        for e in range(n_exp):
            start = jnp.minimum((lowers[e] // 16) * 16, cap - win)
            row0 = pl.multiple_of((e * batch + bb) * cap + start, 16)
            copies.append(pltpu.make_async_copy(y_hbm.at[pl.ds(row0, win), :],
                                                ywin_ref.at[to_buf, pl.ds(e * win, win), :],
                                                sem_ref.at[to_buf]))
            starts.append(start)
        return copies, starts

    b, offs = block_offsets(i, 0)
    _, nxts = block_offsets(i, 1)
    most = functools.reduce(jnp.maximum, [nx - of for nx, of in zip(nxts, offs)])

    @pl.when(i == 0)
    def _():
        for cp in window_copies(b, offs, buf)[0]:
            cp.start()

    @pl.when(i + 1 < n_steps)
    def _():
        b_next, offs_next = block_offsets(i + 1, 0)
        for cp in window_copies(b_next, offs_next, 1 - buf)[0]:
            cp.start()

    o_ref[...] = x1_ref[...]
    posm = posm_ref[...]
    prow = jnp.concatenate([jnp.broadcast_to(posm[e:e + 1, :], (win, tb)) for e in range(n_exp)], axis=0)

    def scatter_round(lowers, fetch):
        copies, starts = window_copies(b, lowers, buf)
        if fetch:
            for cp in copies:
                cp.start()
        cols = []
        for e in range(n_exp):
            slot = starts[e] + lax.broadcasted_iota(jnp.int32, (win, 1), 0)
            cols.append(jnp.where((slot >= lowers[e]) & (slot < lowers[e] + adv), slot, -2).astype(F32))
        hit_t = jnp.where(prow == jnp.concatenate(cols, axis=0), 1.0, 0.0).astype(BF16)
        for cp in copies:
            cp.wait()
        o_ref[...] += lax.dot_general(hit_t, ywin_ref[buf], (((0,), (0,)), ((), ())),
                                      preferred_element_type=F32)

    scatter_round(offs, fetch=False)

    def extra_round(k, carry):
        scatter_round([of + k * adv for of in offs], fetch=True)
        return carry

    lax.fori_loop(1, (most + adv - 1) // adv, extra_round, 0)


def _ec_combine(boff_flat, x1, posm, y2d, *, cap, tb, win, adv):
    n, d = x1.shape
    batch, n_exp, s = posm.shape
    nblk = s // tb
    assert cap % 16 == 0 and win % 16 == 0 and win >= adv + 16 and cap >= win
    grid_spec = pltpu.PrefetchScalarGridSpec(
        num_scalar_prefetch=1,
        grid=(n // tb,),
        in_specs=[
            pl.BlockSpec((tb, d), lambda i, bo: (i, 0)),
            pl.BlockSpec((None, n_exp, tb), lambda i, bo: (i // nblk, 0, i % nblk)),
            pl.BlockSpec(memory_space=pl.ANY),
        ],
        out_specs=pl.BlockSpec((tb, d), lambda i, bo: (i, 0)),
        scratch_shapes=[pltpu.VMEM((2, n_exp * win, d), BF16), pltpu.SemaphoreType.DMA((2,))],
    )
    return pl.pallas_call(
        functools.partial(_combine_kernel, batch=batch, cap=cap, win=win, adv=adv),
        grid_spec=grid_spec,
        out_shape=jax.ShapeDtypeStruct((n, d), F32),
        compiler_params=_params("arbitrary"),
        name="ec_combine",
    )(boff_flat, x1, posm, y2d)


COMBINE_TOKENS = 256
COMBINE_WINDOW = 64
COMBINE_ADVANCE = 48
ROUTE_SLOT_TILE = 64


def _tiles(n, d, ff):
    pick = lambda total, want: want if total % want == 0 else total
    return dict(tm_in=pick(n, 1024), tn_in=1024, tq=pick(n, 512), tm_out=pick(n, 512),
                tf=pick(ff, 256))


def kernel(x, ln1_g, w_in, qn_a, kn_a, rpb_a, on_a, qn_b, kn_b, lam_q1, lam_k1, lam_q2, lam_k2,
           subln_b, w_out, ln2_g, w_router, w_gate, w_up, w_down):
    b, s, d = x.shape
    n = b * s
    depth = w_in.shape[0]
    n_exp = w_router.shape[-1]
    cap = EC_CAPACITY_FACTOR * s // n_exp
    t = _tiles(n, d, w_gate.shape[-1])
    slopes = 2.0 ** (-8.0 * jnp.arange(1, DA_HEADS + 1, dtype=F32) / DA_HEADS)
    ones_v = jnp.ones((NA_WIDTH,), F32)
    x2d = x.reshape(n, d)
    for l in range(depth):
        tile_heads = lambda g, heads: jnp.tile(g.astype(F32), heads)
        col_gain = jnp.concatenate([
            tile_heads(qn_a[l], NA_HEADS), tile_heads(kn_a[l], NA_HEADS), ones_v,
            tile_heads(qn_b[l], 2 * DA_HEADS), tile_heads(kn_b[l], 2 * DA_HEADS),
            jnp.ones((DA_WIDTH,), F32)]).reshape(1, IN_COLS)
        proj = _in_projection(x2d, ln1_g[l], w_in[l], col_gain,
                              tm=t["tm_in"], tn=t["tn_in"])
        proj3d = proj.reshape(b, s, IN_COLS)
        oa = _na_attention(proj3d, *_na_bias_table(rpb_a[l], qn_a[l], kn_a[l]))
        ob = _da_attention(proj3d, slopes, qn_b[l], kn_b[l], lam_q1[l], lam_k1[l], lam_q2[l], lam_k2[l], subln_b[l],
                           lam_init=_lambda_init(l), tq=min(t["tq"], s))
        x1, logits_t = _out_projection(
            oa.reshape(n, NA_WIDTH), ob.reshape(n, DA_WIDTH), x2d, on_a[l],
            w_out[l].astype(BF16), ln2_g[l], w_router[l].astype(F32), tm=t["tm_out"])
        idx, gate, posm, boff = _ec_route(logits_t, b, cap=cap, slot_tile=ROUTE_SLOT_TILE, tb=COMBINE_TOKENS)
        rows_flat = (idx[..., 0] + (jnp.arange(b, dtype=jnp.int32) * s)[None, :, None]).reshape(-1)
        gate_e = gate.reshape(n_exp, b * cap, 1)
        y = _ec_ffn(rows_flat, x1, ln2_g[l], gate_e, w_gate[l], w_up[l], w_down[l], tf=t["tf"])
        x2d = _ec_combine(boff.reshape(-1), x1, posm, y.reshape(n_exp * b * cap, d), cap=cap,
                          tb=COMBINE_TOKENS, win=COMBINE_WINDOW, adv=COMBINE_ADVANCE)
    return x2d.reshape(b, s, d)
```

```python
import functools
import math

import jax
import jax.numpy as jnp
from jax import lax
from jax.experimental import pallas as pl
from jax.experimental.pallas import tpu as pltpu

F32 = jnp.float32
BF16 = jnp.bfloat16

GRID_W = 64
HEAD_DIM = 128
NA_HEADS = 8
NA_WIDTH = NA_HEADS * HEAD_DIM
NA_WIN_ROWS = 8
NA_WIN_COLS = 16
DA_HEADS = 4
DA_VDIM = 2 * HEAD_DIM
DA_WIDTH = DA_HEADS * DA_VDIM
DA_QK = DA_HEADS * HEAD_DIM
DA_LAYER_LAMBDA_BASE = 0.8
IN_COLS = 3 * NA_WIDTH + 4 * DA_QK + DA_WIDTH
N_EXPERTS = 16
EC_CAPACITY_FACTOR = 2
RMS_EPS = 1e-6
ATTN_SCALE = HEAD_DIM ** -0.5
LOG2_E = math.log2(math.e)
MAX_STATIC_SHIFT = 30.0
QK_BOUND_MARGIN = 1.02

V7X_LANES = 128
V7X_VMEM_LIMIT_BYTES = 56 * 1024 * 1024
MATMUL_ROW_SUBTILE = 256
NA_ROW_GROUP = 32
ROUTE_WINDOW_TOKENS = 1024

NT_DIMS = (((1,), (1,)), ((), ()))


def _lambda_init(layer_idx):
    return DA_LAYER_LAMBDA_BASE - 0.6 * math.exp(-0.3 * layer_idx)


def _rms(y):
    return y * lax.rsqrt(jnp.mean(y * y, axis=-1, keepdims=True) + RMS_EPS)


def _qk_score_bound(q_gain, k_gain):
    return (QK_BOUND_MARGIN * math.sqrt(HEAD_DIM) * jnp.max(jnp.abs(q_gain.astype(F32)))
            * jnp.max(jnp.abs(k_gain.astype(F32))))


def _params(*semantics):
    return pltpu.CompilerParams(dimension_semantics=semantics,
                                vmem_limit_bytes=V7X_VMEM_LIMIT_BYTES)


def _inproj_kernel(x_ref, g1_ref, w_ref, gain_ref, o_ref, h_ref, *, tn, sub):
    j = pl.program_id(1)

    @pl.when(j == 0)
    def _():
        h_ref[...] = (_rms(x_ref[...]) * g1_ref[...]).astype(BF16)

    w = w_ref[...].astype(BF16)
    col0 = j * tn
    qk_cols = (col0 < 2 * NA_WIDTH) | ((col0 >= 3 * NA_WIDTH) & (col0 < 3 * NA_WIDTH + 4 * DA_QK))

    tm = h_ref.shape[0]
    for r in range(tm // sub):
        rows = slice(r * sub, (r + 1) * sub)
        acc = jnp.dot(h_ref[rows, :], w, preferred_element_type=F32)
        for c in range(tn // HEAD_DIM):
            sl = slice(c * HEAD_DIM, (c + 1) * HEAD_DIM)
            y = acc[:, sl]
            inv = lax.rsqrt(jnp.mean(y * y, axis=-1, keepdims=True) + RMS_EPS)
            o_ref[rows, sl] = (y * jnp.where(qk_cols, inv, 1.0) * gain_ref[:, sl]).astype(o_ref.dtype)


def _in_projection(x2d, ln1_g, w_in, col_gain, *, tm, tn):
    n, d = x2d.shape
    cols = w_in.shape[1]
    return pl.pallas_call(
        functools.partial(_inproj_kernel, tn=tn, sub=min(MATMUL_ROW_SUBTILE, tm)),
        grid=(n // tm, cols // tn),
        in_specs=[
            pl.BlockSpec((tm, d), lambda i, j: (i, 0)),
            pl.BlockSpec((1, d), lambda i, j: (0, 0)),
            pl.BlockSpec((d, tn), lambda i, j: (0, j)),
            pl.BlockSpec((1, tn), lambda i, j: (0, j)),
        ],
        out_specs=pl.BlockSpec((tm, tn), lambda i, j: (i, j)),
        out_shape=jax.ShapeDtypeStruct((n, cols), BF16),
        scratch_shapes=[pltpu.VMEM((tm, d), BF16)],
        compiler_params=_params("parallel", "arbitrary"),
        name="in_projection",
    )(x2d, ln1_g.reshape(1, d), w_in, col_gain)


def _na_bias_table(rpb, qn_a, kn_a):
    rpb = rpb.astype(F32)
    c = jnp.arange(GRID_W)
    col_off = jnp.clip(c[None, :] - c[:, None], -(NA_WIN_COLS - 1), NA_WIN_COLS - 1) + (NA_WIN_COLS - 1)
    onehot = (col_off[None] == jnp.arange(2 * NA_WIN_COLS - 1)[:, None, None]).astype(F32)
    t = jnp.einsum("hro,oqk->hrqk", rpb, onehot, precision=lax.Precision.HIGHEST)
    cs = jnp.clip(c - NA_WIN_COLS // 2, 0, GRID_W - NA_WIN_COLS)
    col_valid = (c[None, :] >= cs[:, None]) & (c[None, :] < cs[:, None] + NA_WIN_COLS)
    qk_bound = _qk_score_bound(qn_a, kn_a)
    static_shift = 2.0 * qk_bound + (jnp.max(rpb) - jnp.min(rpb)) <= 2.0 * MAX_STATIC_SHIFT
    shift = jnp.where(static_shift, qk_bound + jnp.max(rpb), 0.0)
    t = jnp.where(col_valid, (t - shift) * LOG2_E, -jnp.inf)
    pairs = jnp.concatenate([t[:, :-1], t[:, 1:]], axis=-1)
    return pairs, static_shift.astype(jnp.int32).reshape(1)


def _na_kernel(flag_ref, q_ref, k_ref, v_ref, bias_ref, o_ref, *, rows, group):
    win = NA_WIN_ROWS * GRID_W

    def make_body(static_shift):
        def body(it, carry):
            r = [it * group + g for g in range(group)]
            rs = [jnp.clip(ri - NA_WIN_ROWS // 2, 0, rows - NA_WIN_ROWS) for ri in r]
            q0 = [pl.multiple_of(ri * GRID_W, GRID_W) for ri in r]
            k0 = [pl.multiple_of(rsi * GRID_W, GRID_W) for rsi in rs]
            s = [lax.dot_general(q_ref[pl.ds(q0[g], GRID_W), :], k_ref[pl.ds(k0[g], win), :], NT_DIMS,
                                 preferred_element_type=F32) for g in range(group)]
            p, l = [], []
            for g in range(group):
                base = rs[g] - r[g] + (NA_WIN_ROWS - 1)
                bias = jnp.concatenate([bias_ref[base + 2 * jj] for jj in range(NA_WIN_ROWS // 2)], axis=1)
                z = s[g] * (ATTN_SCALE * LOG2_E) + bias
                if not static_shift:
                    z = z - jnp.max(z, axis=-1, keepdims=True)
                pg = jnp.exp2(z)
                l.append(jnp.sum(pg, axis=-1, keepdims=True))
                p.append(pg.astype(BF16))
            for g in range(group):
                o = jnp.dot(p[g], v_ref[pl.ds(k0[g], win), :], preferred_element_type=F32)
                o_ref[pl.ds(q0[g], GRID_W), :] = (o / l[g]).astype(o_ref.dtype)
            return carry
        return body

    @pl.when(flag_ref[0] == 1)
    def _():
        lax.fori_loop(0, rows // group, make_body(True), 0)

    @pl.when(flag_ref[0] != 1)
    def _():
        lax.fori_loop(0, rows // group, make_body(False), 0)


def _na_attention(proj3d, bias_pairs, static_shift):
    b, s, _ = proj3d.shape
    rows = s // GRID_W
    assert rows >= NA_WIN_ROWS
    qkv_spec = lambda off: pl.BlockSpec((None, s, HEAD_DIM), lambda bi, h, fl: (bi, 0, off + h))
    grid_spec = pltpu.PrefetchScalarGridSpec(
        num_scalar_prefetch=1,
        grid=(b, NA_HEADS),
        in_specs=[
            qkv_spec(0), qkv_spec(NA_HEADS), qkv_spec(2 * NA_HEADS),
            pl.BlockSpec((None,) + bias_pairs.shape[1:], lambda bi, h, fl: (h, 0, 0, 0)),
        ],
        out_specs=pl.BlockSpec((None, s, HEAD_DIM), lambda bi, h, fl: (bi, 0, h)),
    )
    return pl.pallas_call(
        functools.partial(_na_kernel, rows=rows, group=math.gcd(rows, NA_ROW_GROUP)),
        grid_spec=grid_spec,
        out_shape=jax.ShapeDtypeStruct((b, s, NA_WIDTH), BF16),
        compiler_params=_params("parallel", "parallel"),
        name="na_attention",
    )(static_shift, proj3d, proj3d, proj3d, bias_pairs)


def _da_kernel(par_ref, q1_ref, q2_ref, k1_ref, k2_ref, v_ref,
               lq1_ref, lk1_ref, lq2_ref, lk2_ref, sub_ref, o_ref, alibi_ref, z_ref, p_ref,
               *, lam_init, kc):
    h = pl.program_id(0)
    qi = pl.program_id(2)
    tq = q1_ref.shape[0]
    s_len = k1_ref.shape[0]
    lam = (jnp.exp(jnp.sum(lq1_ref[...] * lk1_ref[...], axis=-1, keepdims=True))
           - jnp.exp(jnp.sum(lq2_ref[...] * lk2_ref[...], axis=-1, keepdims=True))
           + lam_init)

    bound = par_ref[DA_HEADS]
    static_shift = bound <= MAX_STATIC_SHIFT

    @pl.when((pl.program_id(1) == 0) & (qi == 0))
    def _():
        i = lax.broadcasted_iota(jnp.int32, (tq, 1), 0).astype(F32)
        x = (lax.broadcasted_iota(jnp.int32, (1, alibi_ref.shape[1]), 1) - (s_len - tq)).astype(F32)
        alibi_ref[...] = (jnp.abs(x - i) * (-LOG2_E * par_ref[h])
                          - LOG2_E * jnp.where(static_shift, bound, 0.0))

    win0 = s_len - tq - qi * tq
    n_chunks = s_len // kc
    heads = ((q1_ref, k1_ref), (q2_ref, k2_ref))

    def scores(q, k_ref, c):
        s = lax.dot_general(q, k_ref[c * kc:(c + 1) * kc, :], NT_DIMS, preferred_element_type=F32)
        return s * (ATTN_SCALE * LOG2_E) + alibi_ref[:, pl.ds(pl.multiple_of(win0 + c * kc, V7X_LANES), kc)]

    def finish(o1, l1, o2, l2):
        o = o1 * (1.0 / l1) - o2 * (lam / l2)
        o_ref[...] = (_rms(o) * sub_ref[...] * (1.0 - lam_init)).astype(o_ref.dtype)

    @pl.when(static_shift)
    def _():
        outs = []
        for t, (q_ref, k_ref) in enumerate(heads):
            q = q_ref[...]
            l_lanes = jnp.zeros((tq, V7X_LANES), F32)
            for c in range(n_chunks):
                p = jnp.exp2(scores(q, k_ref, c))
                for j in range(kc // V7X_LANES):
                    l_lanes = l_lanes + p[:, j * V7X_LANES:(j + 1) * V7X_LANES]
                p_ref[t, :, c * kc:(c + 1) * kc] = p.astype(BF16)
            o = jnp.dot(p_ref[t], v_ref[...], preferred_element_type=F32)
            outs += [o, jnp.sum(l_lanes, axis=-1, keepdims=True)]
        finish(*outs)

    @pl.when(jnp.logical_not(static_shift))
    def _():
        outs = []
        for t, (q_ref, k_ref) in enumerate(heads):
            q = q_ref[...]
            m = jnp.full((tq, 1), -jnp.inf, F32)
            for c in range(n_chunks):
                z = scores(q, k_ref, c)
                z_ref[:, c * kc:(c + 1) * kc] = z
                m = jnp.maximum(m, jnp.max(z, axis=-1, keepdims=True))
            l = jnp.zeros((tq, 1), F32)
            for c in range(n_chunks):
                p = jnp.exp2(z_ref[:, c * kc:(c + 1) * kc] - m)
                l = l + jnp.sum(p, axis=-1, keepdims=True)
                p_ref[t, :, c * kc:(c + 1) * kc] = p.astype(BF16)
            outs += [jnp.dot(p_ref[t], v_ref[...], preferred_element_type=F32), l]
        finish(*outs)


def _da_attention(proj3d, slopes, qn_b, kn_b, lam_q1, lam_k1, lam_q2, lam_k2, subln, *, lam_init, tq):
    b, s, _ = proj3d.shape
    qk0 = 3 * NA_WIDTH // HEAD_DIM
    v0 = (3 * NA_WIDTH + 4 * DA_QK) // DA_VDIM
    q_spec = lambda off: pl.BlockSpec((None, tq, HEAD_DIM), lambda h, bi, qi, sl: (bi, qi, off + h))
    k_spec = lambda off: pl.BlockSpec((None, s, HEAD_DIM), lambda h, bi, qi, sl: (bi, 0, off + h))
    vec_spec = lambda n: pl.BlockSpec((1, n), lambda h, bi, qi, sl: (0, 0))
    kc = min(512, s)
    par = jnp.concatenate([slopes, _qk_score_bound(qn_b, kn_b).reshape(1)])
    grid_spec = pltpu.PrefetchScalarGridSpec(
        num_scalar_prefetch=1,
        grid=(DA_HEADS, b, s // tq),
        in_specs=[
            q_spec(qk0), q_spec(qk0 + DA_HEADS), k_spec(qk0 + 2 * DA_HEADS), k_spec(qk0 + 3 * DA_HEADS),
            pl.BlockSpec((None, s, DA_VDIM), lambda h, bi, qi, sl: (bi, 0, v0 + h)),
            vec_spec(HEAD_DIM), vec_spec(HEAD_DIM), vec_spec(HEAD_DIM), vec_spec(HEAD_DIM),
            vec_spec(DA_VDIM),
        ],
        out_specs=pl.BlockSpec((None, tq, DA_VDIM), lambda h, bi, qi, sl: (bi, qi, h)),
        scratch_shapes=[pltpu.VMEM((tq, 2 * s - tq), F32), pltpu.VMEM((tq, s), F32),
                        pltpu.VMEM((2, tq, s), BF16)],
    )
    row = lambda a: a.reshape(1, -1).astype(F32)
    return pl.pallas_call(
        functools.partial(_da_kernel, lam_init=lam_init, kc=kc),
        grid_spec=grid_spec,
        out_shape=jax.ShapeDtypeStruct((b, s, DA_WIDTH), BF16),
        compiler_params=_params("arbitrary", "arbitrary", "arbitrary"),
        name="da_attention",
    )(par, proj3d, proj3d, proj3d, proj3d, proj3d,
      row(lam_q1), row(lam_k1), row(lam_q2), row(lam_k2), row(subln))


def _outproj_kernel(oa_ref, ob_ref, x_ref, ona_ref, wa_ref, wb_ref, g2_ref, wr_ref,
                    x1_ref, lg_ref, *, sub):
    n_exp = lg_ref.shape[0]
    tm = x_ref.shape[0]
    for r in range(tm // sub):
        rows = slice(r * sub, (r + 1) * sub)
        oan = (_rms(oa_ref[rows, :].astype(F32)) * ona_ref[...]).astype(BF16)
        acc = jnp.dot(oan, wa_ref[...], preferred_element_type=F32)
        acc = acc + jnp.dot(ob_ref[rows, :], wb_ref[...], preferred_element_type=F32)
        x1 = x_ref[rows, :] + acc
        x1_ref[rows, :] = x1
        h2 = _rms(x1) * g2_ref[...]
        h_hi = h2.astype(BF16)
        h_lo = (h2 - h_hi.astype(F32)).astype(BF16)
        parts = (jnp.dot(h_hi, wr_ref[...], preferred_element_type=F32)
                 + jnp.dot(h_lo, wr_ref[...], preferred_element_type=F32))
        parts_t = parts.T
        lg_ref[:, rows] = parts_t[:n_exp] + parts_t[n_exp:2 * n_exp]


def _out_projection(oa2d, ob2d, x2d, on_a, w_out, ln2_g, w_router, *, tm):
    n, d = x2d.shape
    n_exp = w_router.shape[1]
    assert 2 * n_exp <= V7X_LANES
    wr_hi = w_router.astype(BF16)
    wr_lo = (w_router - wr_hi.astype(F32)).astype(BF16)
    wr_pieces = jnp.concatenate(
        [wr_hi, wr_lo, jnp.zeros((d, V7X_LANES - 2 * n_exp), BF16)], axis=1)
    assert DA_WIDTH == NA_WIDTH
    half_spec = lambda half: pl.BlockSpec((NA_WIDTH, d), lambda i: (half, 0))
    row_spec = lambda width: pl.BlockSpec((tm, width), lambda i: (i, 0))
    full_spec = lambda a: pl.BlockSpec(a.shape, lambda i: (0, 0))
    on_a = on_a.reshape(1, -1)
    ln2_g = ln2_g.reshape(1, -1)
    return pl.pallas_call(
        functools.partial(_outproj_kernel, sub=min(MATMUL_ROW_SUBTILE, tm)),
        grid=(n // tm,),
        in_specs=[row_spec(NA_WIDTH), row_spec(DA_WIDTH), row_spec(d), full_spec(on_a),
                  half_spec(0), half_spec(1), full_spec(ln2_g), full_spec(wr_pieces)],
        out_specs=[row_spec(d), pl.BlockSpec((n_exp, tm), lambda i: (0, i))],
        out_shape=[jax.ShapeDtypeStruct((n, d), F32), jax.ShapeDtypeStruct((n_exp, n), F32)],
        compiler_params=_params("parallel"),
        name="out_projection",
    )(oa2d, ob2d, x2d, on_a, w_out, w_out, ln2_g, wr_pieces)


def _exclusive_prefix(mask):
    rows, s_len = mask.shape
    m = jnp.where(mask, 1.0, 0.0).astype(BF16)
    r = lax.broadcasted_iota(jnp.int32, (V7X_LANES, V7X_LANES), 0)
    c = lax.broadcasted_iota(jnp.int32, (V7X_LANES, V7X_LANES), 1)
    before = jnp.where(r < c, 1.0, 0.0).astype(BF16)
    ones = jnp.ones((V7X_LANES, V7X_LANES), BF16)
    run = jnp.zeros((rows, V7X_LANES), F32)
    out = []
    for blk in range(s_len // V7X_LANES):
        mb = m[:, blk * V7X_LANES:(blk + 1) * V7X_LANES]
        out.append(run + jnp.dot(mb, before, preferred_element_type=F32))
        run = run + jnp.dot(mb, ones, preferred_element_type=F32)
    return jnp.concatenate(out, axis=1)


def _route_kernel(lg_ref, idx_ref, gate_ref, posm_ref, boff_ref, aff_ref, idx_acc, gate_acc,
                  boff_vmem, boff_smem, sem_ref, *, cap, slot_tile, tb):
    lg = lg_ref[...]
    n_exp, s_len = lg.shape
    ex = jnp.exp(lg - jnp.max(lg, axis=0, keepdims=True))
    aff_ref[...] = ex / jnp.sum(ex, axis=0, keepdims=True)
    aff = aff_ref[...]

    def enough(values, t):
        return jnp.sum(jnp.where(values >= t, 1.0, 0.0), axis=1, keepdims=True) >= cap

    thr = jnp.full((n_exp, 1), 2.0 ** -126, F32)
    for shift in (64, 32, 16, 8, 4, 2, 1):
        cand = thr * (2.0 ** shift)
        thr = jnp.where(enough(aff, cand), cand, thr)
    step = thr
    for _ in range(23):
        step = step * 0.5
        cand = thr + step
        thr = jnp.where(enough(aff, cand), cand, thr)
    thr = jnp.where(enough(aff, thr), thr, 0.0)
    res = aff - thr
    rho = jnp.zeros((n_exp, 1), F32)
    for _ in range(24):
        step = step * 0.5
        cand = rho + step
        rho = jnp.where(enough(res, cand), cand, rho)

    gt = res > rho
    eq = res == rho
    need = cap - jnp.sum(jnp.where(gt, 1.0, 0.0), axis=1, keepdims=True)
    sel = gt | (eq & (_exclusive_prefix(eq) < need))
    posm_ref[...] = jnp.where(sel, _exclusive_prefix(sel), -1.0)

    t_idx = lax.broadcasted_iota(jnp.int32, (s_len, V7X_LANES), 0)
    j_idx = lax.broadcasted_iota(jnp.int32, (s_len, V7X_LANES), 1)
    before_block = jnp.where(t_idx < j_idx * tb, 1.0, 0.0).astype(BF16)
    boff = jnp.dot(jnp.where(sel, 1.0, 0.0).astype(BF16), before_block, preferred_element_type=F32)
    boff_ref[...] = boff.astype(jnp.int32)
    boff_vmem[...] = boff.astype(jnp.int32)
    to_smem = pltpu.make_async_copy(boff_vmem, boff_smem, sem_ref.at[0])
    to_smem.start()
    to_smem.wait()

    n_blk = s_len // tb
    n_tiles = cap // slot_tile
    span = min(s_len, max(tb, ROUTE_WINDOW_TOKENS))
    lane_iota = lax.broadcasted_iota(jnp.int32, (1, span), 1)

    def per_expert(e, carry):
        offs = [boff_smem[e, j] for j in range(n_blk + 1)]
        first_tok, n_rounds = [], 0
        for st in range(n_tiles):
            s0 = st * slot_tile
            j_lo = sum((offs[j + 1] <= s0).astype(jnp.int32) for j in range(n_blk))
            j_hi = sum((offs[j] < s0 + slot_tile).astype(jnp.int32) for j in range(n_blk))
            first_tok.append(j_lo * tb)
            n_rounds = jnp.maximum(n_rounds, ((j_hi - j_lo) * tb + span - 1) // span)
        idx_acc[...] = jnp.zeros_like(idx_acc)
        gate_acc[...] = jnp.zeros_like(gate_acc)

        def one_round(r, c2):
            for st in range(n_tiles):
                nominal = first_tok[st] + r * span
                start = pl.multiple_of(jnp.minimum(nominal, s_len - span), V7X_LANES)
                tok_w = (start + lane_iota).astype(F32)
                pos_w = posm_ref[pl.ds(e, 1), pl.ds(start, span)]
                pos_w = jnp.where(tok_w >= nominal.astype(F32), pos_w, -1.0)
                aff_w = aff_ref[pl.ds(e, 1), pl.ds(start, span)]
                slots = (st * slot_tile + lax.broadcasted_iota(jnp.int32, (slot_tile, 1), 0)).astype(F32)
                hit = pos_w == slots
                rows = pl.ds(st * slot_tile, slot_tile)
                idx_acc[rows, :] += jnp.sum(jnp.where(hit, tok_w, 0.0), axis=1, keepdims=True)
                gate_acc[rows, :] += jnp.sum(jnp.where(hit, aff_w, 0.0), axis=1, keepdims=True)
            return c2

        lax.fori_loop(0, n_rounds, one_round, 0)
        idx_ref[e] = idx_acc[...].astype(jnp.int32)
        gate_ref[e] = gate_acc[...]
        return carry

    lax.fori_loop(0, n_exp, per_expert, 0)


def _ec_route(logits_t, batch, *, cap, slot_tile, tb):
    n_exp, n = logits_t.shape
    s = n // batch
    assert s // tb + 1 <= V7X_LANES
    slot_spec = pl.BlockSpec((n_exp, None, cap, 1), lambda b: (0, b, 0, 0))
    return pl.pallas_call(
        functools.partial(_route_kernel, cap=cap, slot_tile=slot_tile, tb=tb),
        grid=(batch,),
        in_specs=[pl.BlockSpec((n_exp, s), lambda b: (0, b))],
        out_specs=[slot_spec, slot_spec,
                   pl.BlockSpec((None, n_exp, s), lambda b: (b, 0, 0)),
                   pl.BlockSpec((None, n_exp, V7X_LANES), lambda b: (b, 0, 0))],
        out_shape=[jax.ShapeDtypeStruct((n_exp, batch, cap, 1), jnp.int32),
                   jax.ShapeDtypeStruct((n_exp, batch, cap, 1), F32),
                   jax.ShapeDtypeStruct((batch, n_exp, s), F32),
                   jax.ShapeDtypeStruct((batch, n_exp, V7X_LANES), jnp.int32)],
        scratch_shapes=[pltpu.VMEM((n_exp, s), F32), pltpu.VMEM((cap, 1), F32), pltpu.VMEM((cap, 1), F32),
                        pltpu.VMEM((n_exp, V7X_LANES), jnp.int32),
                        pltpu.SMEM((n_exp, V7X_LANES), jnp.int32), pltpu.SemaphoreType.DMA((1,))],
        compiler_params=_params("parallel"),
        name="ec_route",
    )(logits_t)


def _ffn_kernel(rows_ref, x1_hbm, g2_ref, gate_ref, wg_ref, wu_ref, wd_ref, y_ref,
                xs_ref, xsb_ref, acc_ref, sem_ref, *, m_rows, chunk, n_total):
    e = pl.program_id(0)
    f = pl.program_id(1)
    n_exp = pl.num_programs(0)
    n_f = pl.num_programs(1)

    def row_copy(expert, i):
        src = rows_ref[expert * m_rows + jnp.minimum(i, m_rows - 1)]
        return pltpu.make_async_copy(x1_hbm.at[pl.ds(src, 1), :], xs_ref.at[pl.ds(i, 1), :],
                                     sem_ref.at[0])

    def wait_rows(count):
        pltpu.make_async_copy(x1_hbm.at[pl.ds(0, count), :], xs_ref.at[pl.ds(0, count), :],
                              sem_ref.at[0]).wait()

    def normalise_rows():
        xsb_ref[...] = (_rms(xs_ref[pl.ds(0, m_rows), :]) * g2_ref[...]).astype(BF16)

    @pl.when((e == 0) & (f == 0))
    def _():
        def group(ig, carry):
            for u in range(8):
                row_copy(0, ig * 8 + u).start()
            return carry
        lax.fori_loop(0, m_rows // 8, group, 0)
        wait_rows(m_rows)
        normalise_rows()

    @pl.when(f == 0)
    def _():
        acc_ref[...] = jnp.zeros_like(acc_ref)

    nxt = jnp.minimum(e + 1, n_exp - 1)
    for u in range(chunk):
        row_copy(nxt, f * chunk + u).start()

    xs = xsb_ref[...]
    g = jnp.dot(xs, wg_ref[...].astype(BF16), preferred_element_type=F32)
    u = jnp.dot(xs, wu_ref[...].astype(BF16), preferred_element_type=F32)
    a = (g * jax.nn.sigmoid(g) * u).astype(BF16)
    acc_ref[...] += jnp.dot(a, wd_ref[...].astype(BF16), preferred_element_type=F32)

    @pl.when(f == n_f - 1)
    def _():
        y_ref[...] = (acc_ref[...] * gate_ref[...]).astype(y_ref.dtype)
        wait_rows(n_total)
        normalise_rows()


def _ec_ffn(rows_flat, x1, ln2_g, gate, w_gate, w_up, w_down, *, tf):
    n, d = x1.shape
    n_exp, _, ff = w_gate.shape
    m_rows = rows_flat.shape[0] // n_exp
    n_f = ff // tf
    assert m_rows % 8 == 0
    chunk = -(-m_rows // (8 * n_f)) * 8
    n_total = chunk * n_f
    assert n_total - m_rows < m_rows
    grid_spec = pltpu.PrefetchScalarGridSpec(
        num_scalar_prefetch=1,
        grid=(n_exp, n_f),
        in_specs=[
            pl.BlockSpec(memory_space=pl.ANY),
            pl.BlockSpec((1, d), lambda e, f, r: (0, 0)),
            pl.BlockSpec((None, m_rows, 1), lambda e, f, r: (e, 0, 0)),
            pl.BlockSpec((None, d, tf), lambda e, f, r: (e, 0, f)),
            pl.BlockSpec((None, d, tf), lambda e, f, r: (e, 0, f)),
            pl.BlockSpec((None, tf, d), lambda e, f, r: (e, f, 0)),
        ],
        out_specs=pl.BlockSpec((None, m_rows, d), lambda e, f, r: (e, 0, 0)),
        scratch_shapes=[pltpu.VMEM((n_total, d), F32), pltpu.VMEM((m_rows, d), BF16),
                        pltpu.VMEM((m_rows, d), F32), pltpu.SemaphoreType.DMA((1,))],
    )
    return pl.pallas_call(
        functools.partial(_ffn_kernel, m_rows=m_rows, chunk=chunk, n_total=n_total),
        grid_spec=grid_spec,
        out_shape=jax.ShapeDtypeStruct((n_exp, m_rows, d), BF16),
        compiler_params=_params("arbitrary", "arbitrary"),
        name="ec_ffn",
    )(rows_flat, x1, ln2_g.reshape(1, d), gate, w_gate, w_up, w_down)


def _combine_kernel(boff_ref, x1_ref, posm_ref, y_hbm, o_ref, ywin_ref, sem_ref,
                    *, batch, cap, win, adv):
    n_exp, tb = posm_ref.shape
    i = pl.program_id(0)
    n_steps = pl.num_programs(0)
    nblk = n_steps // batch
    buf = i % 2

    def block_offsets(step, shift):
        bb = step // nblk
        return bb, [boff_ref[(bb * n_exp + e) * V7X_LANES + step % nblk + shift] for e in range(n_exp)]

    def window_copies(bb, lowers, to_buf):
        copies, starts = [], []
        for e in range(n_exp):
            start = jnp.minimum((lowers[e] // 16) * 16, cap - win)
            row0 = pl.multiple_of((e * batch + bb) * cap + start, 16)
            copies.append(pltpu.make_async_copy(y_hbm.at[pl.ds(row0, win), :],
                                                ywin_ref.at[to_buf, pl.ds(e * win, win), :],
                                                sem_ref.at[to_buf]))
            starts.append(start)
        return copies, starts

    b, offs = block_offsets(i, 0)
    _, nxts = block_offsets(i, 1)
    most = functools.reduce(jnp.maximum, [nx - of for nx, of in zip(nxts, offs)])

    @pl.when(i == 0)
    def _():
        for cp in window_copies(b, offs, buf)[0]:
            cp.start()

    @pl.when(i + 1 < n_steps)
    def _():
        b_next, offs_next = block_offsets(i + 1, 0)
        for cp in window_copies(b_next, offs_next, 1 - buf)[0]:
            cp.start()

    o_ref[...] = x1_ref[...]
    posm = posm_ref[...]
    prow = jnp.concatenate([jnp.broadcast_to(posm[e:e + 1, :], (win, tb)) for e in range(n_exp)], axis=0)

    def scatter_round(lowers, fetch):
        copies, starts = window_copies(b, lowers, buf)
        if fetch:
            for cp in copies:
                cp.start()
        cols = []
        for e in range(n_exp):
            slot = starts[e] + lax.broadcasted_iota(jnp.int32, (win, 1), 0)
            cols.append(jnp.where((slot >= lowers[e]) & (slot < lowers[e] + adv), slot, -2).astype(F32))
        hit_t = jnp.where(prow == jnp.concatenate(cols, axis=0), 1.0, 0.0).astype(BF16)
        for cp in copies:
            cp.wait()
        o_ref[...] += lax.dot_general(hit_t, ywin_ref[buf], (((0,), (0,)), ((), ())),
                                      preferred_element_type=F32)

    scatter_round(offs, fetch=False)

    def extra_round(k, carry):
        scatter_round([of + k * adv for of in offs], fetch=True)
        return carry

    lax.fori_loop(1, (most + adv - 1) // adv, extra_round, 0)


def _ec_combine(boff_flat, x1, posm, y2d, *, cap, tb, win, adv):
    n, d = x1.shape
    batch, n_exp, s = posm.shape
    nblk = s // tb
    assert cap % 16 == 0 and win % 16 == 0 and win >= adv + 16 and cap >= win
    grid_spec = pltpu.PrefetchScalarGridSpec(
        num_scalar_prefetch=1,
        grid=(n // tb,),
        in_specs=[
            pl.BlockSpec((tb, d), lambda i, bo: (i, 0)),
            pl.BlockSpec((None, n_exp, tb), lambda i, bo: (i // nblk, 0, i % nblk)),
            pl.BlockSpec(memory_space=pl.ANY),
        ],
        out_specs=pl.BlockSpec((tb, d), lambda i, bo: (i, 0)),
        scratch_shapes=[pltpu.VMEM((2, n_exp * win, d), BF16), pltpu.SemaphoreType.DMA((2,))],
    )
    return pl.pallas_call(
        functools.partial(_combine_kernel, batch=batch, cap=cap, win=win, adv=adv),
        grid_spec=grid_spec,
        out_shape=jax.ShapeDtypeStruct((n, d), F32),
        compiler_params=_params("arbitrary"),
        name="ec_combine",
    )(boff_flat, x1, posm, y2d)


COMBINE_TOKENS = 256
COMBINE_WINDOW = 80
COMBINE_ADVANCE = 64
ROUTE_SLOT_TILE = 64


def _tiles(n, d, ff):
    pick = lambda total, want: want if total % want == 0 else total
    return dict(tm_in=pick(n, 1024), tn_in=1024, tq=pick(n, 512), tm_out=pick(n, 512),
                tf=pick(ff, 256))


def kernel(x, ln1_g, w_in, qn_a, kn_a, rpb_a, on_a, qn_b, kn_b, lam_q1, lam_k1, lam_q2, lam_k2,
           subln_b, w_out, ln2_g, w_router, w_gate, w_up, w_down):
    b, s, d = x.shape
    n = b * s
    depth = w_in.shape[0]
    n_exp = w_router.shape[-1]
    cap = EC_CAPACITY_FACTOR * s // n_exp
    t = _tiles(n, d, w_gate.shape[-1])
    slopes = 2.0 ** (-8.0 * jnp.arange(1, DA_HEADS + 1, dtype=F32) / DA_HEADS)
    ones_v = jnp.ones((NA_WIDTH,), F32)
    x2d = x.reshape(n, d)
    for l in range(depth):
        tile_heads = lambda g, heads: jnp.tile(g.astype(F32), heads)
        col_gain = jnp.concatenate([
            tile_heads(qn_a[l], NA_HEADS), tile_heads(kn_a[l], NA_HEADS), ones_v,
            tile_heads(qn_b[l], 2 * DA_HEADS), tile_heads(kn_b[l], 2 * DA_HEADS),
            jnp.ones((DA_WIDTH,), F32)]).reshape(1, IN_COLS)
        proj = _in_projection(x2d, ln1_g[l], w_in[l], col_gain,
                              tm=t["tm_in"], tn=t["tn_in"])
        proj3d = proj.reshape(b, s, IN_COLS)
        oa = _na_attention(proj3d, *_na_bias_table(rpb_a[l], qn_a[l], kn_a[l]))
        ob = _da_attention(proj3d, slopes, qn_b[l], kn_b[l], lam_q1[l], lam_k1[l], lam_q2[l], lam_k2[l], subln_b[l],
                           lam_init=_lambda_init(l), tq=min(t["tq"], s))
        x1, logits_t = _out_projection(
            oa.reshape(n, NA_WIDTH), ob.reshape(n, DA_WIDTH), x2d, on_a[l],
            w_out[l].astype(BF16), ln2_g[l], w_router[l].astype(F32), tm=t["tm_out"])
        idx, gate, posm, boff = _ec_route(logits_t, b, cap=cap, slot_tile=ROUTE_SLOT_TILE, tb=COMBINE_TOKENS)
        rows_flat = (idx[..., 0] + (jnp.arange(b, dtype=jnp.int32) * s)[None, :, None]).reshape(-1)
        gate_e = gate.reshape(n_exp, b * cap, 1)
        y = _ec_ffn(rows_flat, x1, ln2_g[l], gate_e, w_gate[l], w_up[l], w_down[l], tf=t["tf"])
        x2d = _ec_combine(boff.reshape(-1), x1, posm, y.reshape(n_exp * b * cap, d), cap=cap,
                          tb=COMBINE_TOKENS, win=COMBINE_WINDOW, adv=COMBINE_ADVANCE)
    return x2d.reshape(b, s, d)
```

```python
import functools
import math

import jax
import jax.numpy as jnp
from jax import lax
from jax.experimental import pallas as pl
from jax.experimental.pallas import tpu as pltpu

F32 = jnp.float32
BF16 = jnp.bfloat16

GRID_W = 64
HEAD_DIM = 128
NA_HEADS = 8
NA_WIDTH = NA_HEADS * HEAD_DIM
NA_WIN_ROWS = 8
NA_WIN_COLS = 16
DA_HEADS = 4
DA_VDIM = 2 * HEAD_DIM
DA_WIDTH = DA_HEADS * DA_VDIM
DA_QK = DA_HEADS * HEAD_DIM
DA_LAYER_LAMBDA_BASE = 0.8
IN_COLS = 3 * NA_WIDTH + 4 * DA_QK + DA_WIDTH
N_EXPERTS = 16
EC_CAPACITY_FACTOR = 2
RMS_EPS = 1e-6
ATTN_SCALE = HEAD_DIM ** -0.5
LOG2_E = math.log2(math.e)
MAX_STATIC_SHIFT = 30.0
QK_BOUND_MARGIN = 1.02

V7X_LANES = 128
V7X_VMEM_LIMIT_BYTES = 56 * 1024 * 1024
MATMUL_ROW_SUBTILE = 256
NA_ROW_GROUP = 32
ROUTE_WINDOW_TOKENS = 1024
WEIGHT_RING = 3

NT_DIMS = (((1,), (1,)), ((), ()))


def _lambda_init(layer_idx):
    return DA_LAYER_LAMBDA_BASE - 0.6 * math.exp(-0.3 * layer_idx)


def _rms(y):
    return y * lax.rsqrt(jnp.mean(y * y, axis=-1, keepdims=True) + RMS_EPS)


def _qk_score_bound(q_gain, k_gain):
    return (QK_BOUND_MARGIN * math.sqrt(HEAD_DIM) * jnp.max(jnp.abs(q_gain.astype(F32)))
            * jnp.max(jnp.abs(k_gain.astype(F32))))


def _params(*semantics):
    return pltpu.CompilerParams(dimension_semantics=semantics,
                                vmem_limit_bytes=V7X_VMEM_LIMIT_BYTES)


def _inproj_kernel(x_ref, g1_ref, w_ref, gain_ref, o_ref, h_ref, *, tn, sub):
    j = pl.program_id(1)

    @pl.when(j == 0)
    def _():
        h_ref[...] = (_rms(x_ref[...]) * g1_ref[...]).astype(BF16)

    w = w_ref[...].astype(BF16)
    col0 = j * tn
    qk_cols = (col0 < 2 * NA_WIDTH) | ((col0 >= 3 * NA_WIDTH) & (col0 < 3 * NA_WIDTH + 4 * DA_QK))

    tm = h_ref.shape[0]
    for r in range(tm // sub):
        rows = slice(r * sub, (r + 1) * sub)
        acc = jnp.dot(h_ref[rows, :], w, preferred_element_type=F32)
        for c in range(tn // HEAD_DIM):
            sl = slice(c * HEAD_DIM, (c + 1) * HEAD_DIM)
            y = acc[:, sl]
            inv = lax.rsqrt(jnp.mean(y * y, axis=-1, keepdims=True) + RMS_EPS)
            o_ref[rows, sl] = (y * jnp.where(qk_cols, inv, 1.0) * gain_ref[:, sl]).astype(o_ref.dtype)


def _in_projection(x2d, ln1_g, w_in, col_gain, *, tm, tn):
    n, d = x2d.shape
    cols = w_in.shape[1]
    return pl.pallas_call(
        functools.partial(_inproj_kernel, tn=tn, sub=min(MATMUL_ROW_SUBTILE, tm)),
        grid=(n // tm, cols // tn),
        in_specs=[
            pl.BlockSpec((tm, d), lambda i, j: (i, 0)),
            pl.BlockSpec((1, d), lambda i, j: (0, 0)),
            pl.BlockSpec((d, tn), lambda i, j: (0, j)),
            pl.BlockSpec((1, tn), lambda i, j: (0, j)),
        ],
        out_specs=pl.BlockSpec((tm, tn), lambda i, j: (i, j)),
        out_shape=jax.ShapeDtypeStruct((n, cols), BF16),
        scratch_shapes=[pltpu.VMEM((tm, d), BF16)],
        compiler_params=_params("parallel", "arbitrary"),
        name="in_projection",
    )(x2d, ln1_g.reshape(1, d), w_in, col_gain)


def _na_bias_table(rpb, qn_a, kn_a):
    rpb = rpb.astype(F32)
    c = jnp.arange(GRID_W)
    col_off = jnp.clip(c[None, :] - c[:, None], -(NA_WIN_COLS - 1), NA_WIN_COLS - 1) + (NA_WIN_COLS - 1)
    onehot = (col_off[None] == jnp.arange(2 * NA_WIN_COLS - 1)[:, None, None]).astype(F32)
    t = jnp.einsum("hro,oqk->hrqk", rpb, onehot, precision=lax.Precision.HIGHEST)
    cs = jnp.clip(c - NA_WIN_COLS // 2, 0, GRID_W - NA_WIN_COLS)
    col_valid = (c[None, :] >= cs[:, None]) & (c[None, :] < cs[:, None] + NA_WIN_COLS)
    qk_bound = _qk_score_bound(qn_a, kn_a)
    static_shift = 2.0 * qk_bound + (jnp.max(rpb) - jnp.min(rpb)) <= 2.0 * MAX_STATIC_SHIFT
    shift = jnp.where(static_shift, qk_bound + jnp.max(rpb), 0.0)
    t = jnp.where(col_valid, (t - shift) * LOG2_E, -jnp.inf)
    pairs = jnp.concatenate([t[:, :-1], t[:, 1:]], axis=-1)
    return pairs, static_shift.astype(jnp.int32).reshape(1)


def _na_kernel(flag_ref, q_ref, k_ref, v_ref, bias_ref, o_ref, *, rows, group):
    win = NA_WIN_ROWS * GRID_W

    def make_body(static_shift):
        def body(it, carry):
            r = [it * group + g for g in range(group)]
            rs = [jnp.clip(ri - NA_WIN_ROWS // 2, 0, rows - NA_WIN_ROWS) for ri in r]
            q0 = [pl.multiple_of(ri * GRID_W, GRID_W) for ri in r]
            k0 = [pl.multiple_of(rsi * GRID_W, GRID_W) for rsi in rs]
            s = [lax.dot_general(q_ref[pl.ds(q0[g], GRID_W), :], k_ref[pl.ds(k0[g], win), :], NT_DIMS,
                                 preferred_element_type=F32) for g in range(group)]
            p, l = [], []
            for g in range(group):
                base = rs[g] - r[g] + (NA_WIN_ROWS - 1)
                bias = jnp.concatenate([bias_ref[base + 2 * jj] for jj in range(NA_WIN_ROWS // 2)], axis=1)
                z = s[g] * (ATTN_SCALE * LOG2_E) + bias
                if not static_shift:
                    z = z - jnp.max(z, axis=-1, keepdims=True)
                pg = jnp.exp2(z)
                l.append(jnp.sum(pg, axis=-1, keepdims=True))
                p.append(pg.astype(BF16))
            for g in range(group):
                o = jnp.dot(p[g], v_ref[pl.ds(k0[g], win), :], preferred_element_type=F32)
                o_ref[pl.ds(q0[g], GRID_W), :] = (o / l[g]).astype(o_ref.dtype)
            return carry
        return body

    @pl.when(flag_ref[0] == 1)
    def _():
        lax.fori_loop(0, rows // group, make_body(True), 0)

    @pl.when(flag_ref[0] != 1)
    def _():
        lax.fori_loop(0, rows // group, make_body(False), 0)


def _na_attention(proj3d, bias_pairs, static_shift):
    b, s, _ = proj3d.shape
    rows = s // GRID_W
    assert rows >= NA_WIN_ROWS
    qkv_spec = lambda off: pl.BlockSpec((None, s, HEAD_DIM), lambda bi, h, fl: (bi, 0, off + h))
    grid_spec = pltpu.PrefetchScalarGridSpec(
        num_scalar_prefetch=1,
        grid=(b, NA_HEADS),
        in_specs=[
            qkv_spec(0), qkv_spec(NA_HEADS), qkv_spec(2 * NA_HEADS),
            pl.BlockSpec((None,) + bias_pairs.shape[1:], lambda bi, h, fl: (h, 0, 0, 0)),
        ],
        out_specs=pl.BlockSpec((None, s, HEAD_DIM), lambda bi, h, fl: (bi, 0, h)),
    )
    return pl.pallas_call(
        functools.partial(_na_kernel, rows=rows, group=math.gcd(rows, NA_ROW_GROUP)),
        grid_spec=grid_spec,
        out_shape=jax.ShapeDtypeStruct((b, s, NA_WIDTH), BF16),
        compiler_params=_params("parallel", "parallel"),
        name="na_attention",
    )(static_shift, proj3d, proj3d, proj3d, bias_pairs)


def _da_kernel(par_ref, q1_ref, q2_ref, k1_ref, k2_ref, v_ref,
               lq1_ref, lk1_ref, lq2_ref, lk2_ref, sub_ref, o_ref, alibi_ref, z_ref, p_ref,
               *, lam_init, kc):
    h = pl.program_id(0)
    qi = pl.program_id(2)
    tq = q1_ref.shape[0]
    s_len = k1_ref.shape[0]
    lam = (jnp.exp(jnp.sum(lq1_ref[...] * lk1_ref[...], axis=-1, keepdims=True))
           - jnp.exp(jnp.sum(lq2_ref[...] * lk2_ref[...], axis=-1, keepdims=True))
           + lam_init)

    bound = par_ref[DA_HEADS]
    static_shift = bound <= MAX_STATIC_SHIFT

    @pl.when((pl.program_id(1) == 0) & (qi == 0))
    def _():
        i = lax.broadcasted_iota(jnp.int32, (tq, 1), 0).astype(F32)
        x = (lax.broadcasted_iota(jnp.int32, (1, alibi_ref.shape[1]), 1) - (s_len - tq)).astype(F32)
        alibi_ref[...] = (jnp.abs(x - i) * (-LOG2_E * par_ref[h])
                          - LOG2_E * jnp.where(static_shift, bound, 0.0))

    win0 = s_len - tq - qi * tq
    n_chunks = s_len // kc
    heads = ((q1_ref, k1_ref), (q2_ref, k2_ref))

    def scores(q, k_ref, c):
        s = lax.dot_general(q, k_ref[c * kc:(c + 1) * kc, :], NT_DIMS, preferred_element_type=F32)
        return s * (ATTN_SCALE * LOG2_E) + alibi_ref[:, pl.ds(pl.multiple_of(win0 + c * kc, V7X_LANES), kc)]

    def finish(o1, l1, o2, l2):
        o = o1 * (1.0 / l1) - o2 * (lam / l2)
        o_ref[...] = (_rms(o) * sub_ref[...] * (1.0 - lam_init)).astype(o_ref.dtype)

    @pl.when(static_shift)
    def _():
        outs = []
        for t, (q_ref, k_ref) in enumerate(heads):
            q = q_ref[...]
            l_lanes = jnp.zeros((tq, V7X_LANES), F32)
            for c in range(n_chunks):
                p = jnp.exp2(scores(q, k_ref, c))
                for j in range(kc // V7X_LANES):
                    l_lanes = l_lanes + p[:, j * V7X_LANES:(j + 1) * V7X_LANES]
                p_ref[t, :, c * kc:(c + 1) * kc] = p.astype(BF16)
            o = jnp.dot(p_ref[t], v_ref[...], preferred_element_type=F32)
            outs += [o, jnp.sum(l_lanes, axis=-1, keepdims=True)]
        finish(*outs)

    @pl.when(jnp.logical_not(static_shift))
    def _():
        outs = []
        for t, (q_ref, k_ref) in enumerate(heads):
            q = q_ref[...]
            m = jnp.full((tq, 1), -jnp.inf, F32)
            for c in range(n_chunks):
                z = scores(q, k_ref, c)
                z_ref[:, c * kc:(c + 1) * kc] = z
                m = jnp.maximum(m, jnp.max(z, axis=-1, keepdims=True))
            l = jnp.zeros((tq, 1), F32)
            for c in range(n_chunks):
                p = jnp.exp2(z_ref[:, c * kc:(c + 1) * kc] - m)
                l = l + jnp.sum(p, axis=-1, keepdims=True)
                p_ref[t, :, c * kc:(c + 1) * kc] = p.astype(BF16)
            outs += [jnp.dot(p_ref[t], v_ref[...], preferred_element_type=F32), l]
        finish(*outs)


def _da_attention(proj3d, slopes, qn_b, kn_b, lam_q1, lam_k1, lam_q2, lam_k2, subln, *, lam_init, tq):
    b, s, _ = proj3d.shape
    qk0 = 3 * NA_WIDTH // HEAD_DIM
    v0 = (3 * NA_WIDTH + 4 * DA_QK) // DA_VDIM
    q_spec = lambda off: pl.BlockSpec((None, tq, HEAD_DIM), lambda h, bi, qi, sl: (bi, qi, off + h))
    k_spec = lambda off: pl.BlockSpec((None, s, HEAD_DIM), lambda h, bi, qi, sl: (bi, 0, off + h))
    vec_spec = lambda n: pl.BlockSpec((1, n), lambda h, bi, qi, sl: (0, 0))
    kc = min(512, s)
    par = jnp.concatenate([slopes, _qk_score_bound(qn_b, kn_b).reshape(1)])
    grid_spec = pltpu.PrefetchScalarGridSpec(
        num_scalar_prefetch=1,
        grid=(DA_HEADS, b, s // tq),
        in_specs=[
            q_spec(qk0), q_spec(qk0 + DA_HEADS), k_spec(qk0 + 2 * DA_HEADS), k_spec(qk0 + 3 * DA_HEADS),
            pl.BlockSpec((None, s, DA_VDIM), lambda h, bi, qi, sl: (bi, 0, v0 + h)),
            vec_spec(HEAD_DIM), vec_spec(HEAD_DIM), vec_spec(HEAD_DIM), vec_spec(HEAD_DIM),
            vec_spec(DA_VDIM),
        ],
        out_specs=pl.BlockSpec((None, tq, DA_VDIM), lambda h, bi, qi, sl: (bi, qi, h)),
        scratch_shapes=[pltpu.VMEM((tq, 2 * s - tq), F32), pltpu.VMEM((tq, s), F32),
                        pltpu.VMEM((2, tq, s), BF16)],
    )
    row = lambda a: a.reshape(1, -1).astype(F32)
    return pl.pallas_call(
        functools.partial(_da_kernel, lam_init=lam_init, kc=kc),
        grid_spec=grid_spec,
        out_shape=jax.ShapeDtypeStruct((b, s, DA_WIDTH), BF16),
        compiler_params=_params("arbitrary", "arbitrary", "arbitrary"),
        name="da_attention",
    )(par, proj3d, proj3d, proj3d, proj3d, proj3d,
      row(lam_q1), row(lam_k1), row(lam_q2), row(lam_k2), row(subln))


def _outproj_kernel(oa_ref, ob_ref, x_ref, ona_ref, wa_ref, wb_ref, g2_ref, wr_ref,
                    x1_ref, lg_ref, *, sub):
    n_exp = lg_ref.shape[0]
    tm = x_ref.shape[0]
    for r in range(tm // sub):
        rows = slice(r * sub, (r + 1) * sub)
        oan = (_rms(oa_ref[rows, :].astype(F32)) * ona_ref[...]).astype(BF16)
        acc = jnp.dot(oan, wa_ref[...], preferred_element_type=F32)
        acc = acc + jnp.dot(ob_ref[rows, :], wb_ref[...], preferred_element_type=F32)
        x1 = x_ref[rows, :] + acc
        x1_ref[rows, :] = x1
        h2 = _rms(x1) * g2_ref[...]
        h_hi = h2.astype(BF16)
        h_lo = (h2 - h_hi.astype(F32)).astype(BF16)
        parts = (jnp.dot(h_hi, wr_ref[...], preferred_element_type=F32)
                 + jnp.dot(h_lo, wr_ref[...], preferred_element_type=F32))
        parts_t = parts.T
        lg_ref[:, rows] = parts_t[:n_exp] + parts_t[n_exp:2 * n_exp]


def _out_projection(oa2d, ob2d, x2d, on_a, w_out, ln2_g, w_router, *, tm):
    n, d = x2d.shape
    n_exp = w_router.shape[1]
    assert 2 * n_exp <= V7X_LANES
    wr_hi = w_router.astype(BF16)
    wr_lo = (w_router - wr_hi.astype(F32)).astype(BF16)
    wr_pieces = jnp.concatenate(
        [wr_hi, wr_lo, jnp.zeros((d, V7X_LANES - 2 * n_exp), BF16)], axis=1)
    assert DA_WIDTH == NA_WIDTH
    half_spec = lambda half: pl.BlockSpec((NA_WIDTH, d), lambda i: (half, 0))
    row_spec = lambda width: pl.BlockSpec((tm, width), lambda i: (i, 0))
    full_spec = lambda a: pl.BlockSpec(a.shape, lambda i: (0, 0))
    on_a = on_a.reshape(1, -1)
    ln2_g = ln2_g.reshape(1, -1)
    return pl.pallas_call(
        functools.partial(_outproj_kernel, sub=min(MATMUL_ROW_SUBTILE, tm)),
        grid=(n // tm,),
        in_specs=[row_spec(NA_WIDTH), row_spec(DA_WIDTH), row_spec(d), full_spec(on_a),
                  half_spec(0), half_spec(1), full_spec(ln2_g), full_spec(wr_pieces)],
        out_specs=[row_spec(d), pl.BlockSpec((n_exp, tm), lambda i: (0, i))],
        out_shape=[jax.ShapeDtypeStruct((n, d), F32), jax.ShapeDtypeStruct((n_exp, n), F32)],
        compiler_params=_params("parallel"),
        name="out_projection",
    )(oa2d, ob2d, x2d, on_a, w_out, w_out, ln2_g, wr_pieces)


def _exclusive_prefix(mask):
    rows, s_len = mask.shape
    m = jnp.where(mask, 1.0, 0.0).astype(BF16)
    r = lax.broadcasted_iota(jnp.int32, (V7X_LANES, V7X_LANES), 0)
    c = lax.broadcasted_iota(jnp.int32, (V7X_LANES, V7X_LANES), 1)
    before = jnp.where(r < c, 1.0, 0.0).astype(BF16)
    ones = jnp.ones((V7X_LANES, V7X_LANES), BF16)
    run = jnp.zeros((rows, V7X_LANES), F32)
    out = []
    for blk in range(s_len // V7X_LANES):
        mb = m[:, blk * V7X_LANES:(blk + 1) * V7X_LANES]
        out.append(run + jnp.dot(mb, before, preferred_element_type=F32))
        run = run + jnp.dot(mb, ones, preferred_element_type=F32)
    return jnp.concatenate(out, axis=1)


def _route_kernel(lg_ref, idx_ref, gate_ref, posm_ref, boff_ref, aff_ref, idx_acc, gate_acc,
                  boff_vmem, boff_smem, sem_ref, *, cap, slot_tile, tb):
    lg = lg_ref[...]
    n_exp, s_len = lg.shape
    ex = jnp.exp(lg - jnp.max(lg, axis=0, keepdims=True))
    aff_ref[...] = ex / jnp.sum(ex, axis=0, keepdims=True)
    aff = aff_ref[...]

    def enough(values, t):
        return jnp.sum(jnp.where(values >= t, 1.0, 0.0), axis=1, keepdims=True) >= cap

    thr = jnp.full((n_exp, 1), 2.0 ** -126, F32)
    for shift in (64, 32, 16, 8, 4, 2, 1):
        cand = thr * (2.0 ** shift)
        thr = jnp.where(enough(aff, cand), cand, thr)
    step = thr
    for _ in range(23):
        step = step * 0.5
        cand = thr + step
        thr = jnp.where(enough(aff, cand), cand, thr)
    thr = jnp.where(enough(aff, thr), thr, 0.0)
    res = aff - thr
    rho = jnp.zeros((n_exp, 1), F32)
    for _ in range(24):
        step = step * 0.5
        cand = rho + step
        rho = jnp.where(enough(res, cand), cand, rho)

    gt = res > rho
    eq = res == rho
    need = cap - jnp.sum(jnp.where(gt, 1.0, 0.0), axis=1, keepdims=True)
    sel = gt | (eq & (_exclusive_prefix(eq) < need))
    posm_ref[...] = jnp.where(sel, _exclusive_prefix(sel), -1.0)

    t_idx = lax.broadcasted_iota(jnp.int32, (s_len, V7X_LANES), 0)
    j_idx = lax.broadcasted_iota(jnp.int32, (s_len, V7X_LANES), 1)
    before_block = jnp.where(t_idx < j_idx * tb, 1.0, 0.0).astype(BF16)
    boff = jnp.dot(jnp.where(sel, 1.0, 0.0).astype(BF16), before_block, preferred_element_type=F32)
    boff_ref[...] = boff.astype(jnp.int32)
    boff_vmem[...] = boff.astype(jnp.int32)
    to_smem = pltpu.make_async_copy(boff_vmem, boff_smem, sem_ref.at[0])
    to_smem.start()
    to_smem.wait()

    n_blk = s_len // tb
    n_tiles = cap // slot_tile
    span = min(s_len, max(tb, ROUTE_WINDOW_TOKENS))
    lane_iota = lax.broadcasted_iota(jnp.int32, (1, span), 1)

    def per_expert(e, carry):
        offs = [boff_smem[e, j] for j in range(n_blk + 1)]
        first_tok, n_rounds = [], 0
        for st in range(n_tiles):
            s0 = st * slot_tile
            j_lo = sum((offs[j + 1] <= s0).astype(jnp.int32) for j in range(n_blk))
            j_hi = sum((offs[j] < s0 + slot_tile).astype(jnp.int32) for j in range(n_blk))
            first_tok.append(j_lo * tb)
            n_rounds = jnp.maximum(n_rounds, ((j_hi - j_lo) * tb + span - 1) // span)
        idx_acc[...] = jnp.zeros_like(idx_acc)
        gate_acc[...] = jnp.zeros_like(gate_acc)

        def one_round(r, c2):
            for st in range(n_tiles):
                nominal = first_tok[st] + r * span
                start = pl.multiple_of(jnp.minimum(nominal, s_len - span), V7X_LANES)
                tok_w = (start + lane_iota).astype(F32)
                pos_w = posm_ref[pl.ds(e, 1), pl.ds(start, span)]
                pos_w = jnp.where(tok_w >= nominal.astype(F32), pos_w, -1.0)
                aff_w = aff_ref[pl.ds(e, 1), pl.ds(start, span)]
                slots = (st * slot_tile + lax.broadcasted_iota(jnp.int32, (slot_tile, 1), 0)).astype(F32)
                hit = pos_w == slots
                rows = pl.ds(st * slot_tile, slot_tile)
                idx_acc[rows, :] += jnp.sum(jnp.where(hit, tok_w, 0.0), axis=1, keepdims=True)
                gate_acc[rows, :] += jnp.sum(jnp.where(hit, aff_w, 0.0), axis=1, keepdims=True)
            return c2

        lax.fori_loop(0, n_rounds, one_round, 0)
        idx_ref[e] = idx_acc[...].astype(jnp.int32)
        gate_ref[e] = gate_acc[...]
        return carry

    lax.fori_loop(0, n_exp, per_expert, 0)


def _ec_route(logits_t, batch, *, cap, slot_tile, tb):
    n_exp, n = logits_t.shape
    s = n // batch
    assert s // tb + 1 <= V7X_LANES
    slot_spec = pl.BlockSpec((n_exp, None, cap, 1), lambda b: (0, b, 0, 0))
    return pl.pallas_call(
        functools.partial(_route_kernel, cap=cap, slot_tile=slot_tile, tb=tb),
        grid=(batch,),
        in_specs=[pl.BlockSpec((n_exp, s), lambda b: (0, b))],
        out_specs=[slot_spec, slot_spec,
                   pl.BlockSpec((None, n_exp, s), lambda b: (b, 0, 0)),
                   pl.BlockSpec((None, n_exp, V7X_LANES), lambda b: (b, 0, 0))],
        out_shape=[jax.ShapeDtypeStruct((n_exp, batch, cap, 1), jnp.int32),
                   jax.ShapeDtypeStruct((n_exp, batch, cap, 1), F32),
                   jax.ShapeDtypeStruct((batch, n_exp, s), F32),
                   jax.ShapeDtypeStruct((batch, n_exp, V7X_LANES), jnp.int32)],
        scratch_shapes=[pltpu.VMEM((n_exp, s), F32), pltpu.VMEM((cap, 1), F32), pltpu.VMEM((cap, 1), F32),
                        pltpu.VMEM((n_exp, V7X_LANES), jnp.int32),
                        pltpu.SMEM((n_exp, V7X_LANES), jnp.int32), pltpu.SemaphoreType.DMA((1,))],
        compiler_params=_params("parallel"),
        name="ec_route",
    )(logits_t)


def _ffn_kernel(rows_ref, x1_hbm, g2_ref, gate_ref, wg_hbm, wu_hbm, wd_hbm, y_ref,
                xs_ref, xsb_ref, acc_ref, wg_buf, wu_buf, wd_buf, sem_ref, wsem_ref,
                *, m_rows, chunk, n_total, tf):
    e = pl.program_id(0)
    f = pl.program_id(1)
    n_exp = pl.num_programs(0)
    n_f = pl.num_programs(1)

    step = e * n_f + f
    n_steps = n_exp * n_f

    def weight_copies(s):
        se, sf = s // n_f, s % n_f
        slot = s % WEIGHT_RING
        cols = pl.ds(pl.multiple_of(sf * tf, V7X_LANES), tf)
        return [pltpu.make_async_copy(wg_hbm.at[se, :, cols], wg_buf.at[slot], wsem_ref.at[slot]),
                pltpu.make_async_copy(wu_hbm.at[se, :, cols], wu_buf.at[slot], wsem_ref.at[slot]),
                pltpu.make_async_copy(wd_hbm.at[se, cols, :], wd_buf.at[slot], wsem_ref.at[slot])]

    @pl.when(step == 0)
    def _():
        for s in range(WEIGHT_RING - 1):
            for cp in weight_copies(s):
                cp.start()

    @pl.when(step + (WEIGHT_RING - 1) < n_steps)
    def _():
        for cp in weight_copies(step + (WEIGHT_RING - 1)):
            cp.start()

    def row_copy(expert, i):
        src = rows_ref[expert * m_rows + jnp.minimum(i, m_rows - 1)]
        return pltpu.make_async_copy(x1_hbm.at[pl.ds(src, 1), :], xs_ref.at[pl.ds(i, 1), :],
                                     sem_ref.at[0])

    def wait_rows(count):
        pltpu.make_async_copy(x1_hbm.at[pl.ds(0, count), :], xs_ref.at[pl.ds(0, count), :],
                              sem_ref.at[0]).wait()

    def normalise_rows():
        xsb_ref[...] = (_rms(xs_ref[pl.ds(0, m_rows), :]) * g2_ref[...]).astype(BF16)

    @pl.when((e == 0) & (f == 0))
    def _():
        def group(ig, carry):
            for u in range(8):
                row_copy(0, ig * 8 + u).start()
            return carry
        lax.fori_loop(0, m_rows // 8, group, 0)
        wait_rows(m_rows)
        normalise_rows()

    @pl.when(f == 0)
    def _():
        acc_ref[...] = jnp.zeros_like(acc_ref)

    for cp in weight_copies(step):
        cp.wait()

    nxt = jnp.minimum(e + 1, n_exp - 1)
    for u in range(chunk):
        row_copy(nxt, f * chunk + u).start()

    slot = step % WEIGHT_RING
    xs = xsb_ref[...]
    g = jnp.dot(xs, wg_buf[slot].astype(BF16), preferred_element_type=F32)
    u = jnp.dot(xs, wu_buf[slot].astype(BF16), preferred_element_type=F32)
    a = (g * jax.nn.sigmoid(g) * u).astype(BF16)
    acc_ref[...] += jnp.dot(a, wd_buf[slot].astype(BF16), preferred_element_type=F32)

    @pl.when(f == n_f - 1)
    def _():
        y_ref[...] = (acc_ref[...] * gate_ref[...]).astype(y_ref.dtype)
        wait_rows(n_total)
        normalise_rows()


def _ec_ffn(rows_flat, x1, ln2_g, gate, w_gate, w_up, w_down, *, tf):
    n, d = x1.shape
    n_exp, _, ff = w_gate.shape
    m_rows = rows_flat.shape[0] // n_exp
    n_f = ff // tf
    assert m_rows % 8 == 0
    chunk = -(-m_rows // (8 * n_f)) * 8
    n_total = chunk * n_f
    assert n_total - m_rows < m_rows
    grid_spec = pltpu.PrefetchScalarGridSpec(
        num_scalar_prefetch=1,
        grid=(n_exp, n_f),
        in_specs=[
            pl.BlockSpec(memory_space=pl.ANY),
            pl.BlockSpec((1, d), lambda e, f, r: (0, 0)),
            pl.BlockSpec((None, m_rows, 1), lambda e, f, r: (e, 0, 0)),
            pl.BlockSpec(memory_space=pl.ANY),
            pl.BlockSpec(memory_space=pl.ANY),
            pl.BlockSpec(memory_space=pl.ANY),
        ],
        out_specs=pl.BlockSpec((None, m_rows, d), lambda e, f, r: (e, 0, 0)),
        scratch_shapes=[pltpu.VMEM((n_total, d), F32), pltpu.VMEM((m_rows, d), BF16),
                        pltpu.VMEM((m_rows, d), F32),
                        pltpu.VMEM((WEIGHT_RING, d, tf), F32), pltpu.VMEM((WEIGHT_RING, d, tf), F32),
                        pltpu.VMEM((WEIGHT_RING, tf, d), F32),
                        pltpu.SemaphoreType.DMA((1,)), pltpu.SemaphoreType.DMA((WEIGHT_RING,))],
    )
    return pl.pallas_call(
        functools.partial(_ffn_kernel, m_rows=m_rows, chunk=chunk, n_total=n_total, tf=tf),
        grid_spec=grid_spec,
        out_shape=jax.ShapeDtypeStruct((n_exp, m_rows, d), BF16),
        compiler_params=_params("arbitrary", "arbitrary"),
        name="ec_ffn",
    )(rows_flat, x1, ln2_g.reshape(1, d), gate, w_gate, w_up, w_down)


def _combine_kernel(boff_ref, x1_ref, posm_ref, y_hbm, o_ref, ywin_ref, sem_ref,
                    *, batch, cap, win, adv):
    n_exp, tb = posm_ref.shape
    i = pl.program_id(0)
    n_steps = pl.num_programs(0)
    nblk = n_steps // batch
    buf = i % 2

    def block_offsets(step, shift):
        bb = step // nblk
        return bb, [boff_ref[(bb * n_exp + e) * V7X_LANES + step % nblk + shift] for e in range(n_exp)]

    def window_copies(bb, lowers, to_buf):
        copies, starts = [], []
        for e in range(n_exp):
            start = jnp.minimum((lowers[e] // 16) * 16, cap - win)
            row0 = pl.multiple_of((e * batch + bb) * cap + start, 16)
            copies.append(pltpu.make_async_copy(y_hbm.at[pl.ds(row0, win), :],
                                                ywin_ref.at[to_buf, pl.ds(e * win, win), :],
                                                sem_ref.at[to_buf]))
            starts.append(start)
        return copies, starts

    b, offs = block_offsets(i, 0)
    _, nxts = block_offsets(i, 1)
    most = functools.reduce(jnp.maximum, [nx - of for nx, of in zip(nxts, offs)])

    @pl.when(i == 0)
    def _():
        for cp in window_copies(b, offs, buf)[0]:
            cp.start()

    @pl.when(i + 1 < n_steps)
    def _():
        b_next, offs_next = block_offsets(i + 1, 0)
        for cp in window_copies(b_next, offs_next, 1 - buf)[0]:
            cp.start()

    o_ref[...] = x1_ref[...]
    posm = posm_ref[...]
    prow = jnp.concatenate([jnp.broadcast_to(posm[e:e + 1, :], (win, tb)) for e in range(n_exp)], axis=0)

    def scatter_round(lowers, fetch):
        copies, starts = window_copies(b, lowers, buf)
        if fetch:
            for cp in copies:
                cp.start()
        cols = []
        for e in range(n_exp):
            slot = starts[e] + lax.broadcasted_iota(jnp.int32, (win, 1), 0)
            cols.append(jnp.where((slot >= lowers[e]) & (slot < lowers[e] + adv), slot, -2).astype(F32))
        hit_t = jnp.where(prow == jnp.concatenate(cols, axis=0), 1.0, 0.0).astype(BF16)
        for cp in copies:
            cp.wait()
        o_ref[...] += lax.dot_general(hit_t, ywin_ref[buf], (((0,), (0,)), ((), ())),
                                      preferred_element_type=F32)

    scatter_round(offs, fetch=False)

    def extra_round(k, carry):
        scatter_round([of + k * adv for of in offs], fetch=True)
        return carry

    lax.fori_loop(1, (most + adv - 1) // adv, extra_round, 0)


def _ec_combine(boff_flat, x1, posm, y2d, *, cap, tb, win, adv):
    n, d = x1.shape
    batch, n_exp, s = posm.shape
    nblk = s // tb
    assert cap % 16 == 0 and win % 16 == 0 and win >= adv + 16 and cap >= win
    grid_spec = pltpu.PrefetchScalarGridSpec(
        num_scalar_prefetch=1,
        grid=(n // tb,),
        in_specs=[
            pl.BlockSpec((tb, d), lambda i, bo: (i, 0)),
            pl.BlockSpec((None, n_exp, tb), lambda i, bo: (i // nblk, 0, i % nblk)),
            pl.BlockSpec(memory_space=pl.ANY),
        ],
        out_specs=pl.BlockSpec((tb, d), lambda i, bo: (i, 0)),
        scratch_shapes=[pltpu.VMEM((2, n_exp * win, d), BF16), pltpu.SemaphoreType.DMA((2,))],
    )
    return pl.pallas_call(
        functools.partial(_combine_kernel, batch=batch, cap=cap, win=win, adv=adv),
        grid_spec=grid_spec,
        out_shape=jax.ShapeDtypeStruct((n, d), F32),
        compiler_params=_params("arbitrary"),
        name="ec_combine",
    )(boff_flat, x1, posm, y2d)


COMBINE_TOKENS = 256
COMBINE_WINDOW = 80
COMBINE_ADVANCE = 64
ROUTE_SLOT_TILE = 64


def _tiles(n, d, ff):
    pick = lambda total, want: want if total % want == 0 else total
    return dict(tm_in=pick(n, 1024), tn_in=1024, tq=pick(n, 512), tm_out=pick(n, 512),
                tf=pick(ff, 256))


def kernel(x, ln1_g, w_in, qn_a, kn_a, rpb_a, on_a, qn_b, kn_b, lam_q1, lam_k1, lam_q2, lam_k2,
           subln_b, w_out, ln2_g, w_router, w_gate, w_up, w_down):
    b, s, d = x.shape
    n = b * s
    depth = w_in.shape[0]
    n_exp = w_router.shape[-1]
    cap = EC_CAPACITY_FACTOR * s // n_exp
    t = _tiles(n, d, w_gate.shape[-1])
    slopes = 2.0 ** (-8.0 * jnp.arange(1, DA_HEADS + 1, dtype=F32) / DA_HEADS)
    ones_v = jnp.ones((NA_WIDTH,), F32)
    x2d = x.reshape(n, d)
    for l in range(depth):
        tile_heads = lambda g, heads: jnp.tile(g.astype(F32), heads)
        col_gain = jnp.concatenate([
            tile_heads(qn_a[l], NA_HEADS), tile_heads(kn_a[l], NA_HEADS), ones_v,
            tile_heads(qn_b[l], 2 * DA_HEADS), tile_heads(kn_b[l], 2 * DA_HEADS),
            jnp.ones((DA_WIDTH,), F32)]).reshape(1, IN_COLS)
        proj = _in_projection(x2d, ln1_g[l], w_in[l], col_gain,
                              tm=t["tm_in"], tn=t["tn_in"])
        proj3d = proj.reshape(b, s, IN_COLS)
        oa = _na_attention(proj3d, *_na_bias_table(rpb_a[l], qn_a[l], kn_a[l]))
        ob = _da_attention(proj3d, slopes, qn_b[l], kn_b[l], lam_q1[l], lam_k1[l], lam_q2[l], lam_k2[l], subln_b[l],
                           lam_init=_lambda_init(l), tq=min(t["tq"], s))
        x1, logits_t = _out_projection(
            oa.reshape(n, NA_WIDTH), ob.reshape(n, DA_WIDTH), x2d, on_a[l],
            w_out[l].astype(BF16), ln2_g[l], w_router[l].astype(F32), tm=t["tm_out"])
        idx, gate, posm, boff = _ec_route(logits_t, b, cap=cap, slot_tile=ROUTE_SLOT_TILE, tb=COMBINE_TOKENS)
        rows_flat = (idx[..., 0] + (jnp.arange(b, dtype=jnp.int32) * s)[None, :, None]).reshape(-1)
        gate_e = gate.reshape(n_exp, b * cap, 1)
        y = _ec_ffn(rows_flat, x1, ln2_g[l], gate_e, w_gate[l], w_up[l], w_down[l], tf=t["tf"])
        x2d = _ec_combine(boff.reshape(-1), x1, posm, y.reshape(n_exp * b * cap, d), cap=cap,
                          tb=COMBINE_TOKENS, win=COMBINE_WINDOW, adv=COMBINE_ADVANCE)
    return x2d.reshape(b, s, d)
```

```python
import functools
import math

import jax
import jax.numpy as jnp
from jax import lax
from jax.experimental import pallas as pl
from jax.experimental.pallas import tpu as pltpu

F32 = jnp.float32
BF16 = jnp.bfloat16

GRID_W = 64
HEAD_DIM = 128
NA_HEADS = 8
NA_WIDTH = NA_HEADS * HEAD_DIM
NA_WIN_ROWS = 8
NA_WIN_COLS = 16
DA_HEADS = 4
DA_VDIM = 2 * HEAD_DIM
DA_WIDTH = DA_HEADS * DA_VDIM
DA_QK = DA_HEADS * HEAD_DIM
DA_LAYER_LAMBDA_BASE = 0.8
IN_COLS = 3 * NA_WIDTH + 4 * DA_QK + DA_WIDTH
N_EXPERTS = 16
EC_CAPACITY_FACTOR = 2
RMS_EPS = 1e-6
ATTN_SCALE = HEAD_DIM ** -0.5
LOG2_E = math.log2(math.e)
MAX_STATIC_SHIFT = 30.0
QK_BOUND_MARGIN = 1.02

V7X_LANES = 128
V7X_VMEM_LIMIT_BYTES = 56 * 1024 * 1024
MATMUL_ROW_SUBTILE = 256
NA_ROW_GROUP = 32
ROUTE_WINDOW_TOKENS = 1024
WEIGHT_RING = 3
WEIGHT_DMA_PRIORITY = 1

NT_DIMS = (((1,), (1,)), ((), ()))


def _lambda_init(layer_idx):
    return DA_LAYER_LAMBDA_BASE - 0.6 * math.exp(-0.3 * layer_idx)


def _rms(y):
    return y * lax.rsqrt(jnp.mean(y * y, axis=-1, keepdims=True) + RMS_EPS)


def _qk_score_bound(q_gain, k_gain):
    return (QK_BOUND_MARGIN * math.sqrt(HEAD_DIM) * jnp.max(jnp.abs(q_gain.astype(F32)))
            * jnp.max(jnp.abs(k_gain.astype(F32))))


def _params(*semantics):
    return pltpu.CompilerParams(dimension_semantics=semantics,
                                vmem_limit_bytes=V7X_VMEM_LIMIT_BYTES)


def _inproj_kernel(x_ref, g1_ref, w_ref, gain_ref, o_ref, h_ref, *, tn, sub):
    j = pl.program_id(1)

    @pl.when(j == 0)
    def _():
        h_ref[...] = (_rms(x_ref[...]) * g1_ref[...]).astype(BF16)

    w = w_ref[...].astype(BF16)
    col0 = j * tn
    qk_cols = (col0 < 2 * NA_WIDTH) | ((col0 >= 3 * NA_WIDTH) & (col0 < 3 * NA_WIDTH + 4 * DA_QK))

    tm = h_ref.shape[0]
    for r in range(tm // sub):
        rows = slice(r * sub, (r + 1) * sub)
        acc = jnp.dot(h_ref[rows, :], w, preferred_element_type=F32)
        for c in range(tn // HEAD_DIM):
            sl = slice(c * HEAD_DIM, (c + 1) * HEAD_DIM)
            y = acc[:, sl]
            inv = lax.rsqrt(jnp.mean(y * y, axis=-1, keepdims=True) + RMS_EPS)
            o_ref[rows, sl] = (y * jnp.where(qk_cols, inv, 1.0) * gain_ref[:, sl]).astype(o_ref.dtype)


def _in_projection(x2d, ln1_g, w_in, col_gain, *, tm, tn):
    n, d = x2d.shape
    cols = w_in.shape[1]
    return pl.pallas_call(
        functools.partial(_inproj_kernel, tn=tn, sub=min(MATMUL_ROW_SUBTILE, tm)),
        grid=(n // tm, cols // tn),
        in_specs=[
            pl.BlockSpec((tm, d), lambda i, j: (i, 0)),
            pl.BlockSpec((1, d), lambda i, j: (0, 0)),
            pl.BlockSpec((d, tn), lambda i, j: (0, j)),
            pl.BlockSpec((1, tn), lambda i, j: (0, j)),
        ],
        out_specs=pl.BlockSpec((tm, tn), lambda i, j: (i, j)),
        out_shape=jax.ShapeDtypeStruct((n, cols), BF16),
        scratch_shapes=[pltpu.VMEM((tm, d), BF16)],
        compiler_params=_params("parallel", "arbitrary"),
        name="in_projection",
    )(x2d, ln1_g.reshape(1, d), w_in, col_gain)


def _na_bias_table(rpb, qn_a, kn_a):
    rpb = rpb.astype(F32)
    c = jnp.arange(GRID_W)
    col_off = jnp.clip(c[None, :] - c[:, None], -(NA_WIN_COLS - 1), NA_WIN_COLS - 1) + (NA_WIN_COLS - 1)
    onehot = (col_off[None] == jnp.arange(2 * NA_WIN_COLS - 1)[:, None, None]).astype(F32)
    t = jnp.einsum("hro,oqk->hrqk", rpb, onehot, precision=lax.Precision.HIGHEST)
    cs = jnp.clip(c - NA_WIN_COLS // 2, 0, GRID_W - NA_WIN_COLS)
    col_valid = (c[None, :] >= cs[:, None]) & (c[None, :] < cs[:, None] + NA_WIN_COLS)
    qk_bound = _qk_score_bound(qn_a, kn_a)
    static_shift = 2.0 * qk_bound + (jnp.max(rpb) - jnp.min(rpb)) <= 2.0 * MAX_STATIC_SHIFT
    shift = jnp.where(static_shift, qk_bound + jnp.max(rpb), 0.0)
    t = jnp.where(col_valid, (t - shift) * LOG2_E, -jnp.inf)
    pairs = jnp.concatenate([t[:, :-1], t[:, 1:]], axis=-1)
    return pairs, static_shift.astype(jnp.int32).reshape(1)


def _na_kernel(flag_ref, q_ref, k_ref, v_ref, bias_ref, o_ref, *, rows, group):
    win = NA_WIN_ROWS * GRID_W

    def make_body(static_shift):
        def body(it, carry):
            r = [it * group + g for g in range(group)]
            rs = [jnp.clip(ri - NA_WIN_ROWS // 2, 0, rows - NA_WIN_ROWS) for ri in r]
            q0 = [pl.multiple_of(ri * GRID_W, GRID_W) for ri in r]
            k0 = [pl.multiple_of(rsi * GRID_W, GRID_W) for rsi in rs]
            s = [lax.dot_general(q_ref[pl.ds(q0[g], GRID_W), :], k_ref[pl.ds(k0[g], win), :], NT_DIMS,
                                 preferred_element_type=F32) for g in range(group)]
            p, l = [], []
            for g in range(group):
                base = rs[g] - r[g] + (NA_WIN_ROWS - 1)
                bias = jnp.concatenate([bias_ref[base + 2 * jj] for jj in range(NA_WIN_ROWS // 2)], axis=1)
                z = s[g] * (ATTN_SCALE * LOG2_E) + bias
                if not static_shift:
                    z = z - jnp.max(z, axis=-1, keepdims=True)
                pg = jnp.exp2(z)
                l.append(jnp.sum(pg, axis=-1, keepdims=True))
                p.append(pg.astype(BF16))
            for g in range(group):
                o = jnp.dot(p[g], v_ref[pl.ds(k0[g], win), :], preferred_element_type=F32)
                o_ref[pl.ds(q0[g], GRID_W), :] = (o / l[g]).astype(o_ref.dtype)
            return carry
        return body

    @pl.when(flag_ref[0] == 1)
    def _():
        lax.fori_loop(0, rows // group, make_body(True), 0)

    @pl.when(flag_ref[0] != 1)
    def _():
        lax.fori_loop(0, rows // group, make_body(False), 0)


def _na_attention(proj3d, bias_pairs, static_shift):
    b, s, _ = proj3d.shape
    rows = s // GRID_W
    assert rows >= NA_WIN_ROWS
    qkv_spec = lambda off: pl.BlockSpec((None, s, HEAD_DIM), lambda bi, h, fl: (bi, 0, off + h))
    grid_spec = pltpu.PrefetchScalarGridSpec(
        num_scalar_prefetch=1,
        grid=(b, NA_HEADS),
        in_specs=[
            qkv_spec(0), qkv_spec(NA_HEADS), qkv_spec(2 * NA_HEADS),
            pl.BlockSpec((None,) + bias_pairs.shape[1:], lambda bi, h, fl: (h, 0, 0, 0)),
        ],
        out_specs=pl.BlockSpec((None, s, HEAD_DIM), lambda bi, h, fl: (bi, 0, h)),
    )
    return pl.pallas_call(
        functools.partial(_na_kernel, rows=rows, group=math.gcd(rows, NA_ROW_GROUP)),
        grid_spec=grid_spec,
        out_shape=jax.ShapeDtypeStruct((b, s, NA_WIDTH), BF16),
        compiler_params=_params("parallel", "parallel"),
        name="na_attention",
    )(static_shift, proj3d, proj3d, proj3d, bias_pairs)


def _da_kernel(par_ref, q1_ref, q2_ref, k1_ref, k2_ref, v_ref,
               lq1_ref, lk1_ref, lq2_ref, lk2_ref, sub_ref, o_ref, alibi_ref, z_ref, p_ref,
               *, lam_init, kc):
    h = pl.program_id(0)
    qi = pl.program_id(2)
    tq = q1_ref.shape[0]
    s_len = k1_ref.shape[0]
    lam = (jnp.exp(jnp.sum(lq1_ref[...] * lk1_ref[...], axis=-1, keepdims=True))
           - jnp.exp(jnp.sum(lq2_ref[...] * lk2_ref[...], axis=-1, keepdims=True))
           + lam_init)

    bound = par_ref[DA_HEADS]
    static_shift = bound <= MAX_STATIC_SHIFT

    @pl.when((pl.program_id(1) == 0) & (qi == 0))
    def _():
        i = lax.broadcasted_iota(jnp.int32, (tq, 1), 0).astype(F32)
        x = (lax.broadcasted_iota(jnp.int32, (1, alibi_ref.shape[1]), 1) - (s_len - tq)).astype(F32)
        alibi_ref[...] = (jnp.abs(x - i) * (-LOG2_E * par_ref[h])
                          - LOG2_E * jnp.where(static_shift, bound, 0.0))

    win0 = s_len - tq - qi * tq
    n_chunks = s_len // kc
    heads = ((q1_ref, k1_ref), (q2_ref, k2_ref))

    def scores(q, k_ref, c):
        s = lax.dot_general(q, k_ref[c * kc:(c + 1) * kc, :], NT_DIMS, preferred_element_type=F32)
        return s * (ATTN_SCALE * LOG2_E) + alibi_ref[:, pl.ds(pl.multiple_of(win0 + c * kc, V7X_LANES), kc)]

    def finish(o1, l1, o2, l2):
        o = o1 * (1.0 / l1) - o2 * (lam / l2)
        o_ref[...] = (_rms(o) * sub_ref[...] * (1.0 - lam_init)).astype(o_ref.dtype)

    @pl.when(static_shift)
    def _():
        outs = []
        for t, (q_ref, k_ref) in enumerate(heads):
            q = q_ref[...]
            l_lanes = jnp.zeros((tq, V7X_LANES), F32)
            for c in range(n_chunks):
                p = jnp.exp2(scores(q, k_ref, c))
                for j in range(kc // V7X_LANES):
                    l_lanes = l_lanes + p[:, j * V7X_LANES:(j + 1) * V7X_LANES]
                p_ref[t, :, c * kc:(c + 1) * kc] = p.astype(BF16)
            o = jnp.dot(p_ref[t], v_ref[...], preferred_element_type=F32)
            outs += [o, jnp.sum(l_lanes, axis=-1, keepdims=True)]
        finish(*outs)

    @pl.when(jnp.logical_not(static_shift))
    def _():
        outs = []
        for t, (q_ref, k_ref) in enumerate(heads):
            q = q_ref[...]
            m = jnp.full((tq, 1), -jnp.inf, F32)
            for c in range(n_chunks):
                z = scores(q, k_ref, c)
                z_ref[:, c * kc:(c + 1) * kc] = z
                m = jnp.maximum(m, jnp.max(z, axis=-1, keepdims=True))
            l = jnp.zeros((tq, 1), F32)
            for c in range(n_chunks):
                p = jnp.exp2(z_ref[:, c * kc:(c + 1) * kc] - m)
                l = l + jnp.sum(p, axis=-1, keepdims=True)
                p_ref[t, :, c * kc:(c + 1) * kc] = p.astype(BF16)
            outs += [jnp.dot(p_ref[t], v_ref[...], preferred_element_type=F32), l]
        finish(*outs)


def _da_attention(proj3d, slopes, qn_b, kn_b, lam_q1, lam_k1, lam_q2, lam_k2, subln, *, lam_init, tq):
    b, s, _ = proj3d.shape
    qk0 = 3 * NA_WIDTH // HEAD_DIM
    v0 = (3 * NA_WIDTH + 4 * DA_QK) // DA_VDIM
    q_spec = lambda off: pl.BlockSpec((None, tq, HEAD_DIM), lambda h, bi, qi, sl: (bi, qi, off + h))
    k_spec = lambda off: pl.BlockSpec((None, s, HEAD_DIM), lambda h, bi, qi, sl: (bi, 0, off + h))
    vec_spec = lambda n: pl.BlockSpec((1, n), lambda h, bi, qi, sl: (0, 0))
    kc = min(512, s)
    par = jnp.concatenate([slopes, _qk_score_bound(qn_b, kn_b).reshape(1)])
    grid_spec = pltpu.PrefetchScalarGridSpec(
        num_scalar_prefetch=1,
        grid=(DA_HEADS, b, s // tq),
        in_specs=[
            q_spec(qk0), q_spec(qk0 + DA_HEADS), k_spec(qk0 + 2 * DA_HEADS), k_spec(qk0 + 3 * DA_HEADS),
            pl.BlockSpec((None, s, DA_VDIM), lambda h, bi, qi, sl: (bi, 0, v0 + h)),
            vec_spec(HEAD_DIM), vec_spec(HEAD_DIM), vec_spec(HEAD_DIM), vec_spec(HEAD_DIM),
            vec_spec(DA_VDIM),
        ],
        out_specs=pl.BlockSpec((None, tq, DA_VDIM), lambda h, bi, qi, sl: (bi, qi, h)),
        scratch_shapes=[pltpu.VMEM((tq, 2 * s - tq), F32), pltpu.VMEM((tq, s), F32),
                        pltpu.VMEM((2, tq, s), BF16)],
    )
    row = lambda a: a.reshape(1, -1).astype(F32)
    return pl.pallas_call(
        functools.partial(_da_kernel, lam_init=lam_init, kc=kc),
        grid_spec=grid_spec,
        out_shape=jax.ShapeDtypeStruct((b, s, DA_WIDTH), BF16),
        compiler_params=_params("arbitrary", "arbitrary", "arbitrary"),
        name="da_attention",
    )(par, proj3d, proj3d, proj3d, proj3d, proj3d,
      row(lam_q1), row(lam_k1), row(lam_q2), row(lam_k2), row(subln))


def _outproj_kernel(oa_ref, ob_ref, x_ref, ona_ref, wa_ref, wb_ref, g2_ref, wr_ref,
                    x1_ref, lg_ref, *, sub):
    n_exp = lg_ref.shape[0]
    tm = x_ref.shape[0]
    for r in range(tm // sub):
        rows = slice(r * sub, (r + 1) * sub)
        oan = (_rms(oa_ref[rows, :].astype(F32)) * ona_ref[...]).astype(BF16)
        acc = jnp.dot(oan, wa_ref[...], preferred_element_type=F32)
        acc = acc + jnp.dot(ob_ref[rows, :], wb_ref[...], preferred_element_type=F32)
        x1 = x_ref[rows, :] + acc
        x1_ref[rows, :] = x1
        h2 = _rms(x1) * g2_ref[...]
        h_hi = h2.astype(BF16)
        h_lo = (h2 - h_hi.astype(F32)).astype(BF16)
        parts = (jnp.dot(h_hi, wr_ref[...], preferred_element_type=F32)
                 + jnp.dot(h_lo, wr_ref[...], preferred_element_type=F32))
        parts_t = parts.T
        lg_ref[:, rows] = parts_t[:n_exp] + parts_t[n_exp:2 * n_exp]


def _out_projection(oa2d, ob2d, x2d, on_a, w_out, ln2_g, w_router, *, tm):
    n, d = x2d.shape
    n_exp = w_router.shape[1]
    assert 2 * n_exp <= V7X_LANES
    wr_hi = w_router.astype(BF16)
    wr_lo = (w_router - wr_hi.astype(F32)).astype(BF16)
    wr_pieces = jnp.concatenate(
        [wr_hi, wr_lo, jnp.zeros((d, V7X_LANES - 2 * n_exp), BF16)], axis=1)
    assert DA_WIDTH == NA_WIDTH
    half_spec = lambda half: pl.BlockSpec((NA_WIDTH, d), lambda i: (half, 0))
    row_spec = lambda width: pl.BlockSpec((tm, width), lambda i: (i, 0))
    full_spec = lambda a: pl.BlockSpec(a.shape, lambda i: (0, 0))
    on_a = on_a.reshape(1, -1)
    ln2_g = ln2_g.reshape(1, -1)
    return pl.pallas_call(
        functools.partial(_outproj_kernel, sub=min(MATMUL_ROW_SUBTILE, tm)),
        grid=(n // tm,),
        in_specs=[row_spec(NA_WIDTH), row_spec(DA_WIDTH), row_spec(d), full_spec(on_a),
                  half_spec(0), half_spec(1), full_spec(ln2_g), full_spec(wr_pieces)],
        out_specs=[row_spec(d), pl.BlockSpec((n_exp, tm), lambda i: (0, i))],
        out_shape=[jax.ShapeDtypeStruct((n, d), F32), jax.ShapeDtypeStruct((n_exp, n), F32)],
        compiler_params=_params("parallel"),
        name="out_projection",
    )(oa2d, ob2d, x2d, on_a, w_out, w_out, ln2_g, wr_pieces)


def _exclusive_prefix(mask):
    rows, s_len = mask.shape
    m = jnp.where(mask, 1.0, 0.0).astype(BF16)
    r = lax.broadcasted_iota(jnp.int32, (V7X_LANES, V7X_LANES), 0)
    c = lax.broadcasted_iota(jnp.int32, (V7X_LANES, V7X_LANES), 1)
    before = jnp.where(r < c, 1.0, 0.0).astype(BF16)
    ones = jnp.ones((V7X_LANES, V7X_LANES), BF16)
    run = jnp.zeros((rows, V7X_LANES), F32)
    out = []
    for blk in range(s_len // V7X_LANES):
        mb = m[:, blk * V7X_LANES:(blk + 1) * V7X_LANES]
        out.append(run + jnp.dot(mb, before, preferred_element_type=F32))
        run = run + jnp.dot(mb, ones, preferred_element_type=F32)
    return jnp.concatenate(out, axis=1)


def _route_kernel(lg_ref, idx_ref, gate_ref, posm_ref, boff_ref, aff_ref, idx_acc, gate_acc,
                  boff_vmem, boff_smem, sem_ref, *, cap, slot_tile, tb):
    lg = lg_ref[...]
    n_exp, s_len = lg.shape
    ex = jnp.exp(lg - jnp.max(lg, axis=0, keepdims=True))
    aff_ref[...] = ex / jnp.sum(ex, axis=0, keepdims=True)
    aff = aff_ref[...]

    def enough(values, t):
        return jnp.sum(jnp.where(values >= t, 1.0, 0.0), axis=1, keepdims=True) >= cap

    thr = jnp.full((n_exp, 1), 2.0 ** -126, F32)
    for shift in (64, 32, 16, 8, 4, 2, 1):
        cand = thr * (2.0 ** shift)
        thr = jnp.where(enough(aff, cand), cand, thr)
    step = thr
    for _ in range(23):
        step = step * 0.5
        cand = thr + step
        thr = jnp.where(enough(aff, cand), cand, thr)
    thr = jnp.where(enough(aff, thr), thr, 0.0)
    res = aff - thr
    rho = jnp.zeros((n_exp, 1), F32)
    for _ in range(24):
        step = step * 0.5
        cand = rho + step
        rho = jnp.where(enough(res, cand), cand, rho)

    gt = res > rho
    eq = res == rho
    need = cap - jnp.sum(jnp.where(gt, 1.0, 0.0), axis=1, keepdims=True)
    sel = gt | (eq & (_exclusive_prefix(eq) < need))
    posm_ref[...] = jnp.where(sel, _exclusive_prefix(sel), -1.0)

    t_idx = lax.broadcasted_iota(jnp.int32, (s_len, V7X_LANES), 0)
    j_idx = lax.broadcasted_iota(jnp.int32, (s_len, V7X_LANES), 1)
    before_block = jnp.where(t_idx < j_idx * tb, 1.0, 0.0).astype(BF16)
    boff = jnp.dot(jnp.where(sel, 1.0, 0.0).astype(BF16), before_block, preferred_element_type=F32)
    boff_ref[...] = boff.astype(jnp.int32)
    boff_vmem[...] = boff.astype(jnp.int32)
    to_smem = pltpu.make_async_copy(boff_vmem, boff_smem, sem_ref.at[0])
    to_smem.start()
    to_smem.wait()

    n_blk = s_len // tb
    n_tiles = cap // slot_tile
    span = min(s_len, max(tb, ROUTE_WINDOW_TOKENS))
    lane_iota = lax.broadcasted_iota(jnp.int32, (1, span), 1)

    def per_expert(e, carry):
        offs = [boff_smem[e, j] for j in range(n_blk + 1)]
        first_tok, n_rounds = [], 0
        for st in range(n_tiles):
            s0 = st * slot_tile
            j_lo = sum((offs[j + 1] <= s0).astype(jnp.int32) for j in range(n_blk))
            j_hi = sum((offs[j] < s0 + slot_tile).astype(jnp.int32) for j in range(n_blk))
            first_tok.append(j_lo * tb)
            n_rounds = jnp.maximum(n_rounds, ((j_hi - j_lo) * tb + span - 1) // span)
        idx_acc[...] = jnp.zeros_like(idx_acc)
        gate_acc[...] = jnp.zeros_like(gate_acc)

        def one_round(r, c2):
            for st in range(n_tiles):
                nominal = first_tok[st] + r * span
                start = pl.multiple_of(jnp.minimum(nominal, s_len - span), V7X_LANES)
                tok_w = (start + lane_iota).astype(F32)
                pos_w = posm_ref[pl.ds(e, 1), pl.ds(start, span)]
                pos_w = jnp.where(tok_w >= nominal.astype(F32), pos_w, -1.0)
                aff_w = aff_ref[pl.ds(e, 1), pl.ds(start, span)]
                slots = (st * slot_tile + lax.broadcasted_iota(jnp.int32, (slot_tile, 1), 0)).astype(F32)
                hit = pos_w == slots
                rows = pl.ds(st * slot_tile, slot_tile)
                idx_acc[rows, :] += jnp.sum(jnp.where(hit, tok_w, 0.0), axis=1, keepdims=True)
                gate_acc[rows, :] += jnp.sum(jnp.where(hit, aff_w, 0.0), axis=1, keepdims=True)
            return c2

        lax.fori_loop(0, n_rounds, one_round, 0)
        idx_ref[e] = idx_acc[...].astype(jnp.int32)
        gate_ref[e] = gate_acc[...]
        return carry

    lax.fori_loop(0, n_exp, per_expert, 0)


def _ec_route(logits_t, batch, *, cap, slot_tile, tb):
    n_exp, n = logits_t.shape
    s = n // batch
    assert s // tb + 1 <= V7X_LANES
    slot_spec = pl.BlockSpec((n_exp, None, cap, 1), lambda b: (0, b, 0, 0))
    return pl.pallas_call(
        functools.partial(_route_kernel, cap=cap, slot_tile=slot_tile, tb=tb),
        grid=(batch,),
        in_specs=[pl.BlockSpec((n_exp, s), lambda b: (0, b))],
        out_specs=[slot_spec, slot_spec,
                   pl.BlockSpec((None, n_exp, s), lambda b: (b, 0, 0)),
                   pl.BlockSpec((None, n_exp, V7X_LANES), lambda b: (b, 0, 0))],
        out_shape=[jax.ShapeDtypeStruct((n_exp, batch, cap, 1), jnp.int32),
                   jax.ShapeDtypeStruct((n_exp, batch, cap, 1), F32),
                   jax.ShapeDtypeStruct((batch, n_exp, s), F32),
                   jax.ShapeDtypeStruct((batch, n_exp, V7X_LANES), jnp.int32)],
        scratch_shapes=[pltpu.VMEM((n_exp, s), F32), pltpu.VMEM((cap, 1), F32), pltpu.VMEM((cap, 1), F32),
                        pltpu.VMEM((n_exp, V7X_LANES), jnp.int32),
                        pltpu.SMEM((n_exp, V7X_LANES), jnp.int32), pltpu.SemaphoreType.DMA((1,))],
        compiler_params=_params("parallel"),
        name="ec_route",
    )(logits_t)


def _ffn_kernel(rows_ref, x1_hbm, g2_ref, gate_ref, wg_hbm, wu_hbm, wd_hbm, y_ref,
                xs_ref, xsb_ref, acc_ref, wg_buf, wu_buf, wd_buf, sem_ref, wsem_ref,
                *, m_rows, chunk, n_total, tf):
    e = pl.program_id(0)
    f = pl.program_id(1)
    n_exp = pl.num_programs(0)
    n_f = pl.num_programs(1)

    step = e * n_f + f
    n_steps = n_exp * n_f

    def weight_copies(s):
        se, sf = s // n_f, s % n_f
        slot = s % WEIGHT_RING
        cols = pl.ds(pl.multiple_of(sf * tf, V7X_LANES), tf)
        return [pltpu.make_async_copy(wg_hbm.at[se, :, cols], wg_buf.at[slot], wsem_ref.at[slot]),
                pltpu.make_async_copy(wu_hbm.at[se, :, cols], wu_buf.at[slot], wsem_ref.at[slot]),
                pltpu.make_async_copy(wd_hbm.at[se, cols, :], wd_buf.at[slot], wsem_ref.at[slot])]

    @pl.when(step == 0)
    def _():
        for s in range(WEIGHT_RING - 1):
            for cp in weight_copies(s):
                cp.start(priority=WEIGHT_DMA_PRIORITY)

    @pl.when(step + (WEIGHT_RING - 1) < n_steps)
    def _():
        for cp in weight_copies(step + (WEIGHT_RING - 1)):
            cp.start(priority=WEIGHT_DMA_PRIORITY)

    def row_copy(expert, i):
        src = rows_ref[expert * m_rows + jnp.minimum(i, m_rows - 1)]
        return pltpu.make_async_copy(x1_hbm.at[pl.ds(src, 1), :], xs_ref.at[pl.ds(i, 1), :],
                                     sem_ref.at[0])

    def wait_rows(count):
        pltpu.make_async_copy(x1_hbm.at[pl.ds(0, count), :], xs_ref.at[pl.ds(0, count), :],
                              sem_ref.at[0]).wait()

    def normalise_rows():
        xsb_ref[...] = (_rms(xs_ref[pl.ds(0, m_rows), :]) * g2_ref[...]).astype(BF16)

    @pl.when((e == 0) & (f == 0))
    def _():
        def group(ig, carry):
            for u in range(8):
                row_copy(0, ig * 8 + u).start()
            return carry
        lax.fori_loop(0, m_rows // 8, group, 0)
        wait_rows(m_rows)
        normalise_rows()

    @pl.when(f == 0)
    def _():
        acc_ref[...] = jnp.zeros_like(acc_ref)

    for cp in weight_copies(step):
        cp.wait()

    nxt = jnp.minimum(e + 1, n_exp - 1)
    for u in range(chunk):
        row_copy(nxt, f * chunk + u).start()

    slot = step % WEIGHT_RING
    xs = xsb_ref[...]
    g = jnp.dot(xs, wg_buf[slot].astype(BF16), preferred_element_type=F32)
    u = jnp.dot(xs, wu_buf[slot].astype(BF16), preferred_element_type=F32)
    a = (g * jax.nn.sigmoid(g) * u).astype(BF16)
    acc_ref[...] += jnp.dot(a, wd_buf[slot].astype(BF16), preferred_element_type=F32)

    @pl.when(f == n_f - 1)
    def _():
        y_ref[...] = (acc_ref[...] * gate_ref[...]).astype(y_ref.dtype)
        wait_rows(n_total)
        normalise_rows()


def _ec_ffn(rows_flat, x1, ln2_g, gate, w_gate, w_up, w_down, *, tf):
    n, d = x1.shape
    n_exp, _, ff = w_gate.shape
    m_rows = rows_flat.shape[0] // n_exp
    n_f = ff // tf
    assert m_rows % 8 == 0
    chunk = -(-m_rows // (8 * n_f)) * 8
    n_total = chunk * n_f
    assert n_total - m_rows < m_rows
    grid_spec = pltpu.PrefetchScalarGridSpec(
        num_scalar_prefetch=1,
        grid=(n_exp, n_f),
        in_specs=[
            pl.BlockSpec(memory_space=pl.ANY),
            pl.BlockSpec((1, d), lambda e, f, r: (0, 0)),
            pl.BlockSpec((None, m_rows, 1), lambda e, f, r: (e, 0, 0)),
            pl.BlockSpec(memory_space=pl.ANY),
            pl.BlockSpec(memory_space=pl.ANY),
            pl.BlockSpec(memory_space=pl.ANY),
        ],
        out_specs=pl.BlockSpec((None, m_rows, d), lambda e, f, r: (e, 0, 0)),
        scratch_shapes=[pltpu.VMEM((n_total, d), F32), pltpu.VMEM((m_rows, d), BF16),
                        pltpu.VMEM((m_rows, d), F32),
                        pltpu.VMEM((WEIGHT_RING, d, tf), F32), pltpu.VMEM((WEIGHT_RING, d, tf), F32),
                        pltpu.VMEM((WEIGHT_RING, tf, d), F32),
                        pltpu.SemaphoreType.DMA((1,)), pltpu.SemaphoreType.DMA((WEIGHT_RING,))],
    )
    return pl.pallas_call(
        functools.partial(_ffn_kernel, m_rows=m_rows, chunk=chunk, n_total=n_total, tf=tf),
        grid_spec=grid_spec,
        out_shape=jax.ShapeDtypeStruct((n_exp, m_rows, d), BF16),
        compiler_params=_params("arbitrary", "arbitrary"),
        name="ec_ffn",
    )(rows_flat, x1, ln2_g.reshape(1, d), gate, w_gate, w_up, w_down)


def _combine_kernel(boff_ref, x1_ref, posm_ref, y_hbm, o_ref, ywin_ref, sem_ref,
                    *, batch, cap, win, adv):
    n_exp, tb = posm_ref.shape
    i = pl.program_id(0)
    n_steps = pl.num_programs(0)
    nblk = n_steps // batch
    buf = i % 2

    def block_offsets(step, shift):
        bb = step // nblk
        return bb, [boff_ref[(bb * n_exp + e) * V7X_LANES + step % nblk + shift] for e in range(n_exp)]

    def window_copies(bb, lowers, to_buf):
        copies, starts = [], []
        for e in range(n_exp):
            start = jnp.minimum((lowers[e] // 16) * 16, cap - win)
            row0 = pl.multiple_of((e * batch + bb) * cap + start, 16)
            copies.append(pltpu.make_async_copy(y_hbm.at[pl.ds(row0, win), :],
                                                ywin_ref.at[to_buf, pl.ds(e * win, win), :],
                                                sem_ref.at[to_buf]))
            starts.append(start)
        return copies, starts

    b, offs = block_offsets(i, 0)
    _, nxts = block_offsets(i, 1)
    most = functools.reduce(jnp.maximum, [nx - of for nx, of in zip(nxts, offs)])

    @pl.when(i == 0)
    def _():
        for cp in window_copies(b, offs, buf)[0]:
            cp.start()

    @pl.when(i + 1 < n_steps)
    def _():
        b_next, offs_next = block_offsets(i + 1, 0)
        for cp in window_copies(b_next, offs_next, 1 - buf)[0]:
            cp.start()

    o_ref[...] = x1_ref[...]
    posm = posm_ref[...]
    prow = jnp.concatenate([jnp.broadcast_to(posm[e:e + 1, :], (win, tb)) for e in range(n_exp)], axis=0)

    def scatter_round(lowers, fetch):
        copies, starts = window_copies(b, lowers, buf)
        if fetch:
            for cp in copies:
                cp.start()
        cols = []
        for e in range(n_exp):
            slot = starts[e] + lax.broadcasted_iota(jnp.int32, (win, 1), 0)
            cols.append(jnp.where((slot >= lowers[e]) & (slot < lowers[e] + adv), slot, -2).astype(F32))
        hit_t = jnp.where(prow == jnp.concatenate(cols, axis=0), 1.0, 0.0).astype(BF16)
        for cp in copies:
            cp.wait()
        o_ref[...] += lax.dot_general(hit_t, ywin_ref[buf], (((0,), (0,)), ((), ())),
                                      preferred_element_type=F32)

    scatter_round(offs, fetch=False)

    def extra_round(k, carry):
        scatter_round([of + k * adv for of in offs], fetch=True)
        return carry

    lax.fori_loop(1, (most + adv - 1) // adv, extra_round, 0)


def _ec_combine(boff_flat, x1, posm, y2d, *, cap, tb, win, adv):
    n, d = x1.shape
    batch, n_exp, s = posm.shape
    nblk = s // tb
    assert cap % 16 == 0 and win % 16 == 0 and win >= adv + 16 and cap >= win
    grid_spec = pltpu.PrefetchScalarGridSpec(
        num_scalar_prefetch=1,
        grid=(n // tb,),
        in_specs=[
            pl.BlockSpec((tb, d), lambda i, bo: (i, 0)),
            pl.BlockSpec((None, n_exp, tb), lambda i, bo: (i // nblk, 0, i % nblk)),
            pl.BlockSpec(memory_space=pl.ANY),
        ],
        out_specs=pl.BlockSpec((tb, d), lambda i, bo: (i, 0)),
        scratch_shapes=[pltpu.VMEM((2, n_exp * win, d), BF16), pltpu.SemaphoreType.DMA((2,))],
    )
    return pl.pallas_call(
        functools.partial(_combine_kernel, batch=batch, cap=cap, win=win, adv=adv),
        grid_spec=grid_spec,
        out_shape=jax.ShapeDtypeStruct((n, d), F32),
        compiler_params=_params("arbitrary"),
        name="ec_combine",
    )(boff_flat, x1, posm, y2d)


COMBINE_TOKENS = 256
COMBINE_WINDOW = 80
COMBINE_ADVANCE = 64
ROUTE_SLOT_TILE = 64


def _tiles(n, d, ff):
    pick = lambda total, want: want if total % want == 0 else total
    return dict(tm_in=pick(n, 1024), tn_in=1024, tq=pick(n, 512), tm_out=pick(n, 512),
                tf=pick(ff, 256))


def kernel(x, ln1_g, w_in, qn_a, kn_a, rpb_a, on_a, qn_b, kn_b, lam_q1, lam_k1, lam_q2, lam_k2,
           subln_b, w_out, ln2_g, w_router, w_gate, w_up, w_down):
    b, s, d = x.shape
    n = b * s
    depth = w_in.shape[0]
    n_exp = w_router.shape[-1]
    cap = EC_CAPACITY_FACTOR * s // n_exp
    t = _tiles(n, d, w_gate.shape[-1])
    slopes = 2.0 ** (-8.0 * jnp.arange(1, DA_HEADS + 1, dtype=F32) / DA_HEADS)
    ones_v = jnp.ones((NA_WIDTH,), F32)
    x2d = x.reshape(n, d)
    for l in range(depth):
        tile_heads = lambda g, heads: jnp.tile(g.astype(F32), heads)
        col_gain = jnp.concatenate([
            tile_heads(qn_a[l], NA_HEADS), tile_heads(kn_a[l], NA_HEADS), ones_v,
            tile_heads(qn_b[l], 2 * DA_HEADS), tile_heads(kn_b[l], 2 * DA_HEADS),
            jnp.ones((DA_WIDTH,), F32)]).reshape(1, IN_COLS)
        proj = _in_projection(x2d, ln1_g[l], w_in[l], col_gain,
                              tm=t["tm_in"], tn=t["tn_in"])
        proj3d = proj.reshape(b, s, IN_COLS)
        oa = _na_attention(proj3d, *_na_bias_table(rpb_a[l], qn_a[l], kn_a[l]))
        ob = _da_attention(proj3d, slopes, qn_b[l], kn_b[l], lam_q1[l], lam_k1[l], lam_q2[l], lam_k2[l], subln_b[l],
                           lam_init=_lambda_init(l), tq=min(t["tq"], s))
        x1, logits_t = _out_projection(
            oa.reshape(n, NA_WIDTH), ob.reshape(n, DA_WIDTH), x2d, on_a[l],
            w_out[l].astype(BF16), ln2_g[l], w_router[l].astype(F32), tm=t["tm_out"])
        idx, gate, posm, boff = _ec_route(logits_t, b, cap=cap, slot_tile=ROUTE_SLOT_TILE, tb=COMBINE_TOKENS)
        rows_flat = (idx[..., 0] + (jnp.arange(b, dtype=jnp.int32) * s)[None, :, None]).reshape(-1)
        gate_e = gate.reshape(n_exp, b * cap, 1)
        y = _ec_ffn(rows_flat, x1, ln2_g[l], gate_e, w_gate[l], w_up[l], w_down[l], tf=t["tf"])
        x2d = _ec_combine(boff.reshape(-1), x1, posm, y.reshape(n_exp * b * cap, d), cap=cap,
                          tb=COMBINE_TOKENS, win=COMBINE_WINDOW, adv=COMBINE_ADVANCE)
    return x2d.reshape(b, s, d)
```

```python
import functools
import math

import jax
import jax.numpy as jnp
from jax import lax
from jax.experimental import pallas as pl
from jax.experimental.pallas import tpu as pltpu

F32 = jnp.float32
BF16 = jnp.bfloat16

GRID_W = 64
HEAD_DIM = 128
NA_HEADS = 8
NA_WIDTH = NA_HEADS * HEAD_DIM
NA_WIN_ROWS = 8
NA_WIN_COLS = 16
DA_HEADS = 4
DA_VDIM = 2 * HEAD_DIM
DA_WIDTH = DA_HEADS * DA_VDIM
DA_QK = DA_HEADS * HEAD_DIM
DA_LAYER_LAMBDA_BASE = 0.8
IN_COLS = 3 * NA_WIDTH + 4 * DA_QK + DA_WIDTH
N_EXPERTS = 16
EC_CAPACITY_FACTOR = 2
RMS_EPS = 1e-6
ATTN_SCALE = HEAD_DIM ** -0.5
LOG2_E = math.log2(math.e)
MAX_STATIC_SHIFT = 30.0
QK_BOUND_MARGIN = 1.02

V7X_LANES = 128
V7X_VMEM_LIMIT_BYTES = 56 * 1024 * 1024
MATMUL_ROW_SUBTILE = 256
NA_ROW_GROUP = 32
ROUTE_WINDOW_TOKENS = 1024

NT_DIMS = (((1,), (1,)), ((), ()))


def _lambda_init(layer_idx):
    return DA_LAYER_LAMBDA_BASE - 0.6 * math.exp(-0.3 * layer_idx)


def _rms(y):
    return y * lax.rsqrt(jnp.mean(y * y, axis=-1, keepdims=True) + RMS_EPS)


def _qk_score_bound(q_gain, k_gain):
    return (QK_BOUND_MARGIN * math.sqrt(HEAD_DIM) * jnp.max(jnp.abs(q_gain.astype(F32)))
            * jnp.max(jnp.abs(k_gain.astype(F32))))


def _params(*semantics):
    return pltpu.CompilerParams(dimension_semantics=semantics,
                                vmem_limit_bytes=V7X_VMEM_LIMIT_BYTES)


def _inproj_kernel(x_ref, g1_ref, w_ref, gain_ref, o_ref, h_ref, *, tn, sub):
    j = pl.program_id(1)

    @pl.when(j == 0)
    def _():
        h_ref[...] = (_rms(x_ref[...]) * g1_ref[...]).astype(BF16)

    w = w_ref[...].astype(BF16)
    col0 = j * tn
    qk_cols = (col0 < 2 * NA_WIDTH) | ((col0 >= 3 * NA_WIDTH) & (col0 < 3 * NA_WIDTH + 4 * DA_QK))

    tm = h_ref.shape[0]
    for r in range(tm // sub):
        rows = slice(r * sub, (r + 1) * sub)
        acc = jnp.dot(h_ref[rows, :], w, preferred_element_type=F32)
        for c in range(tn // HEAD_DIM):
            sl = slice(c * HEAD_DIM, (c + 1) * HEAD_DIM)
            y = acc[:, sl]
            inv = lax.rsqrt(jnp.mean(y * y, axis=-1, keepdims=True) + RMS_EPS)
            o_ref[rows, sl] = (y * jnp.where(qk_cols, inv, 1.0) * gain_ref[:, sl]).astype(o_ref.dtype)


def _in_projection(x2d, ln1_g, w_in, col_gain, *, tm, tn):
    n, d = x2d.shape
    cols = w_in.shape[1]
    return pl.pallas_call(
        functools.partial(_inproj_kernel, tn=tn, sub=min(MATMUL_ROW_SUBTILE, tm)),
        grid=(n // tm, cols // tn),
        in_specs=[
            pl.BlockSpec((tm, d), lambda i, j: (i, 0)),
            pl.BlockSpec((1, d), lambda i, j: (0, 0)),
            pl.BlockSpec((d, tn), lambda i, j: (0, j)),
            pl.BlockSpec((1, tn), lambda i, j: (0, j)),
        ],
        out_specs=pl.BlockSpec((tm, tn), lambda i, j: (i, j)),
        out_shape=jax.ShapeDtypeStruct((n, cols), BF16),
        scratch_shapes=[pltpu.VMEM((tm, d), BF16)],
        compiler_params=_params("parallel", "arbitrary"),
        name="in_projection",
    )(x2d, ln1_g.reshape(1, d), w_in, col_gain)


def _na_bias_table(rpb, qn_a, kn_a):
    rpb = rpb.astype(F32)
    c = jnp.arange(GRID_W)
    col_off = jnp.clip(c[None, :] - c[:, None], -(NA_WIN_COLS - 1), NA_WIN_COLS - 1) + (NA_WIN_COLS - 1)
    onehot = (col_off[None] == jnp.arange(2 * NA_WIN_COLS - 1)[:, None, None]).astype(F32)
    t = jnp.einsum("hro,oqk->hrqk", rpb, onehot, precision=lax.Precision.HIGHEST)
    cs = jnp.clip(c - NA_WIN_COLS // 2, 0, GRID_W - NA_WIN_COLS)
    col_valid = (c[None, :] >= cs[:, None]) & (c[None, :] < cs[:, None] + NA_WIN_COLS)
    qk_bound = _qk_score_bound(qn_a, kn_a)
    static_shift = 2.0 * qk_bound + (jnp.max(rpb) - jnp.min(rpb)) <= 2.0 * MAX_STATIC_SHIFT
    shift = jnp.where(static_shift, qk_bound + jnp.max(rpb), 0.0)
    t = jnp.where(col_valid, (t - shift) * LOG2_E, -jnp.inf)
    pairs = jnp.concatenate([t[:, :-1], t[:, 1:]], axis=-1)
    return pairs, static_shift.astype(jnp.int32).reshape(1)


def _na_kernel(flag_ref, q_ref, k_ref, v_ref, bias_ref, o_ref, *, rows, group):
    win = NA_WIN_ROWS * GRID_W

    def make_body(static_shift):
        def body(it, carry):
            r = [it * group + g for g in range(group)]
            rs = [jnp.clip(ri - NA_WIN_ROWS // 2, 0, rows - NA_WIN_ROWS) for ri in r]
            q0 = [pl.multiple_of(ri * GRID_W, GRID_W) for ri in r]
            k0 = [pl.multiple_of(rsi * GRID_W, GRID_W) for rsi in rs]
            s = [lax.dot_general(q_ref[pl.ds(q0[g], GRID_W), :], k_ref[pl.ds(k0[g], win), :], NT_DIMS,
                                 preferred_element_type=F32) for g in range(group)]
            p, l = [], []
            for g in range(group):
                base = rs[g] - r[g] + (NA_WIN_ROWS - 1)
                bias = jnp.concatenate([bias_ref[base + 2 * jj] for jj in range(NA_WIN_ROWS // 2)], axis=1)
                z = s[g] * (ATTN_SCALE * LOG2_E) + bias
                if not static_shift:
                    z = z - jnp.max(z, axis=-1, keepdims=True)
                pg = jnp.exp2(z)
                l.append(jnp.sum(pg, axis=-1, keepdims=True))
                p.append(pg.astype(BF16))
            for g in range(group):
                o = jnp.dot(p[g], v_ref[pl.ds(k0[g], win), :], preferred_element_type=F32)
                o_ref[pl.ds(q0[g], GRID_W), :] = (o / l[g]).astype(o_ref.dtype)
            return carry
        return body

    @pl.when(flag_ref[0] == 1)
    def _():
        lax.fori_loop(0, rows // group, make_body(True), 0)

    @pl.when(flag_ref[0] != 1)
    def _():
        lax.fori_loop(0, rows // group, make_body(False), 0)


def _na_attention(proj3d, bias_pairs, static_shift):
    b, s, _ = proj3d.shape
    rows = s // GRID_W
    assert rows >= NA_WIN_ROWS
    qkv_spec = lambda off: pl.BlockSpec((None, s, HEAD_DIM), lambda bi, h, fl: (bi, 0, off + h))
    grid_spec = pltpu.PrefetchScalarGridSpec(
        num_scalar_prefetch=1,
        grid=(b, NA_HEADS),
        in_specs=[
            qkv_spec(0), qkv_spec(NA_HEADS), qkv_spec(2 * NA_HEADS),
            pl.BlockSpec((None,) + bias_pairs.shape[1:], lambda bi, h, fl: (h, 0, 0, 0)),
        ],
        out_specs=pl.BlockSpec((None, s, HEAD_DIM), lambda bi, h, fl: (bi, 0, h)),
    )
    return pl.pallas_call(
        functools.partial(_na_kernel, rows=rows, group=math.gcd(rows, NA_ROW_GROUP)),
        grid_spec=grid_spec,
        out_shape=jax.ShapeDtypeStruct((b, s, NA_WIDTH), BF16),
        compiler_params=_params("parallel", "parallel"),
        name="na_attention",
    )(static_shift, proj3d, proj3d, proj3d, bias_pairs)


def _da_kernel(par_ref, q1_ref, q2_ref, k1_ref, k2_ref, v_ref,
               lq1_ref, lk1_ref, lq2_ref, lk2_ref, sub_ref, o_ref, alibi_ref, z_ref, p_ref,
               *, lam_init, kc):
    h = pl.program_id(0)
    qi = pl.program_id(2)
    tq = q1_ref.shape[0]
    s_len = k1_ref.shape[0]
    lam = (jnp.exp(jnp.sum(lq1_ref[...] * lk1_ref[...], axis=-1, keepdims=True))
           - jnp.exp(jnp.sum(lq2_ref[...] * lk2_ref[...], axis=-1, keepdims=True))
           + lam_init)

    bound = par_ref[DA_HEADS]
    static_shift = bound <= MAX_STATIC_SHIFT

    @pl.when((pl.program_id(1) == 0) & (qi == 0))
    def _():
        i = lax.broadcasted_iota(jnp.int32, (tq, 1), 0).astype(F32)
        x = (lax.broadcasted_iota(jnp.int32, (1, alibi_ref.shape[1]), 1) - (s_len - tq)).astype(F32)
        alibi_ref[...] = (jnp.abs(x - i) * (-LOG2_E * par_ref[h])
                          - LOG2_E * jnp.where(static_shift, bound, 0.0))

    win0 = s_len - tq - qi * tq
    n_chunks = s_len // kc
    heads = ((q1_ref, k1_ref), (q2_ref, k2_ref))

    def scores(q, k_ref, c):
        s = lax.dot_general(q, k_ref[c * kc:(c + 1) * kc, :], NT_DIMS, preferred_element_type=F32)
        return s * (ATTN_SCALE * LOG2_E) + alibi_ref[:, pl.ds(pl.multiple_of(win0 + c * kc, V7X_LANES), kc)]

    def finish(o1, l1, o2, l2):
        o = o1 * (1.0 / l1) - o2 * (lam / l2)
        o_ref[...] = (_rms(o) * sub_ref[...] * (1.0 - lam_init)).astype(o_ref.dtype)

    @pl.when(static_shift)
    def _():
        outs = []
        for t, (q_ref, k_ref) in enumerate(heads):
            q = q_ref[...]
            l_lanes = jnp.zeros((tq, V7X_LANES), F32)
            for c in range(n_chunks):
                p = jnp.exp2(scores(q, k_ref, c))
                for j in range(kc // V7X_LANES):
                    l_lanes = l_lanes + p[:, j * V7X_LANES:(j + 1) * V7X_LANES]
                p_ref[t, :, c * kc:(c + 1) * kc] = p.astype(BF16)
            o = jnp.dot(p_ref[t], v_ref[...], preferred_element_type=F32)
            outs += [o, jnp.sum(l_lanes, axis=-1, keepdims=True)]
        finish(*outs)

    @pl.when(jnp.logical_not(static_shift))
    def _():
        outs = []
        for t, (q_ref, k_ref) in enumerate(heads):
            q = q_ref[...]
            m = jnp.full((tq, 1), -jnp.inf, F32)
            for c in range(n_chunks):
                z = scores(q, k_ref, c)
                z_ref[:, c * kc:(c + 1) * kc] = z
                m = jnp.maximum(m, jnp.max(z, axis=-1, keepdims=True))
            l = jnp.zeros((tq, 1), F32)
            for c in range(n_chunks):
                p = jnp.exp2(z_ref[:, c * kc:(c + 1) * kc] - m)
                l = l + jnp.sum(p, axis=-1, keepdims=True)
                p_ref[t, :, c * kc:(c + 1) * kc] = p.astype(BF16)
            outs += [jnp.dot(p_ref[t], v_ref[...], preferred_element_type=F32), l]
        finish(*outs)


def _da_attention(proj3d, slopes, qn_b, kn_b, lam_q1, lam_k1, lam_q2, lam_k2, subln, *, lam_init, tq):
    b, s, _ = proj3d.shape
    qk0 = 3 * NA_WIDTH // HEAD_DIM
    v0 = (3 * NA_WIDTH + 4 * DA_QK) // DA_VDIM
    q_spec = lambda off: pl.BlockSpec((None, tq, HEAD_DIM), lambda h, bi, qi, sl: (bi, qi, off + h))
    k_spec = lambda off: pl.BlockSpec((None, s, HEAD_DIM), lambda h, bi, qi, sl: (bi, 0, off + h))
    vec_spec = lambda n: pl.BlockSpec((1, n), lambda h, bi, qi, sl: (0, 0))
    kc = min(512, s)
    par = jnp.concatenate([slopes, _qk_score_bound(qn_b, kn_b).reshape(1)])
    grid_spec = pltpu.PrefetchScalarGridSpec(
        num_scalar_prefetch=1,
        grid=(DA_HEADS, b, s // tq),
        in_specs=[
            q_spec(qk0), q_spec(qk0 + DA_HEADS), k_spec(qk0 + 2 * DA_HEADS), k_spec(qk0 + 3 * DA_HEADS),
            pl.BlockSpec((None, s, DA_VDIM), lambda h, bi, qi, sl: (bi, 0, v0 + h)),
            vec_spec(HEAD_DIM), vec_spec(HEAD_DIM), vec_spec(HEAD_DIM), vec_spec(HEAD_DIM),
            vec_spec(DA_VDIM),
        ],
        out_specs=pl.BlockSpec((None, tq, DA_VDIM), lambda h, bi, qi, sl: (bi, qi, h)),
        scratch_shapes=[pltpu.VMEM((tq, 2 * s - tq), F32), pltpu.VMEM((tq, s), F32),
                        pltpu.VMEM((2, tq, s), BF16)],
    )
    row = lambda a: a.reshape(1, -1).astype(F32)
    return pl.pallas_call(
        functools.partial(_da_kernel, lam_init=lam_init, kc=kc),
        grid_spec=grid_spec,
        out_shape=jax.ShapeDtypeStruct((b, s, DA_WIDTH), BF16),
        compiler_params=_params("arbitrary", "arbitrary", "arbitrary"),
        name="da_attention",
    )(par, proj3d, proj3d, proj3d, proj3d, proj3d,
      row(lam_q1), row(lam_k1), row(lam_q2), row(lam_k2), row(subln))


def _outproj_kernel(oa_ref, ob_ref, x_ref, ona_ref, wa_ref, wb_ref, g2_ref, wr_ref,
                    x1_ref, lg_ref, *, sub):
    n_exp = lg_ref.shape[0]
    tm = x_ref.shape[0]
    for r in range(tm // sub):
        rows = slice(r * sub, (r + 1) * sub)
        oan = (_rms(oa_ref[rows, :].astype(F32)) * ona_ref[...]).astype(BF16)
        acc = jnp.dot(oan, wa_ref[...], preferred_element_type=F32)
        acc = acc + jnp.dot(ob_ref[rows, :], wb_ref[...], preferred_element_type=F32)
        x1 = x_ref[rows, :] + acc
        x1_ref[rows, :] = x1
        h2 = _rms(x1) * g2_ref[...]
        h_hi = h2.astype(BF16)
        h_lo = (h2 - h_hi.astype(F32)).astype(BF16)
        parts = (jnp.dot(h_hi, wr_ref[...], preferred_element_type=F32)
                 + jnp.dot(h_lo, wr_ref[...], preferred_element_type=F32))
        parts_t = parts.T
        lg_ref[:, rows] = parts_t[:n_exp] + parts_t[n_exp:2 * n_exp]


def _out_projection(oa2d, ob2d, x2d, on_a, w_out, ln2_g, w_router, *, tm):
    n, d = x2d.shape
    n_exp = w_router.shape[1]
    assert 2 * n_exp <= V7X_LANES
    wr_hi = w_router.astype(BF16)
    wr_lo = (w_router - wr_hi.astype(F32)).astype(BF16)
    wr_pieces = jnp.concatenate(
        [wr_hi, wr_lo, jnp.zeros((d, V7X_LANES - 2 * n_exp), BF16)], axis=1)
    assert DA_WIDTH == NA_WIDTH
    half_spec = lambda half: pl.BlockSpec((NA_WIDTH, d), lambda i: (half, 0))
    row_spec = lambda width: pl.BlockSpec((tm, width), lambda i: (i, 0))
    full_spec = lambda a: pl.BlockSpec(a.shape, lambda i: (0, 0))
    on_a = on_a.reshape(1, -1)
    ln2_g = ln2_g.reshape(1, -1)
    return pl.pallas_call(
        functools.partial(_outproj_kernel, sub=min(MATMUL_ROW_SUBTILE, tm)),
        grid=(n // tm,),
        in_specs=[row_spec(NA_WIDTH), row_spec(DA_WIDTH), row_spec(d), full_spec(on_a),
                  half_spec(0), half_spec(1), full_spec(ln2_g), full_spec(wr_pieces)],
        out_specs=[row_spec(d), pl.BlockSpec((n_exp, tm), lambda i: (0, i))],
        out_shape=[jax.ShapeDtypeStruct((n, d), F32), jax.ShapeDtypeStruct((n_exp, n), F32)],
        compiler_params=_params("parallel"),
        name="out_projection",
    )(oa2d, ob2d, x2d, on_a, w_out, w_out, ln2_g, wr_pieces)


def _exclusive_prefix(mask):
    rows, s_len = mask.shape
    m = jnp.where(mask, 1.0, 0.0).astype(BF16)
    r = lax.broadcasted_iota(jnp.int32, (V7X_LANES, V7X_LANES), 0)
    c = lax.broadcasted_iota(jnp.int32, (V7X_LANES, V7X_LANES), 1)
    before = jnp.where(r < c, 1.0, 0.0).astype(BF16)
    ones = jnp.ones((V7X_LANES, V7X_LANES), BF16)
    run = jnp.zeros((rows, V7X_LANES), F32)
    out = []
    for blk in range(s_len // V7X_LANES):
        mb = m[:, blk * V7X_LANES:(blk + 1) * V7X_LANES]
        out.append(run + jnp.dot(mb, before, preferred_element_type=F32))
        run = run + jnp.dot(mb, ones, preferred_element_type=F32)
    return jnp.concatenate(out, axis=1)


def _route_kernel(lg_ref, idx_ref, gate_ref, posm_ref, boff_ref, aff_ref, idx_acc, gate_acc,
                  boff_vmem, boff_smem, sem_ref, *, cap, slot_tile, tb):
    lg = lg_ref[...]
    n_exp, s_len = lg.shape
    ex = jnp.exp(lg - jnp.max(lg, axis=0, keepdims=True))
    aff_ref[...] = ex / jnp.sum(ex, axis=0, keepdims=True)
    aff = aff_ref[...]

    def enough(values, t):
        return jnp.sum(jnp.where(values >= t, 1.0, 0.0), axis=1, keepdims=True) >= cap

    thr = jnp.full((n_exp, 1), 2.0 ** -126, F32)
    for shift in (64, 32, 16, 8, 4, 2, 1):
        cand = thr * (2.0 ** shift)
        thr = jnp.where(enough(aff, cand), cand, thr)
    step = thr
    for _ in range(23):
        step = step * 0.5
        cand = thr + step
        thr = jnp.where(enough(aff, cand), cand, thr)
    thr = jnp.where(enough(aff, thr), thr, 0.0)
    res = aff - thr
    rho = jnp.zeros((n_exp, 1), F32)
    for _ in range(24):
        step = step * 0.5
        cand = rho + step
        rho = jnp.where(enough(res, cand), cand, rho)

    gt = res > rho
    eq = res == rho
    need = cap - jnp.sum(jnp.where(gt, 1.0, 0.0), axis=1, keepdims=True)
    sel = gt | (eq & (_exclusive_prefix(eq) < need))
    posm_ref[...] = jnp.where(sel, _exclusive_prefix(sel), -1.0)

    t_idx = lax.broadcasted_iota(jnp.int32, (s_len, V7X_LANES), 0)
    j_idx = lax.broadcasted_iota(jnp.int32, (s_len, V7X_LANES), 1)
    before_block = jnp.where(t_idx < j_idx * tb, 1.0, 0.0).astype(BF16)
    boff = jnp.dot(jnp.where(sel, 1.0, 0.0).astype(BF16), before_block, preferred_element_type=F32)
    boff_ref[...] = boff.astype(jnp.int32)
    boff_vmem[...] = boff.astype(jnp.int32)
    to_smem = pltpu.make_async_copy(boff_vmem, boff_smem, sem_ref.at[0])
    to_smem.start()
    to_smem.wait()

    n_blk = s_len // tb
    n_tiles = cap // slot_tile
    span = min(s_len, max(tb, ROUTE_WINDOW_TOKENS))
    lane_iota = lax.broadcasted_iota(jnp.int32, (1, span), 1)

    def per_expert(e, carry):
        offs = [boff_smem[e, j] for j in range(n_blk + 1)]
        first_tok, n_rounds = [], 0
        for st in range(n_tiles):
            s0 = st * slot_tile
            j_lo = sum((offs[j + 1] <= s0).astype(jnp.int32) for j in range(n_blk))
            j_hi = sum((offs[j] < s0 + slot_tile).astype(jnp.int32) for j in range(n_blk))
            first_tok.append(j_lo * tb)
            n_rounds = jnp.maximum(n_rounds, ((j_hi - j_lo) * tb + span - 1) // span)
        idx_acc[...] = jnp.zeros_like(idx_acc)
        gate_acc[...] = jnp.zeros_like(gate_acc)

        def one_round(r, c2):
            for st in range(n_tiles):
                nominal = first_tok[st] + r * span
                start = pl.multiple_of(jnp.minimum(nominal, s_len - span), V7X_LANES)
                tok_w = (start + lane_iota).astype(F32)
                pos_w = posm_ref[pl.ds(e, 1), pl.ds(start, span)]
                pos_w = jnp.where(tok_w >= nominal.astype(F32), pos_w, -1.0)
                aff_w = aff_ref[pl.ds(e, 1), pl.ds(start, span)]
                slots = (st * slot_tile + lax.broadcasted_iota(jnp.int32, (slot_tile, 1), 0)).astype(F32)
                hit = pos_w == slots
                rows = pl.ds(st * slot_tile, slot_tile)
                idx_acc[rows, :] += jnp.sum(jnp.where(hit, tok_w, 0.0), axis=1, keepdims=True)
                gate_acc[rows, :] += jnp.sum(jnp.where(hit, aff_w, 0.0), axis=1, keepdims=True)
            return c2

        lax.fori_loop(0, n_rounds, one_round, 0)
        idx_ref[e] = idx_acc[...].astype(jnp.int32)
        gate_ref[e] = gate_acc[...]
        return carry

    lax.fori_loop(0, n_exp, per_expert, 0)


def _ec_route(logits_t, batch, *, cap, slot_tile, tb):
    n_exp, n = logits_t.shape
    s = n // batch
    assert s // tb + 1 <= V7X_LANES
    slot_spec = pl.BlockSpec((n_exp, None, cap, 1), lambda b: (0, b, 0, 0))
    return pl.pallas_call(
        functools.partial(_route_kernel, cap=cap, slot_tile=slot_tile, tb=tb),
        grid=(batch,),
        in_specs=[pl.BlockSpec((n_exp, s), lambda b: (0, b))],
        out_specs=[slot_spec, slot_spec,
                   pl.BlockSpec((None, n_exp, s), lambda b: (b, 0, 0)),
                   pl.BlockSpec((None, n_exp, V7X_LANES), lambda b: (b, 0, 0))],
        out_shape=[jax.ShapeDtypeStruct((n_exp, batch, cap, 1), jnp.int32),
                   jax.ShapeDtypeStruct((n_exp, batch, cap, 1), F32),
                   jax.ShapeDtypeStruct((batch, n_exp, s), F32),
                   jax.ShapeDtypeStruct((batch, n_exp, V7X_LANES), jnp.int32)],
        scratch_shapes=[pltpu.VMEM((n_exp, s), F32), pltpu.VMEM((cap, 1), F32), pltpu.VMEM((cap, 1), F32),
                        pltpu.VMEM((n_exp, V7X_LANES), jnp.int32),
                        pltpu.SMEM((n_exp, V7X_LANES), jnp.int32), pltpu.SemaphoreType.DMA((1,))],
        compiler_params=_params("parallel"),
        name="ec_route",
    )(logits_t)


def _ffn_kernel(rows_ref, x1_hbm, g2_ref, gate_ref, wg_ref, wu_ref, wd_ref, y_ref,
                xs_ref, xsb_ref, acc_ref, sem_ref, *, m_rows, chunk, n_total):
    e = pl.program_id(0)
    f = pl.program_id(1)
    n_exp = pl.num_programs(0)
    n_f = pl.num_programs(1)

    def row_copy(expert, i):
        src = rows_ref[expert * m_rows + jnp.minimum(i, m_rows - 1)]
        return pltpu.make_async_copy(x1_hbm.at[pl.ds(src, 1), :], xs_ref.at[pl.ds(i, 1), :],
                                     sem_ref.at[0])

    def wait_rows(count):
        pltpu.make_async_copy(x1_hbm.at[pl.ds(0, count), :], xs_ref.at[pl.ds(0, count), :],
                              sem_ref.at[0]).wait()

    def normalise_rows():
        xsb_ref[...] = (_rms(xs_ref[pl.ds(0, m_rows), :]) * g2_ref[...]).astype(BF16)

    @pl.when((e == 0) & (f == 0))
    def _():
        def group(ig, carry):
            for u in range(8):
                row_copy(0, ig * 8 + u).start()
            return carry
        lax.fori_loop(0, m_rows // 8, group, 0)
        wait_rows(m_rows)
        normalise_rows()

    @pl.when(f == 0)
    def _():
        acc_ref[...] = jnp.zeros_like(acc_ref)

    nxt = jnp.minimum(e + 1, n_exp - 1)
    for u in range(chunk):
        row_copy(nxt, f * chunk + u).start(priority=u % 2)

    xs = xsb_ref[...]
    g = jnp.dot(xs, wg_ref[...].astype(BF16), preferred_element_type=F32)
    u = jnp.dot(xs, wu_ref[...].astype(BF16), preferred_element_type=F32)
    a = (g * jax.nn.sigmoid(g) * u).astype(BF16)
    acc_ref[...] += jnp.dot(a, wd_ref[...].astype(BF16), preferred_element_type=F32)

    @pl.when(f == n_f - 1)
    def _():
        y_ref[...] = (acc_ref[...] * gate_ref[...]).astype(y_ref.dtype)
        wait_rows(n_total)
        normalise_rows()


def _ec_ffn(rows_flat, x1, ln2_g, gate, w_gate, w_up, w_down, *, tf):
    n, d = x1.shape
    n_exp, _, ff = w_gate.shape
    m_rows = rows_flat.shape[0] // n_exp
    n_f = ff // tf
    assert m_rows % 8 == 0
    chunk = -(-m_rows // (8 * n_f)) * 8
    n_total = chunk * n_f
    assert n_total - m_rows < m_rows
    grid_spec = pltpu.PrefetchScalarGridSpec(
        num_scalar_prefetch=1,
        grid=(n_exp, n_f),
        in_specs=[
            pl.BlockSpec(memory_space=pl.ANY),
            pl.BlockSpec((1, d), lambda e, f, r: (0, 0)),
            pl.BlockSpec((None, m_rows, 1), lambda e, f, r: (e, 0, 0)),
            pl.BlockSpec((None, d, tf), lambda e, f, r: (e, 0, f)),
            pl.BlockSpec((None, d, tf), lambda e, f, r: (e, 0, f)),
            pl.BlockSpec((None, tf, d), lambda e, f, r: (e, f, 0)),
        ],
        out_specs=pl.BlockSpec((None, m_rows, d), lambda e, f, r: (e, 0, 0)),
        scratch_shapes=[pltpu.VMEM((n_total, d), F32), pltpu.VMEM((m_rows, d), BF16),
                        pltpu.VMEM((m_rows, d), F32), pltpu.SemaphoreType.DMA((1,))],
    )
    return pl.pallas_call(
        functools.partial(_ffn_kernel, m_rows=m_rows, chunk=chunk, n_total=n_total),
        grid_spec=grid_spec,
        out_shape=jax.ShapeDtypeStruct((n_exp, m_rows, d), BF16),
        compiler_params=_params("arbitrary", "arbitrary"),
        name="ec_ffn",
    )(rows_flat, x1, ln2_g.reshape(1, d), gate, w_gate, w_up, w_down)


def _combine_kernel(boff_ref, x1_ref, posm_ref, y_hbm, o_ref, ywin_ref, sem_ref,
                    *, batch, cap, win, adv):
    n_exp, tb = posm_ref.shape
    i = pl.program_id(0)
    n_steps = pl.num_programs(0)
    nblk = n_steps // batch
    buf = i % 2

    def block_offsets(step, shift):
        bb = step // nblk
        return bb, [boff_ref[(bb * n_exp + e) * V7X_LANES + step % nblk + shift] for e in range(n_exp)]

    def window_copies(bb, lowers, to_buf):
        copies, starts = [], []
        for e in range(n_exp):
            start = jnp.minimum((lowers[e] // 16) * 16, cap - win)
            row0 = pl.multiple_of((e * batch + bb) * cap + start, 16)
            copies.append(pltpu.make_async_copy(y_hbm.at[pl.ds(row0, win), :],
                                                ywin_ref.at[to_buf, pl.ds(e * win, win), :],
                                                sem_ref.at[to_buf]))
            starts.append(start)
        return copies, starts

    b, offs = block_offsets(i, 0)
    _, nxts = block_offsets(i, 1)
    most = functools.reduce(jnp.maximum, [nx - of for nx, of in zip(nxts, offs)])

    @pl.when(i == 0)
    def _():
        for cp in window_copies(b, offs, buf)[0]:
            cp.start()

    @pl.when(i + 1 < n_steps)
    def _():
        b_next, offs_next = block_offsets(i + 1, 0)
        for cp in window_copies(b_next, offs_next, 1 - buf)[0]:
            cp.start()

    o_ref[...] = x1_ref[...]
    posm = posm_ref[...]
    prow = jnp.concatenate([jnp.broadcast_to(posm[e:e + 1, :], (win, tb)) for e in range(n_exp)], axis=0)

    def scatter_round(lowers, fetch):
        copies, starts = window_copies(b, lowers, buf)
        if fetch:
            for cp in copies:
                cp.start()
        cols = []
        for e in range(n_exp):
            slot = starts[e] + lax.broadcasted_iota(jnp.int32, (win, 1), 0)
            cols.append(jnp.where((slot >= lowers[e]) & (slot < lowers[e] + adv), slot, -2).astype(F32))
        hit_t = jnp.where(prow == jnp.concatenate(cols, axis=0), 1.0, 0.0).astype(BF16)
        for cp in copies:
            cp.wait()
        o_ref[...] += lax.dot_general(hit_t, ywin_ref[buf], (((0,), (0,)), ((), ())),
                                      preferred_element_type=F32)

    scatter_round(offs, fetch=False)

    def extra_round(k, carry):
        scatter_round([of + k * adv for of in offs], fetch=True)
        return carry

    lax.fori_loop(1, (most + adv - 1) // adv, extra_round, 0)


def _ec_combine(boff_flat, x1, posm, y2d, *, cap, tb, win, adv):
    n, d = x1.shape
    batch, n_exp, s = posm.shape
    nblk = s // tb
    assert cap % 16 == 0 and win % 16 == 0 and win >= adv + 16 and cap >= win
    grid_spec = pltpu.PrefetchScalarGridSpec(
        num_scalar_prefetch=1,
        grid=(n // tb,),
        in_specs=[
            pl.BlockSpec((tb, d), lambda i, bo: (i, 0)),
            pl.BlockSpec((None, n_exp, tb), lambda i, bo: (i // nblk, 0, i % nblk)),
            pl.BlockSpec(memory_space=pl.ANY),
        ],
        out_specs=pl.BlockSpec((tb, d), lambda i, bo: (i, 0)),
        scratch_shapes=[pltpu.VMEM((2, n_exp * win, d), BF16), pltpu.SemaphoreType.DMA((2,))],
    )
    return pl.pallas_call(
        functools.partial(_combine_kernel, batch=batch, cap=cap, win=win, adv=adv),
        grid_spec=grid_spec,
        out_shape=jax.ShapeDtypeStruct((n, d), F32),
        compiler_params=_params("arbitrary"),
        name="ec_combine",
    )(boff_flat, x1, posm, y2d)


COMBINE_TOKENS = 256
COMBINE_WINDOW = 80
COMBINE_ADVANCE = 64
ROUTE_SLOT_TILE = 64


def _tiles(n, d, ff):
    pick = lambda total, want: want if total % want == 0 else total
    return dict(tm_in=pick(n, 1024), tn_in=1024, tq=pick(n, 512), tm_out=pick(n, 512),
                tf=pick(ff, 256))


def kernel(x, ln1_g, w_in, qn_a, kn_a, rpb_a, on_a, qn_b, kn_b, lam_q1, lam_k1, lam_q2, lam_k2,
           subln_b, w_out, ln2_g, w_router, w_gate, w_up, w_down):
    b, s, d = x.shape
    n = b * s
    depth = w_in.shape[0]
    n_exp = w_router.shape[-1]
    cap = EC_CAPACITY_FACTOR * s // n_exp
    t = _tiles(n, d, w_gate.shape[-1])
    slopes = 2.0 ** (-8.0 * jnp.arange(1, DA_HEADS + 1, dtype=F32) / DA_HEADS)
    ones_v = jnp.ones((NA_WIDTH,), F32)
    x2d = x.reshape(n, d)
    for l in range(depth):
        tile_heads = lambda g, heads: jnp.tile(g.astype(F32), heads)
        col_gain = jnp.concatenate([
            tile_heads(qn_a[l], NA_HEADS), tile_heads(kn_a[l], NA_HEADS), ones_v,
            tile_heads(qn_b[l], 2 * DA_HEADS), tile_heads(kn_b[l], 2 * DA_HEADS),
            jnp.ones((DA_WIDTH,), F32)]).reshape(1, IN_COLS)
        proj = _in_projection(x2d, ln1_g[l], w_in[l], col_gain,
                              tm=t["tm_in"], tn=t["tn_in"])
        proj3d = proj.reshape(b, s, IN_COLS)
        oa = _na_attention(proj3d, *_na_bias_table(rpb_a[l], qn_a[l], kn_a[l]))
        ob = _da_attention(proj3d, slopes, qn_b[l], kn_b[l], lam_q1[l], lam_k1[l], lam_q2[l], lam_k2[l], subln_b[l],
                           lam_init=_lambda_init(l), tq=min(t["tq"], s))
        x1, logits_t = _out_projection(
            oa.reshape(n, NA_WIDTH), ob.reshape(n, DA_WIDTH), x2d, on_a[l],
            w_out[l].astype(BF16), ln2_g[l], w_router[l].astype(F32), tm=t["tm_out"])
        idx, gate, posm, boff = _ec_route(logits_t, b, cap=cap, slot_tile=ROUTE_SLOT_TILE, tb=COMBINE_TOKENS)
        rows_flat = (idx[..., 0] + (jnp.arange(b, dtype=jnp.int32) * s)[None, :, None]).reshape(-1)
        gate_e = gate.reshape(n_exp, b * cap, 1)
        y = _ec_ffn(rows_flat, x1, ln2_g[l], gate_e, w_gate[l], w_up[l], w_down[l], tf=t["tf"])
        x2d = _ec_combine(boff.reshape(-1), x1, posm, y.reshape(n_exp * b * cap, d), cap=cap,
                          tb=COMBINE_TOKENS, win=COMBINE_WINDOW, adv=COMBINE_ADVANCE)
    return x2d.reshape(b, s, d)
```
